```python
import math
import jax, jax.numpy as jnp
from jax import lax
import numpy as np

D_MODEL = 1024
BATCH = 8
SEQ = 2048
DEPTH = 1
DEC_BATCH = 128
DEC_SEQ = 1
PAST_LEN = 8192
PAGE_SIZE = 128

HEAD_DIM = 64
A_HEADS = 8
A_KV_HEADS = 2
IDX_HEADS = 8
IDX_DIM = 64
IDX_TOPK = 256
B_HEADS = 8
B_KV_HEADS = 2
MOBA_BLOCK = 256
MOBA_TOPK = 3
D_FF = 2816
CONV_W = 3
ROPE_THETA = 10000.0
EPS = 1e-6
DSA_Q_BLOCK = 128
MOBA_Q_BLOCK = 16
NEG = -1e30

A_W = A_HEADS * HEAD_DIM
A_KV_W = A_KV_HEADS * HEAD_DIM
B_W = B_HEADS * HEAD_DIM
B_KV_W = B_KV_HEADS * HEAD_DIM
SPLITS = (A_W, A_KV_W, A_KV_W, IDX_HEADS * IDX_DIM, IDX_DIM, IDX_HEADS, B_W, B_KV_W, B_KV_W, D_MODEL, D_MODEL)
N_IN = sum(SPLITS)

kernel_name = 'hybrid_dsa_moba_convffn_step'


def rms_norm(x, g):
    xf = x.astype(jnp.float32)
    y = xf * lax.rsqrt(jnp.mean(xf * xf, axis=-1, keepdims=True) + EPS)
    return (y * g.astype(jnp.float32)).astype(x.dtype)


def rope(x, pos):
    half = x.shape[-1] // 2
    inv = ROPE_THETA ** (-jnp.arange(half, dtype=jnp.float32) / half)
    ang = pos.astype(jnp.float32)[:, None] * inv[None, :]
    cos = jnp.cos(ang)[None, :, None, :]
    sin = jnp.sin(ang)[None, :, None, :]
    xf = x.astype(jnp.float32)
    x1, x2 = xf[..., :half], xf[..., half:]
    return jnp.concatenate([x1 * cos - x2 * sin, x2 * cos + x1 * sin], axis=-1).astype(x.dtype)


def sweep_queries(fn, blk, q_arrays, q_pos):
    B, T = q_arrays[0].shape[:2]
    nb = T // blk
    xs = tuple(jnp.moveaxis(a.reshape((B, nb, blk) + a.shape[2:]), 1, 0) for a in q_arrays)
    out = lax.map(lambda a: fn(*a), xs + (q_pos.reshape(nb, blk),))
    return jnp.moveaxis(out, 0, 1).reshape((B, T) + out.shape[3:])


def dsa_attend(q, qi, wi, q_pos, k, v, ki, n_sel):
    B, T = q.shape[:2]
    L = k.shape[1]
    logits = jnp.einsum('bthd,bsd->btsh', qi, ki) * (IDX_DIM ** -0.5)
    score = jnp.einsum('btsh,bth->bts', jax.nn.relu(logits), wi).astype(jnp.float32) * (IDX_HEADS ** -0.5)
    admissible = jnp.arange(L)[None, :] <= q_pos[:, None]
    score = jnp.where(admissible[None], score, NEG)
    _, idx = lax.top_k(score, n_sel)
    valid = idx <= q_pos[None, :, None]
    bidx = jnp.arange(B)[:, None, None]
    ks = k[bidx, idx]
    vs = v[bidx, idx]
    qg = q.reshape(B, T, A_KV_HEADS, A_HEADS // A_KV_HEADS, HEAD_DIM)
    s = jnp.einsum('btkgd,btnkd->btkgn', qg, ks).astype(jnp.float32) * (HEAD_DIM ** -0.5)
    s = jnp.where(valid[:, :, None, None, :], s, NEG)
    p = jax.nn.softmax(s, axis=-1).astype(v.dtype)
    o = jnp.einsum('btkgn,btnkd->btkgd', p, vs)
    return o.reshape(B, T, A_W)


def moba_blocks(k, v):
    B, L = k.shape[:2]
    nb = -(-L // MOBA_BLOCK)
    pad = ((0, 0), (0, nb * MOBA_BLOCK - L), (0, 0), (0, 0))
    kb = jnp.pad(k, pad).reshape(B, nb, MOBA_BLOCK, B_KV_HEADS, HEAD_DIM).transpose(0, 3, 1, 2, 4)
    vb = jnp.pad(v, pad).reshape(B, nb, MOBA_BLOCK, B_KV_HEADS, HEAD_DIM).transpose(0, 3, 1, 2, 4)
    kbar = jnp.mean(kb.astype(jnp.float32), axis=3).astype(k.dtype)
    return kb, vb, kbar


def moba_attend(q, q_pos, kb, vb, kbar, n_sel):
    B, T = q.shape[:2]
    nb = kb.shape[2]
    G = B_HEADS // B_KV_HEADS
    qg = q.reshape(B, T, B_KV_HEADS, G, HEAD_DIM)
    own = q_pos // MOBA_BLOCK
    own_b = jnp.broadcast_to(own[None, :, None, None, None], (B, T, B_KV_HEADS, G, 1))
    if n_sel > 0:
        gate = jnp.einsum('btkgd,bknd->btkgn', qg, kbar).astype(jnp.float32)
        past = jnp.arange(nb)[None, :] < own[:, None]
        gate = jnp.where(past[None, :, None, None, :], gate, NEG)
        _, top = lax.top_k(gate, n_sel)
        sel = jnp.concatenate([top, own_b], axis=-1)
        sel_ok = jnp.concatenate([top < own[None, :, None, None, None], jnp.ones_like(own_b, dtype=bool)], axis=-1)
    else:
        sel = own_b
        sel_ok = jnp.ones_like(own_b, dtype=bool)
    bidx = jnp.arange(B)[:, None, None, None, None]
    kidx = jnp.arange(B_KV_HEADS)[None, None, :, None, None]
    ks = kb[bidx, kidx, sel]
    vs = vb[bidx, kidx, sel]
    key_pos = sel[..., None] * MOBA_BLOCK + jnp.arange(MOBA_BLOCK)
    mask = sel_ok[..., None] & (key_pos <= q_pos[None, :, None, None, None, None])
    s = jnp.einsum('btkgd,btkgsjd->btkgsj', qg, ks).astype(jnp.float32) * (HEAD_DIM ** -0.5)
    s = jnp.where(mask, s, NEG)
    shp = s.shape
    p = jax.nn.softmax(s.reshape(shp[:4] + (-1,)), axis=-1).reshape(shp).astype(vb.dtype)
    o = jnp.einsum('btkgsj,btkgsjd->btkgd', p, vs)
    return o.reshape(B, T, B_W)


def gather_pages(pool, page_table):
    g = pool[page_table]
    return g.reshape((g.shape[0], g.shape[1] * g.shape[2]) + g.shape[3:])


def layer_forward(x, pos, past, conv_prev, lp, sweep):
    (norm1, w_in, q_norm_a, k_norm_a, k_norm_idx, q_norm_b, k_norm_b,
     w_proj_a, w_proj_b, w_out, norm2, w_up, conv_w, conv_b, w_down) = lp
    past_ak, past_av, past_ki, past_bk, past_bv = past
    B, T, _ = x.shape
    h = rms_norm(x, norm1)
    offsets = np.cumsum(SPLITS)[:-1].tolist()
    qa, ka, va, qi, ki, wi, qb, kb, vb, ga, gb = jnp.split(h @ w_in, offsets, axis=-1)
    qa = rope(rms_norm(qa.reshape(B, T, A_HEADS, HEAD_DIM), q_norm_a), pos)
    ka = rope(rms_norm(ka.reshape(B, T, A_KV_HEADS, HEAD_DIM), k_norm_a), pos)
    va = va.reshape(B, T, A_KV_HEADS, HEAD_DIM)
    qi = rope(qi.reshape(B, T, IDX_HEADS, IDX_DIM), pos)
    ki = rope(rms_norm(ki, k_norm_idx)[:, :, None, :], pos)[:, :, 0]
    qb = rope(rms_norm(qb.reshape(B, T, B_HEADS, HEAD_DIM), q_norm_b), pos)
    kb = rope(rms_norm(kb.reshape(B, T, B_KV_HEADS, HEAD_DIM), k_norm_b), pos)
    vb = vb.reshape(B, T, B_KV_HEADS, HEAD_DIM)

    ka_all = jnp.concatenate([past_ak, ka], axis=1)
    va_all = jnp.concatenate([past_av, va], axis=1)
    ki_all = jnp.concatenate([past_ki, ki], axis=1)
    L = ka_all.shape[1]
    n_idx = min(IDX_TOPK, L // 4)
    dsa = lambda q_, qi_, wi_, p_: dsa_attend(q_, qi_, wi_, p_, ka_all, va_all, ki_all, n_idx)
    o_a = sweep_queries(dsa, DSA_Q_BLOCK, (qa, qi, wi), pos) if sweep else dsa(qa, qi, wi, pos)

    kb_blk, vb_blk, kbar = moba_blocks(jnp.concatenate([past_bk, kb], axis=1), jnp.concatenate([past_bv, vb], axis=1))
    n_blk = min(MOBA_TOPK, (L - 1) // MOBA_BLOCK)
    moba = lambda q_, p_: moba_attend(q_, p_, kb_blk, vb_blk, kbar, n_blk)
    o_b = sweep_queries(moba, MOBA_Q_BLOCK, (qb,), pos) if sweep else moba(qb, pos)

    merged = jax.nn.sigmoid(ga) * (o_a @ w_proj_a) + jax.nn.sigmoid(gb) * (o_b @ w_proj_b)
    x = x + merged @ w_out

    up = rms_norm(x, norm2) @ w_up
    ext = jnp.concatenate([conv_prev, up], axis=1)
    conv = conv_b + ext[:, 0:T] * conv_w[0]
    for j in range(1, CONV_W):
        conv = conv + ext[:, j:j + T] * conv_w[j]
    a, b = jnp.split(conv, 2, axis=-1)
    x = x + (jax.nn.silu(a) * b) @ w_down
    return x, (ka, va, ki, kb, vb, ext[:, T:])


def setup_inputs(seed: int = 0) -> dict:
    key = jax.random.key(seed)
    ks = jax.random.split(key, 32)
    f32 = jnp.float32
    n_pages = PAST_LEN // PAGE_SIZE
    n_used = DEC_BATCH * n_pages
    n_pool = n_used + max(1, n_used // 4)

    def nrm(k, shape, scale=1.0):
        return scale * jax.random.normal(k, shape, f32)

    def gain(k, n):
        return 1.0 + 0.05 * jax.random.normal(k, (DEPTH, n), f32)

    page_table = jax.random.permutation(ks[0], n_pool)[:n_used].reshape(DEC_BATCH, n_pages).astype(jnp.int32)
    return {
        'x_prompt': nrm(ks[1], (BATCH, SEQ, D_MODEL)),
        'x_sample': nrm(ks[2], (DEC_BATCH, DEC_SEQ, D_MODEL)),
        'cache_a_k': nrm(ks[3], (DEPTH, n_pool, PAGE_SIZE, A_KV_HEADS, HEAD_DIM)),
        'cache_a_v': nrm(ks[4], (DEPTH, n_pool, PAGE_SIZE, A_KV_HEADS, HEAD_DIM)),
        'cache_idx_k': nrm(ks[5], (DEPTH, n_pool, PAGE_SIZE, IDX_DIM)),
        'cache_b_k': nrm(ks[6], (DEPTH, n_pool, PAGE_SIZE, B_KV_HEADS, HEAD_DIM)),
        'cache_b_v': nrm(ks[7], (DEPTH, n_pool, PAGE_SIZE, B_KV_HEADS, HEAD_DIM)),
        'state_conv': nrm(ks[8], (DEPTH, DEC_BATCH, CONV_W - 1, 2 * D_FF)),
        'page_table': page_table,
        'norm1': gain(ks[9], D_MODEL),
        'w_in': nrm(ks[10], (DEPTH, D_MODEL, N_IN), D_MODEL ** -0.5),
        'q_norm_a': gain(ks[11], HEAD_DIM),
        'k_norm_a': gain(ks[12], HEAD_DIM),
        'k_norm_idx': gain(ks[13], IDX_DIM),
        'q_norm_b': gain(ks[14], HEAD_DIM),
        'k_norm_b': gain(ks[15], HEAD_DIM),
        'w_proj_a': nrm(ks[16], (DEPTH, A_W, D_MODEL), A_W ** -0.5),
        'w_proj_b': nrm(ks[17], (DEPTH, B_W, D_MODEL), B_W ** -0.5),
        'w_out': nrm(ks[18], (DEPTH, D_MODEL, D_MODEL), D_MODEL ** -0.5),
        'norm2': gain(ks[19], D_MODEL),
        'w_up': nrm(ks[20], (DEPTH, D_MODEL, 2 * D_FF), D_MODEL ** -0.5),
        'conv_w': nrm(ks[21], (DEPTH, CONV_W, 2 * D_FF), CONV_W ** -0.5),
        'conv_b': nrm(ks[22], (DEPTH, 2 * D_FF), 0.01),
        'w_down': nrm(ks[23], (DEPTH, D_FF, D_MODEL), D_FF ** -0.5),
    }


def reference(x_prompt, x_sample, cache_a_k, cache_a_v, cache_idx_k, cache_b_k, cache_b_v, state_conv, page_table,
              norm1, w_in, q_norm_a, k_norm_a, k_norm_idx, q_norm_b, k_norm_b, w_proj_a, w_proj_b, w_out,
              norm2, w_up, conv_w, conv_b, w_down):
    n_p = x_prompt.shape[0]
    pos_p = jnp.arange(x_prompt.shape[1], dtype=jnp.int32)
    pos_s = PAST_LEN + jnp.arange(x_sample.shape[1], dtype=jnp.int32)
    dt = x_prompt.dtype
    yp, ys = x_prompt, x_sample
    rows_p, rows_s = [], []
    for l in range(DEPTH):
        lp = (norm1[l], w_in[l], q_norm_a[l], k_norm_a[l], k_norm_idx[l], q_norm_b[l], k_norm_b[l],
              w_proj_a[l], w_proj_b[l], w_out[l], norm2[l], w_up[l], conv_w[l], conv_b[l], w_down[l])
        empty = (jnp.zeros((n_p, 0, A_KV_HEADS, HEAD_DIM), dt), jnp.zeros((n_p, 0, A_KV_HEADS, HEAD_DIM), dt),
                 jnp.zeros((n_p, 0, IDX_DIM), dt), jnp.zeros((n_p, 0, B_KV_HEADS, HEAD_DIM), dt),
                 jnp.zeros((n_p, 0, B_KV_HEADS, HEAD_DIM), dt))
        yp, rp = layer_forward(yp, pos_p, empty, jnp.zeros((n_p, CONV_W - 1, 2 * D_FF), dt), lp, True)
        past = (gather_pages(cache_a_k[l], page_table), gather_pages(cache_a_v[l], page_table),
                gather_pages(cache_idx_k[l], page_table), gather_pages(cache_b_k[l], page_table),
                gather_pages(cache_b_v[l], page_table))
        ys, rs = layer_forward(ys, pos_s, past, state_conv[l], lp, False)
        rows_p.append(rp)
        rows_s.append(rs)
    p_a_k, p_a_v, p_idx_k, p_b_k, p_b_v, p_conv = [jnp.stack(t) for t in zip(*rows_p)]
    s_a_k, s_a_v, s_idx_k, s_b_k, s_b_v, s_conv = [jnp.stack(t) for t in zip(*rows_s)]
    return (yp, ys, p_a_k, p_a_v, p_idx_k, p_b_k, p_b_v, p_conv, s_a_k, s_a_v, s_idx_k, s_b_k, s_b_v, s_conv)
```

```python
import functools

import jax
import jax.numpy as jnp
from jax import lax
from jax.experimental import pallas as pl
from jax.experimental.pallas import tpu as pltpu

HEAD_DIM = 64
A_HEADS = 8
A_KV_HEADS = 2
IDX_HEADS = 8
IDX_DIM = 64
IDX_TOPK = 256
B_HEADS = 8
B_KV_HEADS = 2
MOBA_BLOCK = 256
MOBA_TOPK = 3
ROPE_THETA = 10000.0
EPS = 1e-6
NEG = -1e30
BIG = 3e38

LANE = 128
Q_BLOCK = 256
ROW_BLOCK = 256
VMEM_LIMIT = 56 * 1024 * 1024
BISECT_STEPS = 16

F32 = jnp.float32
BF16 = jnp.bfloat16

A_W = A_HEADS * HEAD_DIM
A_KV_W = A_KV_HEADS * HEAD_DIM
I_W = IDX_HEADS * IDX_DIM
B_W = B_HEADS * HEAD_DIM
B_KV_W = B_KV_HEADS * HEAD_DIM
assert A_W == 512 and I_W == 512 and B_W == 512 and A_KV_W == LANE and B_KV_W == LANE


def _params(n_grid):
    return pltpu.CompilerParams(dimension_semantics=("arbitrary",) * n_grid,
                                vmem_limit_bytes=VMEM_LIMIT)


def _dot(a, b):
    return jnp.dot(a, b, preferred_element_type=F32)


def _dot_nt(a, b):
    return lax.dot_general(a, b, (((1,), (1,)), ((), ())), preferred_element_type=F32)


def _iota(shape, dim):
    return lax.broadcasted_iota(jnp.int32, shape, dim)


def _rms(x, g):
    return x * lax.rsqrt(jnp.mean(x * x, axis=-1, keepdims=True) + EPS) * g


def _proj_kernel(x_ref, n1_ref, w_ref, g_ref, cos_ref, sin_ref,
                 qa_ref, ka_ref, va_ref, qi_ref, qb_ref, kb_ref, vb_ref, sga_ref, sgb_ref, kw_ref,
                 *, d_model):
    tm = x_ref.shape[0]
    h = _rms(x_ref[...], n1_ref[...]).astype(BF16)
    gr = lax.shift_right_logical(_iota((2 * LANE, 2 * LANE), 0), 6)
    gc = lax.shift_right_logical(_iota((2 * LANE, 2 * LANE), 1), 6)
    gsum = jnp.where(gr == gc, 1.0, 0.0).astype(BF16)
    cos1 = cos_ref[...]
    sin1 = sin_ref[...]

    def seg(off, width):
        return _dot(h, w_ref[:, off:off + width])

    def head_norm(x, off):
        width = x.shape[1]
        ss = _dot((x * x).astype(BF16), gsum[:width, :width])
        return x * lax.rsqrt(ss * (1.0 / HEAD_DIM) + EPS) * g_ref[:, off:off + width]

    def rope(x):
        width = x.shape[1]
        rep = width // LANE
        cs = jnp.concatenate([cos1] * rep, axis=1) if rep > 1 else cos1
        sn = jnp.concatenate([sin1] * rep, axis=1) if rep > 1 else sin1
        hi = (_iota(x.shape, 1) & (HEAD_DIM // 2)) != 0
        swapped = jnp.where(hi, pltpu.roll(x, HEAD_DIM // 2, axis=1),
                            pltpu.roll(x, width - HEAD_DIM // 2, axis=1))
        return x * cs + swapped * sn

    o = 0
    for half in range(2):
        x = seg(o + half * 256, 256)
        qa_ref[:, half * 256:(half + 1) * 256] = rope(head_norm(x, o + half * 256))
    o = A_W
    x = seg(o, 256)
    xn = rope(head_norm(x, o))
    ka_ref[...] = xn[:, :LANE]
    va_ref[...] = x[:, LANE:]
    o = A_W + 2 * LANE
    for half in range(2):
        qi_ref[:, half * 256:(half + 1) * 256] = rope(seg(o + half * 256, 256))
    o = A_W + 2 * LANE + I_W
    for half in range(2):
        x = seg(o + half * 256, 256)
        qb_ref[:, half * 256:(half + 1) * 256] = rope(head_norm(x, o + half * 256))
    o = A_W + 2 * LANE + I_W + B_W
    x = seg(o, 256)
    xn = rope(head_norm(x, o))
    kb_ref[...] = xn[:, :LANE]
    vb_ref[...] = x[:, LANE:]
    o = A_W + 2 * LANE + I_W + B_W + 2 * LANE
    for part in range(d_model // 256):
        x = seg(o + part * 256, 256)
        sga_ref[:, part * 256:(part + 1) * 256] = 1.0 / (1.0 + jnp.exp(-x))
    o += d_model
    for part in range(d_model // 256):
        x = seg(o + part * 256, 256)
        sgb_ref[:, part * 256:(part + 1) * 256] = 1.0 / (1.0 + jnp.exp(-x))
    o += d_model
    x = seg(o, LANE)
    xn = rope(head_norm(x, o))
    kw_ref[...] = jnp.where(_iota(x.shape, 1) < IDX_DIM, xn, x)
    del tm


def _proj(x2d, n1, w, gains, cos, sin, rows_per_seq):
    n, d = x2d.shape
    tm = min(ROW_BLOCK, n)
    nt = rows_per_seq // tm if rows_per_seq >= tm else 1
    nw = w.shape[1]
    row = lambda c: pl.BlockSpec((tm, c), lambda i: (i, 0))
    const = lambda shape: pl.BlockSpec(shape, lambda i: (0, 0))
    tab = pl.BlockSpec((tm, LANE), lambda i: (i % nt, 0))
    outs = [(n, A_W), (n, LANE), (n, LANE), (n, I_W), (n, B_W), (n, LANE), (n, LANE), (n, d), (n, d), (n, LANE)]
    return pl.pallas_call(
        functools.partial(_proj_kernel, d_model=d),
        grid=(n // tm,),
        in_specs=[row(d), const((1, d)), const((d, nw)), const((1, nw)), tab, tab],
        out_specs=[row(s[1]) for s in outs],
        out_shape=[jax.ShapeDtypeStruct(s, F32) for s in outs],
        compiler_params=_params(1),
        name="proj",
    )(x2d, n1, w, gains, cos, sin)


def _head_operand(q_ref, h, want_low, scale):
    ch = q_ref[:, LANE * (h // 2):LANE * (h // 2) + LANE]
    if (h % 2 == 0) != want_low:
        ch = pltpu.roll(ch, HEAD_DIM, axis=1)
    low = _iota(ch.shape, 1) < HEAD_DIM
    keep = low if want_low else jnp.logical_not(low)
    return (jnp.where(keep, ch, 0.0) * scale).astype(BF16)


def _fill_kv(k_ref, v_ref, kb_s, vt_s, nchunk, tc):
    kb_s[...] = k_ref[...].astype(BF16)
    for c in range(nchunk):
        vt = v_ref[c * tc:(c + 1) * tc, :].T
        for kv in range(2):
            vt_s[kv, c] = vt[kv * HEAD_DIM:(kv + 1) * HEAD_DIM, :].astype(BF16)


def _online_softmax_step(s, h, kv, c, vt_s, m_s, l_s, acc_s):
    m_old = m_s[h]
    m_new = jnp.maximum(m_old, jnp.max(s, axis=0, keepdims=True))
    alpha = jnp.exp(m_old - m_new)
    p = jnp.exp(s - m_new)
    l_s[h] = alpha * l_s[h] + jnp.sum(p, axis=0, keepdims=True)
    acc_s[h] = alpha * acc_s[h] + _dot(vt_s[kv, c], p.astype(BF16))
    m_s[h] = m_new


def _init_softmax(m_s, l_s, acc_s):
    m_s[...] = jnp.full(m_s.shape, NEG, F32)
    l_s[...] = jnp.zeros(l_s.shape, F32)
    acc_s[...] = jnp.zeros(acc_s.shape, F32)


def _write_heads(o_ref, l_s, acc_s, n_heads):
    for j in range(n_heads // 2):
        o0 = acc_s[2 * j] / l_s[2 * j]
        o1 = acc_s[2 * j + 1] / l_s[2 * j + 1]
        o_ref[:, LANE * j:LANE * (j + 1)] = jnp.concatenate([o0, o1], axis=0).T


def _select_topk(sweep, count_gt, n_sel, small, key_index, tq, t_total):
    row = lambda v: jnp.full((1, tq), v, F32)
    rowmax = sweep(lambda s, c, a: jnp.maximum(a, jnp.max(s, axis=0, keepdims=True)), row(NEG))
    rowmin = sweep(lambda s, c, a: jnp.minimum(
        a, jnp.min(jnp.where(s > 0.5 * NEG, s, BIG), axis=0, keepdims=True)), row(BIG))

    def bisect(_, lh):
        lo, hi = lh
        mid = 0.5 * (lo + hi)
        ge = count_gt(mid) >= n_sel
        return jnp.where(ge, mid, lo), jnp.where(ge, hi, mid)

    lo, _ = lax.fori_loop(0, BISECT_STEPS, bisect, (rowmin, rowmax))
    u0 = sweep(lambda s, c, a: jnp.minimum(
        a, jnp.min(jnp.where(s >= lo, s, BIG), axis=0, keepdims=True)), row(BIG))

    def walk_cond(st):
        return st[2] == 0

    def walk(st):
        u = st[0]

        def f(s, c, carry):
            cnt, nxt = carry
            gt = s > u
            cnt = cnt + jnp.sum(jnp.where(gt, 1.0, 0.0), axis=0, keepdims=True)
            nxt = jnp.minimum(nxt, jnp.min(jnp.where(gt, s, BIG), axis=0, keepdims=True))
            return cnt, nxt

        cnt, nxt = sweep(f, (row(0.0), row(BIG)))
        done = jnp.logical_or(cnt < n_sel, small)
        all_done = jnp.min(jnp.where(done, 1.0, 0.0)).astype(jnp.int32)
        return jnp.where(done, u, nxt), cnt, all_done

    u, cgt, _ = lax.while_loop(walk_cond, walk, (u0, row(0.0), jnp.int32(0)))
    vstar = jnp.where(small, NEG, u)
    need = jnp.where(small, 0.0, n_sel - cgt)
    n_eq = sweep(lambda s, c, a: a + jnp.sum(jnp.where(s == vstar, 1.0, 0.0), axis=0, keepdims=True), row(0.0))
    excess = jnp.max(jnp.where(jnp.logical_and(n_eq > need, jnp.logical_not(small)), 1.0, 0.0))

    n_jsteps = jnp.where(excess > 0.0, t_total.bit_length() + 1, 0).astype(jnp.int32)

    def jstep(_, jj):
        jlo, jhi = jj
        mid = lax.shift_right_logical(jlo + jhi, 1)
        e = sweep(lambda s, c, a: a + jnp.sum(
            jnp.where(s == vstar, jnp.where(key_index(c) < mid, 1.0, 0.0), 0.0), axis=0, keepdims=True), row(0.0))
        ge = e >= need
        return jnp.where(ge, jlo, mid), jnp.where(ge, mid, jhi)

    zero_i = jnp.zeros((1, tq), jnp.int32)
    _, jcut = lax.fori_loop(0, n_jsteps, jstep, (zero_i, zero_i + t_total))
    return vstar, jnp.where(small, 0, jcut)


def _dsa_prompt_kernel(qi_ref, qa_ref, kwq_ref, kwk_ref, ka_ref, va_ref, o_ref,
                       kib_s, kab_s, vt_s, qim_s, qam_s, sc_s, m_s, l_s, acc_s,
                       *, n_sel, t_total):
    tq = qi_ref.shape[0]
    tc = tq
    nchunk = t_total // tc
    i = pl.program_id(1)

    @pl.when(i == 0)
    def _():
        kib_s[...] = kwk_ref[...].astype(BF16)
        _fill_kv(ka_ref, va_ref, kab_s, vt_s, nchunk, tc)

    heads_per_kv = A_HEADS // A_KV_HEADS
    for h in range(IDX_HEADS):
        qim_s[h] = _head_operand(qi_ref, h, True, 1.0)
    for h in range(A_HEADS):
        qam_s[h] = _head_operand(qa_ref, h, (h // heads_per_kv) == 0, HEAD_DIM ** -0.5)
    w8 = kwq_ref[...].T[IDX_DIM:IDX_DIM + IDX_HEADS, :] * (IDX_DIM ** -0.5 * IDX_HEADS ** -0.5)

    krow = _iota((tc, tq), 0)
    qcol = _iota((tc, tq), 1)

    def chunk(c):
        return pl.ds(pl.multiple_of(c * tc, tc), tc)

    def score_chunk(c, diag):
        kc = kib_s[chunk(c), :]
        a = jnp.zeros((tc, tq), F32)
        for h in range(IDX_HEADS):
            a = a + jnp.maximum(_dot_nt(kc, qim_s[h]), 0.0) * w8[h:h + 1, :]
        if diag:
            a = jnp.where(krow <= qcol, a, NEG)
        sc_s[chunk(c), :] = a

    def score_body(c, carry):
        score_chunk(c, False)
        return carry

    lax.fori_loop(0, i, score_body, 0)
    score_chunk(i, True)

    def sweep(fn, init):
        return lax.fori_loop(0, i + 1, lambda c, carry: fn(sc_s[chunk(c), :], c, carry), init)

    def count_gt(x):
        return sweep(lambda s, c, a: a + jnp.sum(jnp.where(s > x, 1.0, 0.0), axis=0, keepdims=True),
                     jnp.zeros((1, tq), F32))

    n_adm = i * tq + _iota((1, tq), 1) + 1
    small = n_adm <= n_sel
    vstar, jcut = _select_topk(sweep, count_gt, float(n_sel), small,
                               lambda c: c * tc + krow, tq, t_total)

    def bias_body(c, carry):
        s = sc_s[chunk(c), :]
        tie = jnp.where((c * tc + krow) < jcut, 0.0, NEG)
        sc_s[chunk(c), :] = jnp.where(s > vstar, 0.0, jnp.where(s == vstar, tie, NEG))
        return carry

    lax.fori_loop(0, i + 1, bias_body, 0)

    _init_softmax(m_s, l_s, acc_s)

    def att_body(c, carry):
        kc = kab_s[chunk(c), :]
        bias = sc_s[chunk(c), :]
        for h in range(A_HEADS):
            s = _dot_nt(kc, qam_s[h]) + bias
            _online_softmax_step(s, h, h // heads_per_kv, c, vt_s, m_s, l_s, acc_s)
        return carry

    lax.fori_loop(0, i + 1, att_body, 0)
    _write_heads(o_ref, l_s, acc_s, A_HEADS)


def _dsa_prompt(qi, qa, kw, ka, va, batch, t):
    tq = Q_BLOCK
    nq = t // tq
    n_sel = min(IDX_TOPK, t // 4)
    qblk = lambda c: pl.BlockSpec((tq, c), lambda b, i: (b * nq + i, 0))
    full = pl.BlockSpec((t, LANE), lambda b, i: (b, 0))
    return pl.pallas_call(
        functools.partial(_dsa_prompt_kernel, n_sel=n_sel, t_total=t),
        grid=(batch, nq),
        in_specs=[qblk(I_W), qblk(A_W), qblk(LANE), full, full, full],
        out_specs=qblk(A_W),
        out_shape=jax.ShapeDtypeStruct((batch * t, A_W), F32),
        scratch_shapes=[
            pltpu.VMEM((t, LANE), BF16), pltpu.VMEM((t, LANE), BF16),
            pltpu.VMEM((2, nq, HEAD_DIM, tq), BF16),
            pltpu.VMEM((IDX_HEADS, tq, LANE), BF16), pltpu.VMEM((A_HEADS, tq, LANE), BF16),
            pltpu.VMEM((t, tq), F32),
            pltpu.VMEM((A_HEADS, 1, tq), F32), pltpu.VMEM((A_HEADS, 1, tq), F32),
            pltpu.VMEM((A_HEADS, HEAD_DIM, tq), F32),
        ],
        compiler_params=_params(2),
        name="dsa_prompt",
    )(qi, qa, kw, kw, ka, va)


def _moba_prompt_kernel(qb_ref, kb_ref, vb_ref, o_ref,
                        kbb_s, vt_s, kbar_s, qbm_s, selb_s, m_s, l_s, acc_s, *, n_blk, t_total):
    tq = qb_ref.shape[0]
    tc = tq
    nb = t_total // tc
    nbp = kbar_s.shape[0]
    i = pl.program_id(1)
    heads_per_kv = B_HEADS // B_KV_HEADS

    @pl.when(i == 0)
    def _():
        _fill_kv(kb_ref, vb_ref, kbb_s, vt_s, nb, tc)
        kbar_s[...] = jnp.zeros(kbar_s.shape, F32)
        for n in range(nb):
            kbar_s[n:n + 1, :] = jnp.mean(kb_ref[n * tc:(n + 1) * tc, :], axis=0, keepdims=True)

    kbar = kbar_s[...].astype(BF16)
    blk = _iota((nbp, tq), 0)
    past = blk < i
    for h in range(B_HEADS):
        qm = _head_operand(qb_ref, h, (h // heads_per_kv) == 0, HEAD_DIM ** -0.5)
        qbm_s[h] = qm
        gate = jnp.where(past, _dot_nt(kbar, qm), NEG)
        rank = jnp.zeros((nbp, tq), F32)
        for m in range(nb):
            gm = gate[m:m + 1, :]
            first = jnp.where(blk > m, 1.0, 0.0)
            rank = rank + jnp.where(gm > gate, 1.0, jnp.where(gm == gate, first, 0.0))
        sel = jnp.logical_and(past, rank < n_blk)
        selb_s[h] = jnp.where(sel, 0.0, NEG)

    _init_softmax(m_s, l_s, acc_s)
    krow = _iota((tc, tq), 0)
    qcol = _iota((tc, tq), 1)
    causal = jnp.where(krow <= qcol, 0.0, NEG)

    def chunk(c):
        return pl.ds(pl.multiple_of(c * tc, tc), tc)

    kc_own = kbb_s[chunk(i), :]
    for h in range(B_HEADS):
        s = _dot_nt(kc_own, qbm_s[h]) + causal
        _online_softmax_step(s, h, h // heads_per_kv, i, vt_s, m_s, l_s, acc_s)

    def att_body(c, carry):
        kc = kbb_s[chunk(c), :]
        for h in range(B_HEADS):
            s = _dot_nt(kc, qbm_s[h]) + selb_s[h, pl.ds(c, 1), :]
            _online_softmax_step(s, h, h // heads_per_kv, c, vt_s, m_s, l_s, acc_s)
        return carry

    lax.fori_loop(0, i, att_body, 0)
    _write_heads(o_ref, l_s, acc_s, B_HEADS)


def _moba_prompt(qb, kb, vb, batch, t):
    tq = Q_BLOCK
    assert tq == MOBA_BLOCK and t % tq == 0
    nq = t // tq
    n_blk = min(MOBA_TOPK, (t - 1) // MOBA_BLOCK)
    qblk = pl.BlockSpec((tq, B_W), lambda b, i: (b * nq + i, 0))
    full = pl.BlockSpec((t, LANE), lambda b, i: (b, 0))
    return pl.pallas_call(
        functools.partial(_moba_prompt_kernel, n_blk=float(n_blk), t_total=t),
        grid=(batch, nq),
        in_specs=[qblk, full, full],
        out_specs=qblk,
        out_shape=jax.ShapeDtypeStruct((batch * t, B_W), F32),
        scratch_shapes=[
            pltpu.VMEM((t, LANE), BF16),
            pltpu.VMEM((2, nq, HEAD_DIM, tq), BF16),
            pltpu.VMEM((max(nq, 8), LANE), F32),
            pltpu.VMEM((B_HEADS, tq, LANE), BF16),
            pltpu.VMEM((B_HEADS, max(nq, 8), tq), F32),
            pltpu.VMEM((B_HEADS, 1, tq), F32), pltpu.VMEM((B_HEADS, 1, tq), F32),
            pltpu.VMEM((B_HEADS, HEAD_DIM, tq), F32),
        ],
        compiler_params=_params(2),
        name="moba_prompt",
    )(qb, kb, vb)


def _page_copy(cache, buf, sem, k, slot, page, p):
    rows, width = cache.shape[1], cache.shape[2]
    return pltpu.make_async_copy(cache.at[page], buf.at[slot, pl.ds(0, rows), pl.ds(p * width, width)],
                                 sem.at[k, slot])


def _gather(pt_ref, b, caches, bufs, sem, slot, n_pages, start):
    for k, (cache, buf) in enumerate(zip(caches, bufs)):
        for p in range(n_pages):
            cp = _page_copy(cache, buf, sem, k, slot, pt_ref[b, p] if start else 0, p)
            if start:
                cp.start()
            else:
                cp.wait()


def _prefetch_pages(pt_ref, caches, bufs, sem, n_pages, before_first=None):
    b = pl.program_id(0)
    nb = pl.num_programs(0)
    slot = lax.rem(b, 2)

    @pl.when(b == 0)
    def _():
        if before_first is not None:
            before_first()
        _gather(pt_ref, b, caches, bufs, sem, slot, n_pages, True)

    @pl.when(b + 1 < nb)
    def _():
        _gather(pt_ref, b + 1, caches, bufs, sem, 1 - slot, n_pages, True)

    _gather(pt_ref, b, caches, bufs, sem, slot, n_pages, False)
    return slot


def _sample_softmax(tiles, s_new, vt_tiles, v_new):
    m = s_new
    for s in tiles:
        m = jnp.maximum(m, jnp.max(s, axis=1, keepdims=True))
    p_new = jnp.exp(s_new - m)
    l = p_new
    o = p_new * v_new
    for s, vt in zip(tiles, vt_tiles):
        p = jnp.exp(s - m)
        l = l + jnp.sum(p, axis=1, keepdims=True)
        o = o + _dot_nt(p.astype(BF16), vt)
    return o / l


N_KEY_CHUNKS = 8


def _dsa_sample_kernel(pt_ref, qi_ref, wi_ref, qbd_ref, kin_ref, kan_ref, van_ref,
                       cik, cak, cav, o_ref, bik, bak, bav, sem, sc_s, *, n_pages, n_sel, past):
    def zero_pad_rows():
        bik[...] = jnp.zeros(bik.shape, F32)

    slot = _prefetch_pages(pt_ref, (cik, cak, cav), (bik, bak, bav), sem, n_pages, zero_pad_rows)
    nq = N_KEY_CHUNKS
    ch = past // nq
    qi = qi_ref[...]
    qib = qi.astype(BF16)
    wrow = wi_ref[...] * (IDX_DIM ** -0.5 * IDX_HEADS ** -0.5)
    eye = _iota((IDX_HEADS, IDX_HEADS), 0) == _iota((IDX_HEADS, IDX_HEADS), 1)
    wcol = jnp.sum(jnp.where(eye, jnp.broadcast_to(wrow, (IDX_HEADS, IDX_HEADS)), 0.0), axis=1, keepdims=True)
    lg_new = jnp.sum(qi * kin_ref[...], axis=1, keepdims=True)
    s_new = jnp.sum(jnp.maximum(lg_new, 0.0) * wcol, axis=0, keepdims=True)

    for q in range(nq):
        kq = bik[slot, :, q * ch:(q + 1) * ch].astype(BF16)
        lg = jnp.maximum(_dot(qib, kq), 0.0)
        sc_s[q:q + 1, :] = jnp.sum(lg * wcol, axis=0, keepdims=True)

    sc = sc_s[...]
    kidx = _iota((nq, ch), 0) * ch + _iota((nq, ch), 1)

    def total(x):
        return jnp.sum(jnp.sum(x, axis=1, keepdims=True), axis=0, keepdims=True)

    def tmin(x):
        return jnp.min(jnp.min(x, axis=1, keepdims=True), axis=0, keepdims=True)

    def count_gt(x):
        return total(jnp.where(sc > x, 1.0, 0.0)) + jnp.where(s_new > x, 1.0, 0.0)

    hi0 = jnp.maximum(-tmin(-sc), s_new)
    lo0 = jnp.minimum(tmin(sc), s_new)

    def bisect(_, lh):
        lo, hi = lh
        mid = 0.5 * (lo + hi)
        ge = count_gt(mid) >= n_sel
        return jnp.where(ge, mid, lo), jnp.where(ge, hi, mid)

    lo, _ = lax.fori_loop(0, BISECT_STEPS, bisect, (lo0, hi0))
    u0 = jnp.minimum(tmin(jnp.where(sc >= lo, sc, BIG)), jnp.where(s_new >= lo, s_new, BIG))

    def walk_cond(st):
        return st[2] == 0

    def walk(st):
        u = st[0]
        cnt = count_gt(u)
        nxt = jnp.minimum(tmin(jnp.where(sc > u, sc, BIG)), jnp.where(s_new > u, s_new, BIG))
        done = cnt < n_sel
        return jnp.where(done, u, nxt), cnt, jnp.min(jnp.where(done, 1.0, 0.0)).astype(jnp.int32)

    vstar, cgt, _ = lax.while_loop(walk_cond, walk, (u0, jnp.zeros((1, 1), F32), jnp.int32(0)))
    need = n_sel - cgt
    n_eq = total(jnp.where(sc == vstar, 1.0, 0.0)) + jnp.where(s_new == vstar, 1.0, 0.0)
    n_jsteps = jnp.where(jnp.max(n_eq - need) > 0.0, (past + 1).bit_length() + 1, 0).astype(jnp.int32)

    def jstep(_, jj):
        jlo, jhi = jj
        mid = lax.shift_right_logical(jlo + jhi, 1)
        e = total(jnp.where(sc == vstar, jnp.where(kidx < mid, 1.0, 0.0), 0.0)) + jnp.where(
            s_new == vstar, jnp.where(past < mid, 1.0, 0.0), 0.0)
        ge = e >= need
        return jnp.where(ge, jlo, mid), jnp.where(ge, mid, jhi)

    zero_i = jnp.zeros((1, 1), jnp.int32)
    _, jcut = lax.fori_loop(0, n_jsteps, jstep, (zero_i, zero_i + (past + 1)))
    bias = jnp.where(sc > vstar, 0.0, jnp.where(sc == vstar, jnp.where(kidx < jcut, 0.0, NEG), NEG))
    bias_new = jnp.where(s_new > vstar, 0.0, jnp.where(s_new == vstar, jnp.where(past < jcut, 0.0, NEG), NEG))

    qs = (qbd_ref[...] * (HEAD_DIM ** -0.5))
    sn = jnp.sum(qs * kan_ref[...], axis=1, keepdims=True) + bias_new
    qsb = qs.astype(BF16)
    tiles, vts = [], []
    for q in range(nq):
        kq = bak[slot, :, q * ch:(q + 1) * ch].astype(BF16)
        tiles.append(_dot(qsb, kq) + bias[q:q + 1, :])
        vts.append(bav[slot, :, q * ch:(q + 1) * ch].astype(BF16))
    o_ref[...] = _sample_softmax(tiles, sn, vts, van_ref[...])


def _moba_sample_kernel(pt_ref, qbd_ref, kbn_ref, vbn_ref, cbk, cbv, o_ref, bbk, bbv, sem,
                        *, n_pages, n_blk, past):
    slot = _prefetch_pages(pt_ref, (cbk, cbv), (bbk, bbv), sem, n_pages)
    nq = N_KEY_CHUNKS
    ch = past // nq
    nblk = past // MOBA_BLOCK
    bpc = ch // MOBA_BLOCK
    qs = qbd_ref[...] * (HEAD_DIM ** -0.5)
    qsb = qs.astype(BF16)
    sn = jnp.sum(qs * kbn_ref[...], axis=1, keepdims=True)
    bcol = _iota((B_HEADS, nblk), 1).astype(F32)

    raw, vts = [], []
    gate = jnp.zeros((B_HEADS, nblk), F32)
    for q in range(nq):
        s = _dot(qsb, bbk[slot, :, q * ch:(q + 1) * ch].astype(BF16))
        raw.append(s)
        vts.append(bbv[slot, :, q * ch:(q + 1) * ch].astype(BF16))
        for k in range(bpc):
            g = jnp.sum(s[:, k * MOBA_BLOCK:(k + 1) * MOBA_BLOCK], axis=1, keepdims=True) * (1.0 / MOBA_BLOCK)
            gate = jnp.where(bcol == float(q * bpc + k), g, gate)
    selm = jnp.zeros((B_HEADS, nblk), F32)
    for _ in range(int(n_blk)):
        mx = jnp.max(gate, axis=1, keepdims=True)
        first = jnp.min(jnp.where(gate == mx, bcol, float(nblk)), axis=1, keepdims=True)
        hit = bcol == first
        selm = jnp.where(hit, 1.0, selm)
        gate = jnp.where(hit, 2.0 * NEG, gate)
    tiles = []
    for q in range(nq):
        bias = jnp.concatenate(
            [jnp.broadcast_to(jnp.where(selm[:, q * bpc + k:q * bpc + k + 1] > 0.0, 0.0, NEG),
                              (B_HEADS, MOBA_BLOCK)) for k in range(bpc)], axis=1)
        tiles.append(raw[q] + bias)
    o_ref[...] = _sample_softmax(tiles, sn, vts, vbn_ref[...])


def _sample_call(kernel, name, page_table, small_inputs, caches, buf_rows, extra_scratch, **kw):
    db, n_pages = page_table.shape
    page = caches[0].shape[2]
    past = n_pages * page
    assert past % (N_KEY_CHUNKS * MOBA_BLOCK) == 0 and page % LANE == 0
    small_specs = [pl.BlockSpec((None,) + a.shape[1:], lambda b, pt: (b, 0, 0)) for a in small_inputs]
    any_spec = pl.BlockSpec(memory_space=pl.ANY)
    grid_spec = pltpu.PrefetchScalarGridSpec(
        num_scalar_prefetch=1,
        grid=(db,),
        in_specs=small_specs + [any_spec] * len(caches),
        out_specs=pl.BlockSpec((None, 8, LANE), lambda b, pt: (b, 0, 0)),
        scratch_shapes=[pltpu.VMEM((2, r, past), F32) for r in buf_rows]
        + [pltpu.SemaphoreType.DMA((len(caches), 2))] + extra_scratch(past),
    )
    return pl.pallas_call(
        functools.partial(kernel, n_pages=n_pages, past=past, **kw),
        grid_spec=grid_spec,
        out_shape=jax.ShapeDtypeStruct((db, 8, LANE), F32),
        compiler_params=_params(1),
        name=name,
    )(page_table, *small_inputs, *caches)


def _merge_kernel(x_ref, oa_ref, ob_ref, sga_ref, sgb_ref, wpa_ref, wpb_ref, wo_ref, y_ref):
    pa = _dot(oa_ref[...].astype(BF16), wpa_ref[...])
    pb = _dot(ob_ref[...].astype(BF16), wpb_ref[...])
    merged = sga_ref[...] * pa + sgb_ref[...] * pb
    y_ref[...] = x_ref[...] + _dot(merged.astype(BF16), wo_ref[...])


def _merge(x2d, oa, ob, sga, sgb, wpa, wpb, wo):
    n, d = x2d.shape
    tm = min(ROW_BLOCK, n)
    row = lambda c: pl.BlockSpec((tm, c), lambda i: (i, 0))
    const = lambda a: pl.BlockSpec(a.shape, lambda i: (0, 0))
    return pl.pallas_call(
        _merge_kernel,
        grid=(n // tm,),
        in_specs=[row(d), row(A_W), row(B_W), row(d), row(d), const(wpa), const(wpb), const(wo)],
        out_specs=row(d),
        out_shape=jax.ShapeDtypeStruct((n, d), F32),
        compiler_params=_params(1),
        name="merge",
    )(x2d, oa, ob, sga, sgb, wpa, wpb, wo)


FFN_COL_CHUNK = 1408


def _ffn_prompt_kernel(x_ref, n2_ref, wup_ref, cw_ref, cb_ref, wdn_ref, prev_ref, y_ref, tail_ref, ext_s, *, d_ff):
    tm = x_ref.shape[0]
    i = pl.program_id(1)

    @pl.when(i == 0)
    def _():
        ext_s[6:8, :] = prev_ref[...]

    x = x_ref[...]
    xn = _rms(x, n2_ref[...]).astype(BF16)
    fc = FFN_COL_CHUNK
    for c in range(2 * d_ff // fc):
        ext_s[8:8 + tm, c * fc:(c + 1) * fc] = _dot(xn, wup_ref[:, c * fc:(c + 1) * fc])

    def conv(lo):
        cols = slice(lo, lo + fc)
        out = cb_ref[:, cols] + ext_s[6:6 + tm, cols] * cw_ref[0:1, cols]
        out = out + ext_s[7:7 + tm, cols] * cw_ref[1:2, cols]
        return out + ext_s[8:8 + tm, cols] * cw_ref[2:3, cols]

    y = x
    for j in range(d_ff // fc):
        a = conv(j * fc)
        g = conv(d_ff + j * fc)
        act = (a / (1.0 + jnp.exp(-a)) * g).astype(BF16)
        y = y + _dot(act, wdn_ref[j * fc:(j + 1) * fc, :])
    y_ref[...] = y
    tail = ext_s[tm + 6:tm + 8, :]
    tail_ref[...] = tail
    ext_s[6:8, :] = tail


def _ffn_prompt(x2d, n2, wup, cw, cb, wdn, prev, batch, t):
    n, d = x2d.shape
    d_ff = wdn.shape[0]
    assert d_ff % FFN_COL_CHUNK == 0
    tm = ROW_BLOCK
    nt = t // tm
    row = pl.BlockSpec((tm, d), lambda b, i: (b * nt + i, 0))
    const = lambda a: pl.BlockSpec(a.shape, lambda b, i: (0, 0), pipeline_mode=pl.Buffered(1))
    per_b = pl.BlockSpec((None, 2, 2 * d_ff), lambda b, i: (b, 0, 0))
    return pl.pallas_call(
        functools.partial(_ffn_prompt_kernel, d_ff=d_ff),
        grid=(batch, nt),
        in_specs=[row, const(n2), const(wup), const(cw), const(cb), const(wdn), per_b],
        out_specs=[row, per_b],
        out_shape=[jax.ShapeDtypeStruct((n, d), F32), jax.ShapeDtypeStruct((batch, 2, 2 * d_ff), F32)],
        scratch_shapes=[pltpu.VMEM((tm + 8, 2 * d_ff), F32)],
        compiler_params=_params(2),
        name="ffn_prompt",
    )(x2d, n2, wup, cw, cb, wdn, prev)


def _ffn_sample_kernel(x_ref, n2_ref, wup_ref, cw_ref, cb_ref, wdn_ref, s0_ref, s1_ref, y_ref, up_ref, *, d_ff):
    x = x_ref[...]
    xn = _rms(x, n2_ref[...]).astype(BF16)
    fc = FFN_COL_CHUNK
    for c in range(2 * d_ff // fc):
        up_ref[:, c * fc:(c + 1) * fc] = _dot(xn, wup_ref[:, c * fc:(c + 1) * fc])

    def conv(lo):
        cols = slice(lo, lo + fc)
        out = cb_ref[:, cols] + s0_ref[:, cols] * cw_ref[0:1, cols]
        out = out + s1_ref[:, cols] * cw_ref[1:2, cols]
        return out + up_ref[:, cols] * cw_ref[2:3, cols]

    y = x
    for j in range(d_ff // fc):
        a = conv(j * fc)
        g = conv(d_ff + j * fc)
        act = (a / (1.0 + jnp.exp(-a)) * g).astype(BF16)
        y = y + _dot(act, wdn_ref[j * fc:(j + 1) * fc, :])
    y_ref[...] = y


def _ffn_sample(x2d, n2, wup, cw, cb, wdn, s0, s1):
    n, d = x2d.shape
    d_ff = wdn.shape[0]
    full = lambda a: pl.BlockSpec(a.shape, lambda i: (0, 0), pipeline_mode=pl.Buffered(1))
    args = (x2d, n2, wup, cw, cb, wdn, s0, s1)
    return pl.pallas_call(
        functools.partial(_ffn_sample_kernel, d_ff=d_ff),
        grid=(1,),
        in_specs=[full(a) for a in args],
        out_specs=[pl.BlockSpec((n, d), lambda i: (0, 0)), pl.BlockSpec((n, 2 * d_ff), lambda i: (0, 0))],
        out_shape=[jax.ShapeDtypeStruct((n, d), F32), jax.ShapeDtypeStruct((n, 2 * d_ff), F32)],
        compiler_params=_params(1),
        name="ffn_sample",
    )(*args)


def _rope_tables(pos):
    half = HEAD_DIM // 2
    inv = ROPE_THETA ** (-jnp.arange(half, dtype=F32) / half)
    ang = pos.astype(F32)[:, None] * inv[None, :]
    cos = jnp.cos(ang)
    sin = jnp.sin(ang)
    return jnp.tile(cos, (1, 4)), jnp.tile(jnp.concatenate([-sin, sin], axis=1), (1, 2))


def _layout_w_in(w_in, d_model):
    splits = (A_W, A_KV_W, A_KV_W, I_W, IDX_DIM, IDX_HEADS, B_W, B_KV_W, B_KV_W, d_model, d_model)
    offs = [0]
    for s in splits:
        offs.append(offs[-1] + s)
    p = [w_in[:, offs[k]:offs[k + 1]] for k in range(len(splits))]
    qa, ka, va, qi, ki, wi, qb, kb, vb, ga, gb = p
    pad = jnp.zeros((w_in.shape[0], LANE - IDX_DIM - IDX_HEADS), w_in.dtype)
    return jnp.concatenate([qa, ka, va, qi, qb, kb, vb, ga, gb, ki, wi, pad], axis=1).astype(BF16)


def _layout_gains(q_norm_a, k_norm_a, k_norm_idx, q_norm_b, k_norm_b, d_model):
    one = lambda n: jnp.ones((n,), F32)
    return jnp.concatenate([
        jnp.tile(q_norm_a, A_HEADS), jnp.tile(k_norm_a, A_KV_HEADS), one(A_KV_W), one(I_W),
        jnp.tile(q_norm_b, B_HEADS), jnp.tile(k_norm_b, B_KV_HEADS), one(B_KV_W),
        one(2 * d_model), k_norm_idx, one(LANE - IDX_DIM)])[None, :]


def _block_diag_q(q, heads_per_kv):
    z = jnp.zeros_like(q)
    low = jnp.concatenate([q, z], axis=-1)
    high = jnp.concatenate([z, q], axis=-1)
    is_low = (jnp.arange(q.shape[1]) // heads_per_kv == 0)[None, :, None]
    return jnp.where(is_low, low, high)


def _pick_kv(o, heads_per_kv):
    n = o.shape[0]
    return jnp.concatenate([o[:, :heads_per_kv, :HEAD_DIM].reshape(n, -1),
                            o[:, heads_per_kv:, HEAD_DIM:].reshape(n, -1)], axis=1)


def _sample_branches(qa_s, ka_s, va_s, qi_s, qb_s, kb_s, vb_s, kw_s, page_table, c_ik, c_ak, c_av, c_bk, c_bv):
    db = qa_s.shape[0]
    n_pool, page = c_ak.shape[0], c_ak.shape[1]
    past = page_table.shape[1] * page
    hpk_a = A_HEADS // A_KV_HEADS
    hpk_b = B_HEADS // B_KV_HEADS
    qi_pad = jnp.pad(qi_s.reshape(db, IDX_HEADS, IDX_DIM), ((0, 0), (0, 0), (0, LANE - IDX_DIM)))
    wi_s = kw_s[:, IDX_DIM:IDX_DIM + IDX_HEADS].reshape(db, 1, IDX_HEADS)
    qa_bd = _block_diag_q(qa_s.reshape(db, A_HEADS, HEAD_DIM), hpk_a)
    qb_bd = _block_diag_q(qb_s.reshape(db, B_HEADS, HEAD_DIM), hpk_b)
    kin = jnp.where(jnp.arange(LANE) < IDX_DIM, kw_s, 0.0).reshape(db, 1, LANE)
    pages_t = lambda c: jnp.moveaxis(c, 1, -1).reshape(n_pool, -1, page)
    n_sel = min(IDX_TOPK, (past + 1) // 4)
    oa_s = _sample_call(
        _dsa_sample_kernel, "dsa_sample", page_table,
        [qi_pad, wi_s, qa_bd, kin, ka_s.reshape(db, 1, LANE), va_s.reshape(db, 1, LANE)],
        [pages_t(c_ik), pages_t(c_ak), pages_t(c_av)], [LANE, LANE, LANE],
        lambda p: [pltpu.VMEM((N_KEY_CHUNKS, p // N_KEY_CHUNKS), F32)], n_sel=float(n_sel))
    n_blk = min(MOBA_TOPK, past // MOBA_BLOCK)
    ob_s = _sample_call(
        _moba_sample_kernel, "moba_sample", page_table,
        [qb_bd, kb_s.reshape(db, 1, LANE), vb_s.reshape(db, 1, LANE)],
        [pages_t(c_bk), pages_t(c_bv)], [LANE, LANE], lambda p: [], n_blk=float(n_blk))
    return _pick_kv(oa_s, hpk_a), _pick_kv(ob_s, hpk_b)


def kernel(x_prompt, x_sample, cache_a_k, cache_a_v, cache_idx_k, cache_b_k, cache_b_v, state_conv, page_table, norm1, w_in, q_norm_a, k_norm_a, k_norm_idx, q_norm_b, k_norm_b, w_proj_a, w_proj_b, w_out, norm2, w_up, conv_w, conv_b, w_down):
    batch, t, d = x_prompt.shape
    db, ds, _ = x_sample.shape
    depth = norm1.shape[0]
    assert depth == 1 and ds == 1
    n_pool, page = cache_a_k.shape[1], cache_a_k.shape[2]
    past = page_table.shape[1] * page
    l = 0
    w = _layout_w_in(w_in[l], d)
    gains = _layout_gains(q_norm_a[l], k_norm_a[l], k_norm_idx[l], q_norm_b[l], k_norm_b[l], d)
    wpa, wpb, wo = w_proj_a[l].astype(BF16), w_proj_b[l].astype(BF16), w_out[l].astype(BF16)
    wup, wdn = w_up[l].astype(BF16), w_down[l].astype(BF16)
    n1, n2 = norm1[l][None, :], norm2[l][None, :]
    cw, cb = conv_w[l], conv_b[l][None, :]
    d_ff = wdn.shape[0]

    xp = x_prompt.reshape(batch * t, d)
    cos_p, sin_p = _rope_tables(jnp.arange(t, dtype=jnp.int32))
    qa, ka, va, qi, qb, kb, vb, sga, sgb, kw = _proj(xp, n1, w, gains, cos_p, sin_p, t)
    oa = _dsa_prompt(qi, qa, kw, ka, va, batch, t)
    ob = _moba_prompt(qb, kb, vb, batch, t)
    x1 = _merge(xp, oa, ob, sga, sgb, wpa, wpb, wo)
    yp, p_conv = _ffn_prompt(x1, n2, wup, cw, cb, wdn, jnp.zeros((batch, 2, 2 * d_ff), F32), batch, t)

    xs = x_sample.reshape(db, d)
    cos_s, sin_s = _rope_tables(jnp.full((db,), past, jnp.int32))
    qa_s, ka_s, va_s, qi_s, qb_s, kb_s, vb_s, sga_s, sgb_s, kw_s = _proj(xs, n1, w, gains, cos_s, sin_s, db)
    oa_s, ob_s = _sample_branches(qa_s, ka_s, va_s, qi_s, qb_s, kb_s, vb_s, kw_s, page_table,
                                  cache_idx_k[l], cache_a_k[l], cache_a_v[l], cache_b_k[l], cache_b_v[l])
    x1_s = _merge(xs, oa_s, ob_s, sga_s, sgb_s, wpa, wpb, wo)
    ys, up_s = _ffn_sample(x1_s, n2, wup, cw, cb, wdn, state_conv[l, :, 0], state_conv[l, :, 1])

    kv5 = lambda a, n, s, h: a.reshape(1, n, s, h, HEAD_DIM)
    return (
        yp.reshape(batch, t, d), ys.reshape(db, 1, d),
        kv5(ka, batch, t, A_KV_HEADS), kv5(va, batch, t, A_KV_HEADS),
        kw[:, :IDX_DIM].reshape(1, batch, t, IDX_DIM),
        kv5(kb, batch, t, B_KV_HEADS), kv5(vb, batch, t, B_KV_HEADS),
        p_conv[None],
        kv5(ka_s, db, 1, A_KV_HEADS), kv5(va_s, db, 1, A_KV_HEADS),
        kw_s[:, :IDX_DIM].reshape(1, db, 1, IDX_DIM),
        kv5(kb_s, db, 1, B_KV_HEADS), kv5(vb_s, db, 1, B_KV_HEADS),
        jnp.stack([state_conv[l, :, 1], up_s], axis=1)[None],
    )
```

```python
import functools

import jax
import jax.numpy as jnp
from jax import lax
from jax.experimental import pallas as pl
from jax.experimental.pallas import tpu as pltpu

HEAD_DIM = 64
A_HEADS = 8
A_KV_HEADS = 2
IDX_HEADS = 8
IDX_DIM = 64
IDX_TOPK = 256
B_HEADS = 8
B_KV_HEADS = 2
MOBA_BLOCK = 256
MOBA_TOPK = 3
ROPE_THETA = 10000.0
EPS = 1e-6
NEG = -1e30
BIG = 3e38

LANE = 128
Q_BLOCK = 256
ROW_BLOCK = 256
VMEM_LIMIT = 56 * 1024 * 1024
BISECT_STEPS = 16
SAMPLE_SPLIT = 16
SAMPLE_SPLIT_ROUNDS = 4

F32 = jnp.float32
BF16 = jnp.bfloat16

A_W = A_HEADS * HEAD_DIM
A_KV_W = A_KV_HEADS * HEAD_DIM
I_W = IDX_HEADS * IDX_DIM
B_W = B_HEADS * HEAD_DIM
B_KV_W = B_KV_HEADS * HEAD_DIM
assert A_W == 512 and I_W == 512 and B_W == 512 and A_KV_W == LANE and B_KV_W == LANE


def _params(n_grid):
    return pltpu.CompilerParams(dimension_semantics=("arbitrary",) * n_grid,
                                vmem_limit_bytes=VMEM_LIMIT)


def _dot(a, b):
    return jnp.dot(a, b, preferred_element_type=F32)


def _dot_nt(a, b):
    return lax.dot_general(a, b, (((1,), (1,)), ((), ())), preferred_element_type=F32)


def _iota(shape, dim):
    return lax.broadcasted_iota(jnp.int32, shape, dim)


def _rms(x, g):
    return x * lax.rsqrt(jnp.mean(x * x, axis=-1, keepdims=True) + EPS) * g


def _proj_kernel(x_ref, n1_ref, w_ref, g_ref, cos_ref, sin_ref,
                 qa_ref, ka_ref, va_ref, qi_ref, qb_ref, kb_ref, vb_ref, sga_ref, sgb_ref, kw_ref,
                 kat_ref, vat_ref, kbt_ref, vbt_ref, kit_ref, *, d_model):
    h = _rms(x_ref[...], n1_ref[...]).astype(BF16)
    gr = lax.shift_right_logical(_iota((2 * LANE, 2 * LANE), 0), 6)
    gc = lax.shift_right_logical(_iota((2 * LANE, 2 * LANE), 1), 6)
    gsum = jnp.where(gr == gc, 1.0, 0.0).astype(BF16)
    cos1 = cos_ref[...]
    sin1 = sin_ref[...]

    def seg(off, width):
        return _dot(h, w_ref[:, off:off + width])

    def head_norm(x, off):
        width = x.shape[1]
        ss = _dot((x * x).astype(BF16), gsum[:width, :width])
        return x * lax.rsqrt(ss * (1.0 / HEAD_DIM) + EPS) * g_ref[:, off:off + width]

    def rope(x):
        width = x.shape[1]
        rep = width // LANE
        cs = jnp.concatenate([cos1] * rep, axis=1) if rep > 1 else cos1
        sn = jnp.concatenate([sin1] * rep, axis=1) if rep > 1 else sin1
        hi = (_iota(x.shape, 1) & (HEAD_DIM // 2)) != 0
        swapped = jnp.where(hi, pltpu.roll(x, HEAD_DIM // 2, axis=1),
                            pltpu.roll(x, width - HEAD_DIM // 2, axis=1))
        return x * cs + swapped * sn

    o = 0
    for half in range(2):
        x = seg(o + half * 256, 256)
        qa_ref[:, half * 256:(half + 1) * 256] = rope(head_norm(x, o + half * 256))
    o = A_W
    x = seg(o, 256)
    xn = rope(head_norm(x, o))
    ka_ref[...] = xn[:, :LANE]
    va_ref[...] = x[:, LANE:]
    kat_ref[...] = xn[:, :LANE].T
    vat_ref[...] = x[:, LANE:].T
    o = A_W + 2 * LANE
    for half in range(2):
        qi_ref[:, half * 256:(half + 1) * 256] = rope(seg(o + half * 256, 256))
    o = A_W + 2 * LANE + I_W
    for half in range(2):
        x = seg(o + half * 256, 256)
        qb_ref[:, half * 256:(half + 1) * 256] = rope(head_norm(x, o + half * 256))
    o = A_W + 2 * LANE + I_W + B_W
    x = seg(o, 256)
    xn = rope(head_norm(x, o))
    kb_ref[...] = xn[:, :LANE]
    vb_ref[...] = x[:, LANE:]
    kbt_ref[...] = xn[:, :LANE].T
    vbt_ref[...] = x[:, LANE:].T
    o = A_W + 2 * LANE + I_W + B_W + 2 * LANE
    for part in range(d_model // 256):
        x = seg(o + part * 256, 256)
        sga_ref[:, part * 256:(part + 1) * 256] = 1.0 / (1.0 + jnp.exp(-x))
    o += d_model
    for part in range(d_model // 256):
        x = seg(o + part * 256, 256)
        sgb_ref[:, part * 256:(part + 1) * 256] = 1.0 / (1.0 + jnp.exp(-x))
    o += d_model
    x = seg(o, LANE)
    xn = rope(head_norm(x, o))
    kw = jnp.where(_iota(x.shape, 1) < IDX_DIM, xn, x)
    kw_ref[...] = kw
    kit_ref[...] = kw.T[:IDX_DIM]


def _proj(x2d, n1, w, gains, cos, sin, rows_per_seq):
    n, d = x2d.shape
    tm = min(ROW_BLOCK, rows_per_seq)
    nt = rows_per_seq // tm
    nseq = n // rows_per_seq
    nw = w.shape[1]
    row = lambda c: pl.BlockSpec((tm, c), lambda i: (i, 0))
    const = lambda shape: pl.BlockSpec(shape, lambda i: (0, 0))
    tab = pl.BlockSpec((tm, LANE), lambda i: (i % nt, 0))
    chan = lambda c: pl.BlockSpec((None, c, tm), lambda i: (i // nt, 0, i % nt))
    rows = [A_W, LANE, LANE, I_W, B_W, LANE, LANE, d, d, LANE]
    chans = [LANE, LANE, LANE, LANE, IDX_DIM]
    return pl.pallas_call(
        functools.partial(_proj_kernel, d_model=d),
        grid=(n // tm,),
        in_specs=[row(d), const((1, d)), const((d, nw)), const((1, nw)), tab, tab],
        out_specs=[row(c) for c in rows] + [chan(c) for c in chans],
        out_shape=[jax.ShapeDtypeStruct((n, c), F32) for c in rows]
        + [jax.ShapeDtypeStruct((nseq, c, rows_per_seq), F32) for c in chans],
        compiler_params=_params(1),
        name="proj",
    )(x2d, n1, w, gains, cos, sin)


def _head_operand(q_ref, h, want_low, scale):
    ch = q_ref[:, LANE * (h // 2):LANE * (h // 2) + LANE]
    if (h % 2 == 0) != want_low:
        ch = pltpu.roll(ch, HEAD_DIM, axis=1)
    low = _iota(ch.shape, 1) < HEAD_DIM
    keep = low if want_low else jnp.logical_not(low)
    return (jnp.where(keep, ch, 0.0) * scale).astype(BF16)


def _fill_kv(k_ref, vt_ref, kb_s, vt_s, nchunk, tc):
    kb_s[...] = k_ref[...].astype(BF16)
    extra = jnp.where(_iota((V_ROWS - HEAD_DIM, tc), 0) == 0, 1.0, 0.0)
    for c in range(nchunk):
        for kv in range(2):
            vt = vt_ref[kv * HEAD_DIM:(kv + 1) * HEAD_DIM, c * tc:(c + 1) * tc]
            vt_s[kv, c] = jnp.concatenate([vt, extra], axis=0).astype(BF16)


def _attend_chunk(kc, c, q_s, vt_s, st, n_heads, heads_per_kv, tile_bias_of=None, row_bias_of=None):
    s_s, p_s, mc_s, a_s, m_s, acc_s = st
    for h in range(n_heads):
        s = _dot_nt(kc, q_s[h])
        if tile_bias_of is not None:
            s = s + tile_bias_of(h)
        s_s[h] = s
        mc_s[h] = jnp.max(s, axis=0, keepdims=True)
    for h in range(n_heads):
        m_old = m_s[h]
        if row_bias_of is None:
            m_new = jnp.maximum(m_old, mc_s[h])
            shift = m_new
        else:
            rb = row_bias_of(h)
            m_new = jnp.maximum(m_old, mc_s[h] + rb)
            shift = m_new - rb
        p_s[h] = jnp.exp2(s_s[h] - shift).astype(BF16)
        a_s[h] = jnp.exp2(m_old - m_new)
        m_s[h] = m_new
    for h in range(n_heads):
        acc_s[h] = a_s[h] * acc_s[h] + _dot(vt_s[h // heads_per_kv, c], p_s[h])


V_ROWS = HEAD_DIM + 16
LOG2E = 1.4426950408889634


def _attention_scratch(n_heads, tq):
    row = pltpu.VMEM((n_heads, 1, tq), F32)
    return [pltpu.VMEM((n_heads, tq, tq), F32), pltpu.VMEM((n_heads, tq, tq), BF16), row, row, row,
            pltpu.VMEM((n_heads, V_ROWS, tq), F32)]


def _init_softmax(st):
    m_s, acc_s = st[-2], st[-1]
    m_s[...] = jnp.full(m_s.shape, NEG, F32)
    acc_s[...] = jnp.zeros(acc_s.shape, F32)


def _write_heads(o_ref, st, n_heads):
    acc_s = st[-1]
    for j in range(n_heads // 2):
        pair = []
        for h in (2 * j, 2 * j + 1):
            acc = acc_s[h]
            pair.append(acc[:HEAD_DIM] / acc[HEAD_DIM:HEAD_DIM + 1])
        o_ref[:, LANE * j:LANE * (j + 1)] = jnp.concatenate(pair, axis=0).T


def _select_topk(sweep, count_gt, n_sel, small, key_index, tq, t_total):
    row = lambda v: jnp.full((1, tq), v, F32)
    rowmax = sweep(lambda s, c, a: jnp.maximum(a, jnp.max(s, axis=0, keepdims=True)), row(NEG))
    rowmin = sweep(lambda s, c, a: jnp.minimum(
        a, jnp.min(jnp.where(s > 0.5 * NEG, s, BIG), axis=0, keepdims=True)), row(BIG))

    def bisect(_, lh):
        lo, hi = lh
        mid = 0.5 * (lo + hi)
        ge = count_gt(mid) >= n_sel
        return jnp.where(ge, mid, lo), jnp.where(ge, hi, mid)

    lo, _ = lax.fori_loop(0, BISECT_STEPS, bisect, (rowmin, rowmax))
    u0 = sweep(lambda s, c, a: jnp.minimum(
        a, jnp.min(jnp.where(s >= lo, s, BIG), axis=0, keepdims=True)), row(BIG))

    def walk_cond(st):
        return st[2] == 0

    def walk(st):
        u = st[0]

        def f(s, c, carry):
            cnt, nxt = carry
            gt = s > u
            cnt = cnt + jnp.sum(jnp.where(gt, 1.0, 0.0), axis=0, keepdims=True)
            nxt = jnp.minimum(nxt, jnp.min(jnp.where(gt, s, BIG), axis=0, keepdims=True))
            return cnt, nxt

        cnt, nxt = sweep(f, (row(0.0), row(BIG)))
        done = jnp.logical_or(cnt < n_sel, small)
        all_done = jnp.min(jnp.where(done, 1.0, 0.0)).astype(jnp.int32)
        return jnp.where(done, u, nxt), cnt, all_done

    u, cgt, _ = lax.while_loop(walk_cond, walk, (u0, row(0.0), jnp.int32(0)))
    vstar = jnp.where(small, NEG, u)
    need = jnp.where(small, 0.0, n_sel - cgt)
    n_eq = sweep(lambda s, c, a: a + jnp.sum(jnp.where(s == vstar, 1.0, 0.0), axis=0, keepdims=True), row(0.0))
    excess = jnp.max(jnp.where(jnp.logical_and(n_eq > need, jnp.logical_not(small)), 1.0, 0.0))

    n_jsteps = jnp.where(excess > 0.0, t_total.bit_length() + 1, 0).astype(jnp.int32)

    def jstep(_, jj):
        jlo, jhi = jj
        mid = lax.shift_right_logical(jlo + jhi, 1)
        e = sweep(lambda s, c, a: a + jnp.sum(
            jnp.where(s == vstar, jnp.where(key_index(c) < mid, 1.0, 0.0), 0.0), axis=0, keepdims=True), row(0.0))
        ge = e >= need
        return jnp.where(ge, jlo, mid), jnp.where(ge, mid, jhi)

    zero_i = jnp.zeros((1, tq), jnp.int32)
    _, jcut = lax.fori_loop(0, n_jsteps, jstep, (zero_i, zero_i + t_total))
    return vstar, jnp.where(small, 0, jcut)


def _dsa_prompt_kernel(qi_ref, qa_ref, kwq_ref, kwk_ref, ka_ref, vat_ref, o_ref,
                       kib_s, kab_s, vt_s, qim_s, qam_s, sc_s, *st, n_sel, t_total):
    tq = qi_ref.shape[0]
    tc = tq
    nchunk = t_total // tc
    i = pl.program_id(1)

    @pl.when(i == 0)
    def _():
        kib_s[...] = kwk_ref[...].astype(BF16)
        _fill_kv(ka_ref, vat_ref, kab_s, vt_s, nchunk, tc)

    heads_per_kv = A_HEADS // A_KV_HEADS
    for h in range(IDX_HEADS):
        qim_s[h] = _head_operand(qi_ref, h, True, 1.0)
    for h in range(A_HEADS):
        qam_s[h] = _head_operand(qa_ref, h, (h // heads_per_kv) == 0, HEAD_DIM ** -0.5 * LOG2E)
    w8 = kwq_ref[...].T[IDX_DIM:IDX_DIM + IDX_HEADS, :] * (IDX_DIM ** -0.5 * IDX_HEADS ** -0.5)

    krow = _iota((tc, tq), 0)
    qcol = _iota((tc, tq), 1)

    def chunk(c):
        return pl.ds(pl.multiple_of(c * tc, tc), tc)

    def score_chunk(c, diag):
        kc = kib_s[chunk(c), :]
        a = jnp.zeros((tc, tq), F32)
        for h in range(IDX_HEADS):
            a = a + jnp.maximum(_dot_nt(kc, qim_s[h]), 0.0) * w8[h:h + 1, :]
        if diag:
            a = jnp.where(krow <= qcol, a, NEG)
        sc_s[chunk(c), :] = a

    def score_body(c, carry):
        score_chunk(c, False)
        return carry

    lax.fori_loop(0, i, score_body, 0)
    score_chunk(i, True)

    def sweep(fn, init):
        return lax.fori_loop(0, i + 1, lambda c, carry: fn(sc_s[chunk(c), :], c, carry), init)

    def count_gt(x):
        return sweep(lambda s, c, a: a + jnp.sum(jnp.where(s > x, 1.0, 0.0), axis=0, keepdims=True),
                     jnp.zeros((1, tq), F32))

    n_adm = i * tq + _iota((1, tq), 1) + 1
    small = n_adm <= n_sel
    vstar, jcut = _select_topk(sweep, count_gt, float(n_sel), small,
                               lambda c: c * tc + krow, tq, t_total)

    def bias_body(c, carry):
        s = sc_s[chunk(c), :]
        tie = jnp.where((c * tc + krow) < jcut, 0.0, NEG)
        sc_s[chunk(c), :] = jnp.where(s > vstar, 0.0, jnp.where(s == vstar, tie, NEG))
        return carry

    lax.fori_loop(0, i + 1, bias_body, 0)

    _init_softmax(st)

    def att_body(c, carry):
        _attend_chunk(kab_s[chunk(c), :], c, qam_s, vt_s, st, A_HEADS, heads_per_kv,
                      tile_bias_of=lambda h: sc_s[chunk(c), :])
        return carry

    lax.fori_loop(0, i + 1, att_body, 0)
    _write_heads(o_ref, st, A_HEADS)


def _dsa_prompt(qi, qa, kw, ka, vat, batch, t):
    tq = Q_BLOCK
    nq = t // tq
    n_sel = min(IDX_TOPK, t // 4)
    qblk = lambda c: pl.BlockSpec((tq, c), lambda b, i: (b * nq + i, 0))
    full = pl.BlockSpec((t, LANE), lambda b, i: (b, 0))
    full_t = pl.BlockSpec((None, LANE, t), lambda b, i: (b, 0, 0))
    return pl.pallas_call(
        functools.partial(_dsa_prompt_kernel, n_sel=n_sel, t_total=t),
        grid=(batch, nq),
        in_specs=[qblk(I_W), qblk(A_W), qblk(LANE), full, full, full_t],
        out_specs=qblk(A_W),
        out_shape=jax.ShapeDtypeStruct((batch * t, A_W), F32),
        scratch_shapes=[
            pltpu.VMEM((t, LANE), BF16), pltpu.VMEM((t, LANE), BF16),
            pltpu.VMEM((2, nq, V_ROWS, tq), BF16),
            pltpu.VMEM((IDX_HEADS, tq, LANE), BF16), pltpu.VMEM((A_HEADS, tq, LANE), BF16),
            pltpu.VMEM((t, tq), F32),
        ] + _attention_scratch(A_HEADS, tq),
        compiler_params=_params(2),
        name="dsa_prompt",
    )(qi, qa, kw, kw, ka, vat)


def _moba_prompt_kernel(qb_ref, kb_ref, vbt_ref, o_ref,
                        kbb_s, vt_s, kbar_s, qbm_s, selb_s, *st, n_blk, t_total):
    tq = qb_ref.shape[0]
    tc = tq
    nb = t_total // tc
    nbp = kbar_s.shape[0]
    i = pl.program_id(1)
    heads_per_kv = B_HEADS // B_KV_HEADS

    @pl.when(i == 0)
    def _():
        _fill_kv(kb_ref, vbt_ref, kbb_s, vt_s, nb, tc)
        kbar_s[...] = jnp.zeros(kbar_s.shape, F32)
        for n in range(nb):
            kbar_s[n:n + 1, :] = jnp.mean(kb_ref[n * tc:(n + 1) * tc, :], axis=0, keepdims=True)

    kbar = kbar_s[...].astype(BF16)
    blk = _iota((nbp, tq), 0)
    past = blk < i
    for h in range(B_HEADS):
        qm = _head_operand(qb_ref, h, (h // heads_per_kv) == 0, HEAD_DIM ** -0.5 * LOG2E)
        qbm_s[h] = qm
        gate = jnp.where(past, _dot_nt(kbar, qm), NEG)
        rank = jnp.zeros((nbp, tq), F32)
        for m in range(nb):
            gm = gate[m:m + 1, :]
            first = jnp.where(blk > m, 1.0, 0.0)
            rank = rank + jnp.where(gm > gate, 1.0, jnp.where(gm == gate, first, 0.0))
        sel = jnp.logical_and(past, rank < n_blk)
        selb_s[h] = jnp.where(sel, 0.0, NEG)

    _init_softmax(st)
    krow = _iota((tc, tq), 0)
    qcol = _iota((tc, tq), 1)

    def chunk(c):
        return pl.ds(pl.multiple_of(c * tc, tc), tc)

    _attend_chunk(kbb_s[chunk(i), :], i, qbm_s, vt_s, st, B_HEADS, heads_per_kv,
                  tile_bias_of=lambda h: jnp.where(krow <= qcol, 0.0, NEG))

    def att_body(c, carry):
        _attend_chunk(kbb_s[chunk(c), :], c, qbm_s, vt_s, st, B_HEADS, heads_per_kv,
                      row_bias_of=lambda h: selb_s[h, pl.ds(c, 1), :])
        return carry

    lax.fori_loop(0, i, att_body, 0)
    _write_heads(o_ref, st, B_HEADS)


def _moba_prompt(qb, kb, vbt, batch, t):
    tq = Q_BLOCK
    assert tq == MOBA_BLOCK and t % tq == 0
    nq = t // tq
    n_blk = min(MOBA_TOPK, (t - 1) // MOBA_BLOCK)
    qblk = pl.BlockSpec((tq, B_W), lambda b, i: (b * nq + i, 0))
    full = pl.BlockSpec((t, LANE), lambda b, i: (b, 0))
    full_t = pl.BlockSpec((None, LANE, t), lambda b, i: (b, 0, 0))
    return pl.pallas_call(
        functools.partial(_moba_prompt_kernel, n_blk=float(n_blk), t_total=t),
        grid=(batch, nq),
        in_specs=[qblk, full, full_t],
        out_specs=qblk,
        out_shape=jax.ShapeDtypeStruct((batch * t, B_W), F32),
        scratch_shapes=[
            pltpu.VMEM((t, LANE), BF16),
            pltpu.VMEM((2, nq, V_ROWS, tq), BF16),
            pltpu.VMEM((max(nq, 8), LANE), F32),
            pltpu.VMEM((B_HEADS, tq, LANE), BF16),
            pltpu.VMEM((B_HEADS, max(nq, 8), tq), F32),
        ] + _attention_scratch(B_HEADS, tq),
        compiler_params=_params(2),
        name="moba_prompt",
    )(qb, kb, vbt)


def _page_copy(cache, buf, sem, k, slot, page, p):
    rows, width = cache.shape[1], cache.shape[2]
    return pltpu.make_async_copy(cache.at[page], buf.at[slot, pl.ds(0, rows), pl.ds(p * width, width)],
                                 sem.at[k, slot])


def _gather(pt_ref, b, caches, bufs, sem, slot, n_pages, start):
    for k, (cache, buf) in enumerate(zip(caches, bufs)):
        for p in range(n_pages):
            cp = _page_copy(cache, buf, sem, k, slot, pt_ref[b, p] if start else 0, p)
            if start:
                cp.start()
            else:
                cp.wait()


def _prefetch_pages(pt_ref, caches, bufs, sem, n_pages, before_first=None):
    b = pl.program_id(0)
    nb = pl.num_programs(0)
    slot = lax.rem(b, 2)

    @pl.when(b == 0)
    def _():
        if before_first is not None:
            before_first()
        _gather(pt_ref, b, caches, bufs, sem, slot, n_pages, True)

    @pl.when(b + 1 < nb)
    def _():
        _gather(pt_ref, b + 1, caches, bufs, sem, 1 - slot, n_pages, True)

    _gather(pt_ref, b, caches, bufs, sem, slot, n_pages, False)
    return slot


def _sample_softmax(tiles, s_new, vt_tiles, v_new):
    m = s_new
    for s in tiles:
        m = jnp.maximum(m, jnp.max(s, axis=1, keepdims=True))
    p_new = jnp.exp(s_new - m)
    l = p_new
    o = p_new * v_new
    for s, vt in zip(tiles, vt_tiles):
        p = jnp.exp(s - m)
        l = l + jnp.sum(p, axis=1, keepdims=True)
        o = o + _dot_nt(p.astype(BF16), vt)
    return o / l


N_KEY_CHUNKS = 8


def _dsa_sample_kernel(pt_ref, qi_ref, wi_ref, qbd_ref, kin_ref, kan_ref, van_ref,
                       cik, cak, cav, o_ref, bik, bak, bav, sem, sc_s, *, n_pages, n_sel, past):
    def zero_pad_rows():
        bik[...] = jnp.zeros(bik.shape, F32)

    slot = _prefetch_pages(pt_ref, (cik, cak, cav), (bik, bak, bav), sem, n_pages, zero_pad_rows)
    nq = N_KEY_CHUNKS
    ch = past // nq
    qi = qi_ref[...]
    qib = qi.astype(BF16)
    wrow = wi_ref[...] * (IDX_DIM ** -0.5 * IDX_HEADS ** -0.5)
    eye = _iota((IDX_HEADS, IDX_HEADS), 0) == _iota((IDX_HEADS, IDX_HEADS), 1)
    wcol = jnp.sum(jnp.where(eye, jnp.broadcast_to(wrow, (IDX_HEADS, IDX_HEADS)), 0.0), axis=1, keepdims=True)
    lg_new = jnp.sum(qi * kin_ref[...], axis=1, keepdims=True)
    s_new = jnp.sum(jnp.maximum(lg_new, 0.0) * wcol, axis=0, keepdims=True)

    for q in range(nq):
        kq = bik[slot, :, q * ch:(q + 1) * ch].astype(BF16)
        lg = jnp.maximum(_dot(qib, kq), 0.0)
        sc_s[q:q + 1, :] = jnp.sum(lg * wcol, axis=0, keepdims=True)

    sc = sc_s[...]
    kidx = _iota((nq, ch), 0) * ch + _iota((nq, ch), 1)

    def total(x):
        return jnp.sum(jnp.sum(x, axis=1, keepdims=True), axis=0, keepdims=True)

    def tmin(x):
        return jnp.min(jnp.min(x, axis=1, keepdims=True), axis=0, keepdims=True)

    def count_gt(x):
        return total(jnp.where(sc > x, 1.0, 0.0)) + jnp.where(s_new > x, 1.0, 0.0)

    hi0 = jnp.maximum(-tmin(-sc), s_new)
    lo0 = jnp.minimum(tmin(sc), s_new)

    lo, hi = lo0, hi0
    for _ in range(SAMPLE_SPLIT_ROUNDS):
        step = (hi - lo) * (1.0 / SAMPLE_SPLIT)
        new_lo, new_hi = lo, hi
        for j in range(1, SAMPLE_SPLIT):
            tj = lo + step * float(j)
            ge = count_gt(tj) >= n_sel
            new_lo = jnp.where(ge, jnp.maximum(new_lo, tj), new_lo)
            new_hi = jnp.where(ge, new_hi, jnp.minimum(new_hi, tj))
        lo, hi = new_lo, new_hi
    u0 = jnp.minimum(tmin(jnp.where(sc >= lo, sc, BIG)), jnp.where(s_new >= lo, s_new, BIG))

    def walk_cond(st):
        return st[2] == 0

    def walk(st):
        u = st[0]
        cnt = count_gt(u)
        nxt = jnp.minimum(tmin(jnp.where(sc > u, sc, BIG)), jnp.where(s_new > u, s_new, BIG))
        done = cnt < n_sel
        return jnp.where(done, u, nxt), cnt, jnp.min(jnp.where(done, 1.0, 0.0)).astype(jnp.int32)

    vstar, cgt, _ = lax.while_loop(walk_cond, walk, (u0, jnp.zeros((1, 1), F32), jnp.int32(0)))
    need = n_sel - cgt
    n_eq = total(jnp.where(sc == vstar, 1.0, 0.0)) + jnp.where(s_new == vstar, 1.0, 0.0)
    n_jsteps = jnp.where(jnp.max(n_eq - need) > 0.0, (past + 1).bit_length() + 1, 0).astype(jnp.int32)

    def jstep(_, jj):
        jlo, jhi = jj
        mid = lax.shift_right_logical(jlo + jhi, 1)
        e = total(jnp.where(sc == vstar, jnp.where(kidx < mid, 1.0, 0.0), 0.0)) + jnp.where(
            s_new == vstar, jnp.where(past < mid, 1.0, 0.0), 0.0)
        ge = e >= need
        return jnp.where(ge, jlo, mid), jnp.where(ge, mid, jhi)

    zero_i = jnp.zeros((1, 1), jnp.int32)
    _, jcut = lax.fori_loop(0, n_jsteps, jstep, (zero_i, zero_i + (past + 1)))
    bias = jnp.where(sc > vstar, 0.0, jnp.where(sc == vstar, jnp.where(kidx < jcut, 0.0, NEG), NEG))
    bias_new = jnp.where(s_new > vstar, 0.0, jnp.where(s_new == vstar, jnp.where(past < jcut, 0.0, NEG), NEG))

    qs = (qbd_ref[...] * (HEAD_DIM ** -0.5))
    sn = jnp.sum(qs * kan_ref[...], axis=1, keepdims=True) + bias_new
    qsb = qs.astype(BF16)
    tiles, vts = [], []
    for q in range(nq):
        kq = bak[slot, :, q * ch:(q + 1) * ch].astype(BF16)
        tiles.append(_dot(qsb, kq) + bias[q:q + 1, :])
        vts.append(bav[slot, :, q * ch:(q + 1) * ch].astype(BF16))
    o_ref[...] = _sample_softmax(tiles, sn, vts, van_ref[...])


def _moba_sample_kernel(pt_ref, qbd_ref, kbn_ref, vbn_ref, cbk, cbv, o_ref, bbk, bbv, sem,
                        *, n_pages, n_blk, past):
    slot = _prefetch_pages(pt_ref, (cbk, cbv), (bbk, bbv), sem, n_pages)
    nq = N_KEY_CHUNKS
    ch = past // nq
    nblk = past // MOBA_BLOCK
    bpc = ch // MOBA_BLOCK
    qs = qbd_ref[...] * (HEAD_DIM ** -0.5)
    qsb = qs.astype(BF16)
    sn = jnp.sum(qs * kbn_ref[...], axis=1, keepdims=True)
    bcol = _iota((B_HEADS, nblk), 1).astype(F32)

    raw, vts = [], []
    gate = jnp.zeros((B_HEADS, nblk), F32)
    for q in range(nq):
        s = _dot(qsb, bbk[slot, :, q * ch:(q + 1) * ch].astype(BF16))
        raw.append(s)
        vts.append(bbv[slot, :, q * ch:(q + 1) * ch].astype(BF16))
        for k in range(bpc):
            g = jnp.sum(s[:, k * MOBA_BLOCK:(k + 1) * MOBA_BLOCK], axis=1, keepdims=True) * (1.0 / MOBA_BLOCK)
            gate = jnp.where(bcol == float(q * bpc + k), g, gate)
    selm = jnp.zeros((B_HEADS, nblk), F32)
    for _ in range(int(n_blk)):
        mx = jnp.max(gate, axis=1, keepdims=True)
        first = jnp.min(jnp.where(gate == mx, bcol, float(nblk)), axis=1, keepdims=True)
        hit = bcol == first
        selm = jnp.where(hit, 1.0, selm)
        gate = jnp.where(hit, 2.0 * NEG, gate)
    tiles = []
    for q in range(nq):
        bias = jnp.concatenate(
            [jnp.broadcast_to(jnp.where(selm[:, q * bpc + k:q * bpc + k + 1] > 0.0, 0.0, NEG),
                              (B_HEADS, MOBA_BLOCK)) for k in range(bpc)], axis=1)
        tiles.append(raw[q] + bias)
    o_ref[...] = _sample_softmax(tiles, sn, vts, vbn_ref[...])


def _sample_call(kernel, name, page_table, small_inputs, caches, buf_rows, extra_scratch, **kw):
    db, n_pages = page_table.shape
    page = caches[0].shape[2]
    past = n_pages * page
    assert past % (N_KEY_CHUNKS * MOBA_BLOCK) == 0 and page % LANE == 0
    small_specs = [pl.BlockSpec((None,) + a.shape[1:], lambda b, pt: (b, 0, 0)) for a in small_inputs]
    any_spec = pl.BlockSpec(memory_space=pl.ANY)
    grid_spec = pltpu.PrefetchScalarGridSpec(
        num_scalar_prefetch=1,
        grid=(db,),
        in_specs=small_specs + [any_spec] * len(caches),
        out_specs=pl.BlockSpec((None, 8, LANE), lambda b, pt: (b, 0, 0)),
        scratch_shapes=[pltpu.VMEM((2, r, past), F32) for r in buf_rows]
        + [pltpu.SemaphoreType.DMA((len(caches), 2))] + extra_scratch(past),
    )
    return pl.pallas_call(
        functools.partial(kernel, n_pages=n_pages, past=past, **kw),
        grid_spec=grid_spec,
        out_shape=jax.ShapeDtypeStruct((db, 8, LANE), F32),
        compiler_params=_params(1),
        name=name,
    )(page_table, *small_inputs, *caches)


def _merge_kernel(x_ref, oa_ref, ob_ref, sga_ref, sgb_ref, wpa_ref, wpb_ref, wo_ref, y_ref):
    pa = _dot(oa_ref[...].astype(BF16), wpa_ref[...])
    pb = _dot(ob_ref[...].astype(BF16), wpb_ref[...])
    merged = sga_ref[...] * pa + sgb_ref[...] * pb
    y_ref[...] = x_ref[...] + _dot(merged.astype(BF16), wo_ref[...])


def _merge(x2d, oa, ob, sga, sgb, wpa, wpb, wo):
    n, d = x2d.shape
    tm = min(ROW_BLOCK, n)
    row = lambda c: pl.BlockSpec((tm, c), lambda i: (i, 0))
    const = lambda a: pl.BlockSpec(a.shape, lambda i: (0, 0))
    return pl.pallas_call(
        _merge_kernel,
        grid=(n // tm,),
        in_specs=[row(d), row(A_W), row(B_W), row(d), row(d), const(wpa), const(wpb), const(wo)],
        out_specs=row(d),
        out_shape=jax.ShapeDtypeStruct((n, d), F32),
        compiler_params=_params(1),
        name="merge",
    )(x2d, oa, ob, sga, sgb, wpa, wpb, wo)


FFN_COL_CHUNK = 1408


def _ffn_prompt_kernel(x_ref, n2_ref, wup_ref, cw_ref, cb_ref, wdn_ref, prev_ref, y_ref, tail_ref, ext_s, *, d_ff):
    tm = x_ref.shape[0]
    i = pl.program_id(1)

    @pl.when(i == 0)
    def _():
        ext_s[6:8, :] = prev_ref[...]

    x = x_ref[...]
    xn = _rms(x, n2_ref[...]).astype(BF16)
    fc = FFN_COL_CHUNK
    for c in range(2 * d_ff // fc):
        ext_s[8:8 + tm, c * fc:(c + 1) * fc] = _dot(xn, wup_ref[:, c * fc:(c + 1) * fc])

    def conv(lo):
        cols = slice(lo, lo + fc)
        out = cb_ref[:, cols] + ext_s[6:6 + tm, cols] * cw_ref[0:1, cols]
        out = out + ext_s[7:7 + tm, cols] * cw_ref[1:2, cols]
        return out + ext_s[8:8 + tm, cols] * cw_ref[2:3, cols]

    y = x
    for j in range(d_ff // fc):
        a = conv(j * fc)
        g = conv(d_ff + j * fc)
        act = (a / (1.0 + jnp.exp(-a)) * g).astype(BF16)
        y = y + _dot(act, wdn_ref[j * fc:(j + 1) * fc, :])
    y_ref[...] = y
    tail = ext_s[tm + 6:tm + 8, :]
    tail_ref[...] = tail
    ext_s[6:8, :] = tail


def _ffn_prompt(x2d, n2, wup, cw, cb, wdn, prev, batch, t):
    n, d = x2d.shape
    d_ff = wdn.shape[0]
    assert d_ff % FFN_COL_CHUNK == 0
    tm = ROW_BLOCK
    nt = t // tm
    row = pl.BlockSpec((tm, d), lambda b, i: (b * nt + i, 0))
    const = lambda a: pl.BlockSpec(a.shape, lambda b, i: (0, 0), pipeline_mode=pl.Buffered(1))
    per_b = pl.BlockSpec((None, 2, 2 * d_ff), lambda b, i: (b, 0, 0))
    return pl.pallas_call(
        functools.partial(_ffn_prompt_kernel, d_ff=d_ff),
        grid=(batch, nt),
        in_specs=[row, const(n2), const(wup), const(cw), const(cb), const(wdn), per_b],
        out_specs=[row, per_b],
        out_shape=[jax.ShapeDtypeStruct((n, d), F32), jax.ShapeDtypeStruct((batch, 2, 2 * d_ff), F32)],
        scratch_shapes=[pltpu.VMEM((tm + 8, 2 * d_ff), F32)],
        compiler_params=_params(2),
        name="ffn_prompt",
    )(x2d, n2, wup, cw, cb, wdn, prev)


def _ffn_sample_kernel(x_ref, n2_ref, wup_ref, cw_ref, cb_ref, wdn_ref, s0_ref, s1_ref, y_ref, up_ref, *, d_ff):
    x = x_ref[...]
    xn = _rms(x, n2_ref[...]).astype(BF16)
    fc = FFN_COL_CHUNK
    for c in range(2 * d_ff // fc):
        up_ref[:, c * fc:(c + 1) * fc] = _dot(xn, wup_ref[:, c * fc:(c + 1) * fc])

    def conv(lo):
        cols = slice(lo, lo + fc)
        out = cb_ref[:, cols] + s0_ref[:, cols] * cw_ref[0:1, cols]
        out = out + s1_ref[:, cols] * cw_ref[1:2, cols]
        return out + up_ref[:, cols] * cw_ref[2:3, cols]

    y = x
    for j in range(d_ff // fc):
        a = conv(j * fc)
        g = conv(d_ff + j * fc)
        act = (a / (1.0 + jnp.exp(-a)) * g).astype(BF16)
        y = y + _dot(act, wdn_ref[j * fc:(j + 1) * fc, :])
    y_ref[...] = y


def _ffn_sample(x2d, n2, wup, cw, cb, wdn, s0, s1):
    n, d = x2d.shape
    d_ff = wdn.shape[0]
    full = lambda a: pl.BlockSpec(a.shape, lambda i: (0, 0), pipeline_mode=pl.Buffered(1))
    args = (x2d, n2, wup, cw, cb, wdn, s0, s1)
    return pl.pallas_call(
        functools.partial(_ffn_sample_kernel, d_ff=d_ff),
        grid=(1,),
        in_specs=[full(a) for a in args],
        out_specs=[pl.BlockSpec((n, d), lambda i: (0, 0)), pl.BlockSpec((n, 2 * d_ff), lambda i: (0, 0))],
        out_shape=[jax.ShapeDtypeStruct((n, d), F32), jax.ShapeDtypeStruct((n, 2 * d_ff), F32)],
        compiler_params=_params(1),
        name="ffn_sample",
    )(*args)


def _rope_tables(pos):
    half = HEAD_DIM // 2
    inv = ROPE_THETA ** (-jnp.arange(half, dtype=F32) / half)
    ang = pos.astype(F32)[:, None] * inv[None, :]
    cos = jnp.cos(ang)
    sin = jnp.sin(ang)
    return jnp.tile(cos, (1, 4)), jnp.tile(jnp.concatenate([-sin, sin], axis=1), (1, 2))


def _layout_w_in(w_in, d_model):
    splits = (A_W, A_KV_W, A_KV_W, I_W, IDX_DIM, IDX_HEADS, B_W, B_KV_W, B_KV_W, d_model, d_model)
    offs = [0]
    for s in splits:
        offs.append(offs[-1] + s)
    p = [w_in[:, offs[k]:offs[k + 1]] for k in range(len(splits))]
    qa, ka, va, qi, ki, wi, qb, kb, vb, ga, gb = p
    pad = jnp.zeros((w_in.shape[0], LANE - IDX_DIM - IDX_HEADS), w_in.dtype)
    return jnp.concatenate([qa, ka, va, qi, qb, kb, vb, ga, gb, ki, wi, pad], axis=1).astype(BF16)


def _layout_gains(q_norm_a, k_norm_a, k_norm_idx, q_norm_b, k_norm_b, d_model):
    one = lambda n: jnp.ones((n,), F32)
    return jnp.concatenate([
        jnp.tile(q_norm_a, A_HEADS), jnp.tile(k_norm_a, A_KV_HEADS), one(A_KV_W), one(I_W),
        jnp.tile(q_norm_b, B_HEADS), jnp.tile(k_norm_b, B_KV_HEADS), one(B_KV_W),
        one(2 * d_model), k_norm_idx, one(LANE - IDX_DIM)])[None, :]


def _block_diag_q(q, heads_per_kv):
    z = jnp.zeros_like(q)
    low = jnp.concatenate([q, z], axis=-1)
    high = jnp.concatenate([z, q], axis=-1)
    is_low = (jnp.arange(q.shape[1]) // heads_per_kv == 0)[None, :, None]
    return jnp.where(is_low, low, high)


def _pick_kv(o, heads_per_kv):
    n = o.shape[0]
    return jnp.concatenate([o[:, :heads_per_kv, :HEAD_DIM].reshape(n, -1),
                            o[:, heads_per_kv:, HEAD_DIM:].reshape(n, -1)], axis=1)


def _sample_branches(qa_s, ka_s, va_s, qi_s, qb_s, kb_s, vb_s, kw_s, page_table, c_ik, c_ak, c_av, c_bk, c_bv):
    db = qa_s.shape[0]
    n_pool, page = c_ak.shape[0], c_ak.shape[1]
    past = page_table.shape[1] * page
    hpk_a = A_HEADS // A_KV_HEADS
    hpk_b = B_HEADS // B_KV_HEADS
    qi_pad = jnp.pad(qi_s.reshape(db, IDX_HEADS, IDX_DIM), ((0, 0), (0, 0), (0, LANE - IDX_DIM)))
    wi_s = kw_s[:, IDX_DIM:IDX_DIM + IDX_HEADS].reshape(db, 1, IDX_HEADS)
    qa_bd = _block_diag_q(qa_s.reshape(db, A_HEADS, HEAD_DIM), hpk_a)
    qb_bd = _block_diag_q(qb_s.reshape(db, B_HEADS, HEAD_DIM), hpk_b)
    kin = jnp.where(jnp.arange(LANE) < IDX_DIM, kw_s, 0.0).reshape(db, 1, LANE)
    pages_t = lambda c: jnp.moveaxis(c, 1, -1).reshape(n_pool, -1, page)
    n_sel = min(IDX_TOPK, (past + 1) // 4)
    oa_s = _sample_call(
        _dsa_sample_kernel, "dsa_sample", page_table,
        [qi_pad, wi_s, qa_bd, kin, ka_s.reshape(db, 1, LANE), va_s.reshape(db, 1, LANE)],
        [pages_t(c_ik), pages_t(c_ak), pages_t(c_av)], [LANE, LANE, LANE],
        lambda p: [pltpu.VMEM((N_KEY_CHUNKS, p // N_KEY_CHUNKS), F32)], n_sel=float(n_sel))
    n_blk = min(MOBA_TOPK, past // MOBA_BLOCK)
    ob_s = _sample_call(
        _moba_sample_kernel, "moba_sample", page_table,
        [qb_bd, kb_s.reshape(db, 1, LANE), vb_s.reshape(db, 1, LANE)],
        [pages_t(c_bk), pages_t(c_bv)], [LANE, LANE], lambda p: [], n_blk=float(n_blk))
    return _pick_kv(oa_s, hpk_a), _pick_kv(ob_s, hpk_b)


def kernel(x_prompt, x_sample, cache_a_k, cache_a_v, cache_idx_k, cache_b_k, cache_b_v, state_conv, page_table, norm1, w_in, q_norm_a, k_norm_a, k_norm_idx, q_norm_b, k_norm_b, w_proj_a, w_proj_b, w_out, norm2, w_up, conv_w, conv_b, w_down):
    batch, t, d = x_prompt.shape
    db, ds, _ = x_sample.shape
    depth = norm1.shape[0]
    assert depth == 1 and ds == 1
    n_pool, page = cache_a_k.shape[1], cache_a_k.shape[2]
    past = page_table.shape[1] * page
    l = 0
    w = _layout_w_in(w_in[l], d)
    gains = _layout_gains(q_norm_a[l], k_norm_a[l], k_norm_idx[l], q_norm_b[l], k_norm_b[l], d)
    wpa, wpb, wo = w_proj_a[l].astype(BF16), w_proj_b[l].astype(BF16), w_out[l].astype(BF16)
    wup, wdn = w_up[l].astype(BF16), w_down[l].astype(BF16)
    n1, n2 = norm1[l][None, :], norm2[l][None, :]
    cw, cb = conv_w[l], conv_b[l][None, :]
    d_ff = wdn.shape[0]

    xp = x_prompt.reshape(batch * t, d)
    cos_p, sin_p = _rope_tables(jnp.arange(t, dtype=jnp.int32))
    qa, ka, _, qi, qb, kb, _, sga, sgb, kw, kat, vat, kbt, vbt, kit = _proj(xp, n1, w, gains, cos_p, sin_p, t)
    oa = _dsa_prompt(qi, qa, kw, ka, vat, batch, t)
    ob = _moba_prompt(qb, kb, vbt, batch, t)
    x1 = _merge(xp, oa, ob, sga, sgb, wpa, wpb, wo)
    yp, p_conv = _ffn_prompt(x1, n2, wup, cw, cb, wdn, jnp.zeros((batch, 2, 2 * d_ff), F32), batch, t)

    xs = x_sample.reshape(db, d)
    cos_s, sin_s = _rope_tables(jnp.full((db,), past, jnp.int32))
    (qa_s, ka_s, va_s, qi_s, qb_s, kb_s, vb_s, sga_s, sgb_s, kw_s,
     kat_s, vat_s, kbt_s, vbt_s, kit_s) = _proj(xs, n1, w, gains, cos_s, sin_s, db)
    oa_s, ob_s = _sample_branches(qa_s, ka_s, va_s, qi_s, qb_s, kb_s, vb_s, kw_s, page_table,
                                  cache_idx_k[l], cache_a_k[l], cache_a_v[l], cache_b_k[l], cache_b_v[l])
    x1_s = _merge(xs, oa_s, ob_s, sga_s, sgb_s, wpa, wpb, wo)
    ys, up_s = _ffn_sample(x1_s, n2, wup, cw, cb, wdn, state_conv[l, :, 0], state_conv[l, :, 1])

    def rows5(a, n, s, h):
        a = a.reshape(a.shape[0], h, HEAD_DIM, a.shape[2])
        return jnp.transpose(a, (0, 3, 1, 2)).reshape(1, n, s, h, HEAD_DIM)

    def rows4(a, n, s):
        return jnp.transpose(a, (0, 2, 1)).reshape(1, n, s, IDX_DIM)

    return (
        yp.reshape(batch, t, d), ys.reshape(db, 1, d),
        rows5(kat, batch, t, A_KV_HEADS), rows5(vat, batch, t, A_KV_HEADS), rows4(kit, batch, t),
        rows5(kbt, batch, t, B_KV_HEADS), rows5(vbt, batch, t, B_KV_HEADS),
        p_conv[None],
        rows5(kat_s, db, 1, A_KV_HEADS), rows5(vat_s, db, 1, A_KV_HEADS), rows4(kit_s, db, 1),
        rows5(kbt_s, db, 1, B_KV_HEADS), rows5(vbt_s, db, 1, B_KV_HEADS),
        jnp.stack([state_conv[l, :, 1], up_s], axis=1)[None],
    )
```

```python
import functools

import jax
import jax.numpy as jnp
from jax import lax
from jax.experimental import pallas as pl
from jax.experimental.pallas import tpu as pltpu

HEAD_DIM = 64
A_HEADS = 8
A_KV_HEADS = 2
IDX_HEADS = 8
IDX_DIM = 64
IDX_TOPK = 256
B_HEADS = 8
B_KV_HEADS = 2
MOBA_BLOCK = 256
MOBA_TOPK = 3
ROPE_THETA = 10000.0
EPS = 1e-6
NEG = -1e30
BIG = 3e38

LANE = 128
Q_BLOCK = 256
ROW_BLOCK = 256
VMEM_LIMIT = 56 * 1024 * 1024
BISECT_STEPS = 16
SAMPLE_SPLIT = 16
SAMPLE_SPLIT_ROUNDS = 4

F32 = jnp.float32
BF16 = jnp.bfloat16

A_W = A_HEADS * HEAD_DIM
A_KV_W = A_KV_HEADS * HEAD_DIM
I_W = IDX_HEADS * IDX_DIM
B_W = B_HEADS * HEAD_DIM
B_KV_W = B_KV_HEADS * HEAD_DIM
assert A_W == 512 and I_W == 512 and B_W == 512 and A_KV_W == LANE and B_KV_W == LANE


def _params(n_grid):
    return pltpu.CompilerParams(dimension_semantics=("arbitrary",) * n_grid,
                                vmem_limit_bytes=VMEM_LIMIT)


def _dot(a, b):
    return jnp.dot(a, b, preferred_element_type=F32)


def _dot_nt(a, b):
    return lax.dot_general(a, b, (((1,), (1,)), ((), ())), preferred_element_type=F32)


def _iota(shape, dim):
    return lax.broadcasted_iota(jnp.int32, shape, dim)


def _rms(x, g):
    return x * lax.rsqrt(jnp.mean(x * x, axis=-1, keepdims=True) + EPS) * g


def _proj_kernel(x_ref, n1_ref, w_ref, g_ref, cos_ref, sin_ref,
                 qa_ref, ka_ref, va_ref, qi_ref, qb_ref, kb_ref, vb_ref, sga_ref, sgb_ref, kw_ref,
                 kat_ref, vat_ref, kbt_ref, vbt_ref, kit_ref, *, d_model):
    h = _rms(x_ref[...], n1_ref[...]).astype(BF16)
    gr = lax.shift_right_logical(_iota((2 * LANE, 2 * LANE), 0), 6)
    gc = lax.shift_right_logical(_iota((2 * LANE, 2 * LANE), 1), 6)
    gsum = jnp.where(gr == gc, 1.0, 0.0).astype(BF16)
    cos1 = cos_ref[...]
    sin1 = sin_ref[...]

    def seg(off, width):
        return _dot(h, w_ref[:, off:off + width])

    def head_norm(x, off):
        width = x.shape[1]
        ss = _dot((x * x).astype(BF16), gsum[:width, :width])
        return x * lax.rsqrt(ss * (1.0 / HEAD_DIM) + EPS) * g_ref[:, off:off + width]

    def rope(x):
        width = x.shape[1]
        rep = width // LANE
        cs = jnp.concatenate([cos1] * rep, axis=1) if rep > 1 else cos1
        sn = jnp.concatenate([sin1] * rep, axis=1) if rep > 1 else sin1
        hi = (_iota(x.shape, 1) & (HEAD_DIM // 2)) != 0
        swapped = jnp.where(hi, pltpu.roll(x, HEAD_DIM // 2, axis=1),
                            pltpu.roll(x, width - HEAD_DIM // 2, axis=1))
        return x * cs + swapped * sn

    o = 0
    for half in range(2):
        x = seg(o + half * 256, 256)
        qa_ref[:, half * 256:(half + 1) * 256] = rope(head_norm(x, o + half * 256))
    o = A_W
    x = seg(o, 256)
    xn = rope(head_norm(x, o))
    ka_ref[...] = xn[:, :LANE]
    va_ref[...] = x[:, LANE:]
    kat_ref[...] = xn[:, :LANE].T
    vat_ref[...] = x[:, LANE:].T
    o = A_W + 2 * LANE
    for half in range(2):
        qi_ref[:, half * 256:(half + 1) * 256] = rope(seg(o + half * 256, 256))
    o = A_W + 2 * LANE + I_W
    for half in range(2):
        x = seg(o + half * 256, 256)
        qb_ref[:, half * 256:(half + 1) * 256] = rope(head_norm(x, o + half * 256))
    o = A_W + 2 * LANE + I_W + B_W
    x = seg(o, 256)
    xn = rope(head_norm(x, o))
    kb_ref[...] = xn[:, :LANE]
    vb_ref[...] = x[:, LANE:]
    kbt_ref[...] = xn[:, :LANE].T
    vbt_ref[...] = x[:, LANE:].T
    o = A_W + 2 * LANE + I_W + B_W + 2 * LANE
    for part in range(d_model // 256):
        x = seg(o + part * 256, 256)
        sga_ref[:, part * 256:(part + 1) * 256] = 1.0 / (1.0 + jnp.exp(-x))
    o += d_model
    for part in range(d_model // 256):
        x = seg(o + part * 256, 256)
        sgb_ref[:, part * 256:(part + 1) * 256] = 1.0 / (1.0 + jnp.exp(-x))
    o += d_model
    x = seg(o, LANE)
    xn = rope(head_norm(x, o))
    kw = jnp.where(_iota(x.shape, 1) < IDX_DIM, xn, x)
    kw_ref[...] = kw
    kit_ref[...] = kw.T[:IDX_DIM]


def _proj(x2d, n1, w, gains, cos, sin, rows_per_seq):
    n, d = x2d.shape
    tm = min(ROW_BLOCK, rows_per_seq)
    nt = rows_per_seq // tm
    nseq = n // rows_per_seq
    nw = w.shape[1]
    row = lambda c: pl.BlockSpec((tm, c), lambda i: (i, 0))
    const = lambda shape: pl.BlockSpec(shape, lambda i: (0, 0))
    tab = pl.BlockSpec((tm, LANE), lambda i: (i % nt, 0))
    chan = lambda c: pl.BlockSpec((None, c, tm), lambda i: (i // nt, 0, i % nt))
    rows = [A_W, LANE, LANE, I_W, B_W, LANE, LANE, d, d, LANE]
    chans = [LANE, LANE, LANE, LANE, IDX_DIM]
    return pl.pallas_call(
        functools.partial(_proj_kernel, d_model=d),
        grid=(n // tm,),
        in_specs=[row(d), const((1, d)), const((d, nw)), const((1, nw)), tab, tab],
        out_specs=[row(c) for c in rows] + [chan(c) for c in chans],
        out_shape=[jax.ShapeDtypeStruct((n, c), F32) for c in rows]
        + [jax.ShapeDtypeStruct((nseq, c, rows_per_seq), F32) for c in chans],
        compiler_params=_params(1),
        name="proj",
    )(x2d, n1, w, gains, cos, sin)


def _head_operand(q_ref, h, want_low, scale):
    ch = q_ref[:, LANE * (h // 2):LANE * (h // 2) + LANE]
    if (h % 2 == 0) != want_low:
        ch = pltpu.roll(ch, HEAD_DIM, axis=1)
    low = _iota(ch.shape, 1) < HEAD_DIM
    keep = low if want_low else jnp.logical_not(low)
    return (jnp.where(keep, ch, 0.0) * scale).astype(BF16)


def _fill_kv(k_ref, vt_ref, kb_s, vt_s, nchunk, tc):
    kb_s[...] = k_ref[...].astype(BF16)
    extra = jnp.where(_iota((V_ROWS - HEAD_DIM, tc), 0) == 0, 1.0, 0.0)
    for c in range(nchunk):
        for kv in range(2):
            vt = vt_ref[kv * HEAD_DIM:(kv + 1) * HEAD_DIM, c * tc:(c + 1) * tc]
            vt_s[kv, c] = jnp.concatenate([vt, extra], axis=0).astype(BF16)


def _attend_chunk(kc, c, q_s, vt_s, st, n_heads, heads_per_kv, tile_bias_of=None, row_bias_of=None):
    s_s, p_s, mc_s, a_s, m_s, acc_s = st
    for h in range(n_heads):
        s = _dot_nt(kc, q_s[h])
        if tile_bias_of is not None:
            s = s + tile_bias_of(h)
        s_s[h] = s
        mc_s[h] = jnp.max(s, axis=0, keepdims=True)
    for h in range(n_heads):
        m_old = m_s[h]
        if row_bias_of is None:
            m_new = jnp.maximum(m_old, mc_s[h])
            shift = m_new
        else:
            rb = row_bias_of(h)
            m_new = jnp.maximum(m_old, mc_s[h] + rb)
            shift = m_new - rb
        p_s[h] = jnp.exp2(s_s[h] - shift).astype(BF16)
        a_s[h] = jnp.exp2(m_old - m_new)
        m_s[h] = m_new
    for h in range(n_heads):
        acc_s[h] = a_s[h] * acc_s[h] + _dot(vt_s[h // heads_per_kv, c], p_s[h])


V_ROWS = HEAD_DIM + 16
LOG2E = 1.4426950408889634


def _attention_scratch(n_heads, tq):
    row = pltpu.VMEM((n_heads, 1, tq), F32)
    return [pltpu.VMEM((n_heads, tq, tq), F32), pltpu.VMEM((n_heads, tq, tq), BF16), row, row, row,
            pltpu.VMEM((n_heads, V_ROWS, tq), F32)]


def _init_softmax(st):
    m_s, acc_s = st[-2], st[-1]
    m_s[...] = jnp.full(m_s.shape, NEG, F32)
    acc_s[...] = jnp.zeros(acc_s.shape, F32)


def _write_heads(o_ref, st, n_heads):
    acc_s = st[-1]
    for j in range(n_heads // 2):
        pair = []
        for h in (2 * j, 2 * j + 1):
            acc = acc_s[h]
            pair.append(acc[:HEAD_DIM] / acc[HEAD_DIM:HEAD_DIM + 1])
        o_ref[:, LANE * j:LANE * (j + 1)] = jnp.concatenate(pair, axis=0).T


def _select_topk(sweep, count_gt, n_sel, small, key_index, tq, t_total):
    row = lambda v: jnp.full((1, tq), v, F32)
    rowmax = sweep(lambda s, c, a: jnp.maximum(a, jnp.max(s, axis=0, keepdims=True)), row(NEG))
    rowmin = sweep(lambda s, c, a: jnp.minimum(
        a, jnp.min(jnp.where(s > 0.5 * NEG, s, BIG), axis=0, keepdims=True)), row(BIG))

    def bisect(_, lh):
        lo, hi = lh
        mid = 0.5 * (lo + hi)
        ge = count_gt(mid) >= n_sel
        return jnp.where(ge, mid, lo), jnp.where(ge, hi, mid)

    lo, _ = lax.fori_loop(0, BISECT_STEPS, bisect, (rowmin, rowmax))
    u0 = sweep(lambda s, c, a: jnp.minimum(
        a, jnp.min(jnp.where(s >= lo, s, BIG), axis=0, keepdims=True)), row(BIG))

    def walk_cond(st):
        return st[2] == 0

    def walk(st):
        u = st[0]

        def f(s, c, carry):
            cnt, nxt = carry
            gt = s > u
            cnt = cnt + jnp.sum(jnp.where(gt, 1.0, 0.0), axis=0, keepdims=True)
            nxt = jnp.minimum(nxt, jnp.min(jnp.where(gt, s, BIG), axis=0, keepdims=True))
            return cnt, nxt

        cnt, nxt = sweep(f, (row(0.0), row(BIG)))
        done = jnp.logical_or(cnt < n_sel, small)
        all_done = jnp.min(jnp.where(done, 1.0, 0.0)).astype(jnp.int32)
        return jnp.where(done, u, nxt), cnt, all_done

    u, cgt, _ = lax.while_loop(walk_cond, walk, (u0, row(0.0), jnp.int32(0)))
    vstar = jnp.where(small, NEG, u)
    need = jnp.where(small, 0.0, n_sel - cgt)
    n_eq = sweep(lambda s, c, a: a + jnp.sum(jnp.where(s == vstar, 1.0, 0.0), axis=0, keepdims=True), row(0.0))
    excess = jnp.max(jnp.where(jnp.logical_and(n_eq > need, jnp.logical_not(small)), 1.0, 0.0))

    n_jsteps = jnp.where(excess > 0.0, t_total.bit_length() + 1, 0).astype(jnp.int32)

    def jstep(_, jj):
        jlo, jhi = jj
        mid = lax.shift_right_logical(jlo + jhi, 1)
        e = sweep(lambda s, c, a: a + jnp.sum(
            jnp.where(s == vstar, jnp.where(key_index(c) < mid, 1.0, 0.0), 0.0), axis=0, keepdims=True), row(0.0))
        ge = e >= need
        return jnp.where(ge, jlo, mid), jnp.where(ge, mid, jhi)

    zero_i = jnp.zeros((1, tq), jnp.int32)
    _, jcut = lax.fori_loop(0, n_jsteps, jstep, (zero_i, zero_i + t_total))
    return vstar, jnp.where(small, 0, jcut)


def _dsa_prompt_kernel(pt_ref, qi_ref, qa_ref, kwq_ref, kwk_ref, ka_ref, vat_ref,
                       sqi_ref, swi_ref, sqbd_ref, skin_ref, skan_ref, svan_ref, cik, cak, cav,
                       o_ref, os_ref,
                       kib_s, kab_s, vt_s, qim_s, qam_s, sc_s, s_s, p_s, mc_s, a_s, m_s, acc_s,
                       bik, bak, bav, sem, scs_s, *, n_sel, t_total, n_sel_s, rider):
    st = (s_s, p_s, mc_s, a_s, m_s, acc_s)
    tq = qi_ref.shape[0]
    tc = tq
    nchunk = t_total // tc
    i = pl.program_id(1)

    def sample_compute(k, slot):
        os_ref[k] = _dsa_sample_row(sqi_ref[k], swi_ref[k], sqbd_ref[k], skin_ref[k], skan_ref[k], svan_ref[k],
                                    bik, bak, bav, slot, scs_s, n_sel_s, rider["past"])

    def zero_pad_rows():
        bik[...] = jnp.zeros(bik.shape, F32)

    rps = rider["rows_per_step"]
    run_row = _sample_row_runner(pt_ref, pl.program_id(0) * pl.num_programs(1) + i, rps, rider["n_rows"],
                                 (cik, cak, cav), (bik, bak, bav), sem, rider["n_pages"], sample_compute,
                                 zero_pad_rows)
    for k in range((rps + 1) // 2):
        run_row(k)

    @pl.when(i == 0)
    def _():
        kib_s[...] = kwk_ref[...].astype(BF16)
        _fill_kv(ka_ref, vat_ref, kab_s, vt_s, nchunk, tc)

    heads_per_kv = A_HEADS // A_KV_HEADS
    for h in range(IDX_HEADS):
        qim_s[h] = _head_operand(qi_ref, h, True, 1.0)
    for h in range(A_HEADS):
        qam_s[h] = _head_operand(qa_ref, h, (h // heads_per_kv) == 0, HEAD_DIM ** -0.5 * LOG2E)
    w8 = kwq_ref[...].T[IDX_DIM:IDX_DIM + IDX_HEADS, :] * (IDX_DIM ** -0.5 * IDX_HEADS ** -0.5)

    krow = _iota((tc, tq), 0)
    qcol = _iota((tc, tq), 1)

    def chunk(c):
        return pl.ds(pl.multiple_of(c * tc, tc), tc)

    def score_chunk(c, diag):
        kc = kib_s[chunk(c), :]
        a = jnp.zeros((tc, tq), F32)
        for h in range(IDX_HEADS):
            a = a + jnp.maximum(_dot_nt(kc, qim_s[h]), 0.0) * w8[h:h + 1, :]
        if diag:
            a = jnp.where(krow <= qcol, a, NEG)
        sc_s[chunk(c), :] = a

    def score_body(c, carry):
        score_chunk(c, False)
        return carry

    lax.fori_loop(0, i, score_body, 0)
    score_chunk(i, True)

    def sweep(fn, init):
        return lax.fori_loop(0, i + 1, lambda c, carry: fn(sc_s[chunk(c), :], c, carry), init)

    def count_gt(x):
        return sweep(lambda s, c, a: a + jnp.sum(jnp.where(s > x, 1.0, 0.0), axis=0, keepdims=True),
                     jnp.zeros((1, tq), F32))

    n_adm = i * tq + _iota((1, tq), 1) + 1
    small = n_adm <= n_sel
    vstar, jcut = _select_topk(sweep, count_gt, float(n_sel), small,
                               lambda c: c * tc + krow, tq, t_total)

    def bias_body(c, carry):
        s = sc_s[chunk(c), :]
        tie = jnp.where((c * tc + krow) < jcut, 0.0, NEG)
        sc_s[chunk(c), :] = jnp.where(s > vstar, 0.0, jnp.where(s == vstar, tie, NEG))
        return carry

    lax.fori_loop(0, i + 1, bias_body, 0)

    for k in range((rps + 1) // 2, rps):
        run_row(k)

    _init_softmax(st)

    def att_body(c, carry):
        _attend_chunk(kab_s[chunk(c), :], c, qam_s, vt_s, st, A_HEADS, heads_per_kv,
                      tile_bias_of=lambda h: sc_s[chunk(c), :])
        return carry

    lax.fori_loop(0, i + 1, att_body, 0)
    _write_heads(o_ref, st, A_HEADS)


def _dsa_branch(qi, qa, kw, ka, vat, batch, t, page_table, sample_inputs, caches):
    tq = Q_BLOCK
    nq = t // tq
    n_sel = min(IDX_TOPK, t // 4)
    qblk = lambda c: pl.BlockSpec((tq, c), lambda b, i, pt: (b * nq + i, 0))
    full = pl.BlockSpec((t, LANE), lambda b, i, pt: (b, 0))
    full_t = pl.BlockSpec((None, LANE, t), lambda b, i, pt: (b, 0, 0))
    rd = _rider(page_table, sample_inputs, caches, batch, nq)
    past = rd["static"]["past"]
    grid_spec = pltpu.PrefetchScalarGridSpec(
        num_scalar_prefetch=1,
        grid=(batch, nq),
        in_specs=[qblk(I_W), qblk(A_W), qblk(LANE), full, full, full_t] + rd["in_specs"],
        out_specs=[qblk(A_W), rd["out_spec"]],
        scratch_shapes=[
            pltpu.VMEM((t, LANE), BF16), pltpu.VMEM((t, LANE), BF16),
            pltpu.VMEM((2, nq, V_ROWS, tq), BF16),
            pltpu.VMEM((IDX_HEADS, tq, LANE), BF16), pltpu.VMEM((A_HEADS, tq, LANE), BF16),
            pltpu.VMEM((t, tq), F32),
        ] + _attention_scratch(A_HEADS, tq) + rd["scratch"]
        + [pltpu.VMEM((N_KEY_CHUNKS, past // N_KEY_CHUNKS), F32)],
    )
    return pl.pallas_call(
        functools.partial(_dsa_prompt_kernel, n_sel=n_sel, t_total=t,
                          n_sel_s=float(min(IDX_TOPK, (past + 1) // 4)), rider=rd["static"]),
        grid_spec=grid_spec,
        out_shape=[jax.ShapeDtypeStruct((batch * t, A_W), F32), rd["out_shape"]],
        compiler_params=_params(2),
        name="dsa",
    )(page_table, qi, qa, kw, kw, ka, vat, *sample_inputs, *caches)


def _moba_prompt_kernel(pt_ref, qb_ref, kb_ref, vbt_ref, sqbd_ref, skbn_ref, svbn_ref, cbk, cbv,
                        o_ref, os_ref,
                        kbb_s, vt_s, kbar_s, qbm_s, selb_s, s_s, p_s, mc_s, a_s, m_s, acc_s,
                        bbk, bbv, sem, *, n_blk, t_total, n_blk_s, rider):
    st = (s_s, p_s, mc_s, a_s, m_s, acc_s)
    tq = qb_ref.shape[0]
    tc = tq
    nb = t_total // tc
    nbp = kbar_s.shape[0]
    i = pl.program_id(1)
    heads_per_kv = B_HEADS // B_KV_HEADS

    def sample_compute(k, slot):
        os_ref[k] = _moba_sample_row(sqbd_ref[k], skbn_ref[k], svbn_ref[k], bbk, bbv, slot, n_blk_s, rider["past"])

    rps = rider["rows_per_step"]
    run_row = _sample_row_runner(pt_ref, pl.program_id(0) * pl.num_programs(1) + i, rps, rider["n_rows"],
                                 (cbk, cbv), (bbk, bbv), sem, rider["n_pages"], sample_compute)
    for k in range((rps + 1) // 2):
        run_row(k)

    @pl.when(i == 0)
    def _():
        _fill_kv(kb_ref, vbt_ref, kbb_s, vt_s, nb, tc)
        kbar_s[...] = jnp.zeros(kbar_s.shape, F32)
        for n in range(nb):
            kbar_s[n:n + 1, :] = jnp.mean(kb_ref[n * tc:(n + 1) * tc, :], axis=0, keepdims=True)

    kbar = kbar_s[...].astype(BF16)
    blk = _iota((nbp, tq), 0)
    past = blk < i
    for h in range(B_HEADS):
        qm = _head_operand(qb_ref, h, (h // heads_per_kv) == 0, HEAD_DIM ** -0.5 * LOG2E)
        qbm_s[h] = qm
        gate = jnp.where(past, _dot_nt(kbar, qm), NEG)
        rank = jnp.zeros((nbp, tq), F32)
        for m in range(nb):
            gm = gate[m:m + 1, :]
            first = jnp.where(blk > m, 1.0, 0.0)
            rank = rank + jnp.where(gm > gate, 1.0, jnp.where(gm == gate, first, 0.0))
        sel = jnp.logical_and(past, rank < n_blk)
        selb_s[h] = jnp.where(sel, 0.0, NEG)

    _init_softmax(st)
    krow = _iota((tc, tq), 0)
    qcol = _iota((tc, tq), 1)

    def chunk(c):
        return pl.ds(pl.multiple_of(c * tc, tc), tc)

    _attend_chunk(kbb_s[chunk(i), :], i, qbm_s, vt_s, st, B_HEADS, heads_per_kv,
                  tile_bias_of=lambda h: jnp.where(krow <= qcol, 0.0, NEG))

    for k in range((rps + 1) // 2, rps):
        run_row(k)

    def att_body(c, carry):
        _attend_chunk(kbb_s[chunk(c), :], c, qbm_s, vt_s, st, B_HEADS, heads_per_kv,
                      row_bias_of=lambda h: selb_s[h, pl.ds(c, 1), :])
        return carry

    lax.fori_loop(0, i, att_body, 0)
    _write_heads(o_ref, st, B_HEADS)


def _moba_branch(qb, kb, vbt, batch, t, page_table, sample_inputs, caches):
    tq = Q_BLOCK
    assert tq == MOBA_BLOCK and t % tq == 0
    nq = t // tq
    n_blk = min(MOBA_TOPK, (t - 1) // MOBA_BLOCK)
    qblk = pl.BlockSpec((tq, B_W), lambda b, i, pt: (b * nq + i, 0))
    full = pl.BlockSpec((t, LANE), lambda b, i, pt: (b, 0))
    full_t = pl.BlockSpec((None, LANE, t), lambda b, i, pt: (b, 0, 0))
    rd = _rider(page_table, sample_inputs, caches, batch, nq)
    past = rd["static"]["past"]
    grid_spec = pltpu.PrefetchScalarGridSpec(
        num_scalar_prefetch=1,
        grid=(batch, nq),
        in_specs=[qblk, full, full_t] + rd["in_specs"],
        out_specs=[qblk, rd["out_spec"]],
        scratch_shapes=[
            pltpu.VMEM((t, LANE), BF16),
            pltpu.VMEM((2, nq, V_ROWS, tq), BF16),
            pltpu.VMEM((max(nq, 8), LANE), F32),
            pltpu.VMEM((B_HEADS, tq, LANE), BF16),
            pltpu.VMEM((B_HEADS, max(nq, 8), tq), F32),
        ] + _attention_scratch(B_HEADS, tq) + rd["scratch"],
    )
    return pl.pallas_call(
        functools.partial(_moba_prompt_kernel, n_blk=float(n_blk), t_total=t,
                          n_blk_s=float(min(MOBA_TOPK, past // MOBA_BLOCK)), rider=rd["static"]),
        grid_spec=grid_spec,
        out_shape=[jax.ShapeDtypeStruct((batch * t, B_W), F32), rd["out_shape"]],
        compiler_params=_params(2),
        name="moba",
    )(page_table, qb, kb, vbt, *sample_inputs, *caches)


def _page_copy(cache, buf, sem, k, slot, page, p):
    rows, width = cache.shape[1], cache.shape[2]
    return pltpu.make_async_copy(cache.at[page], buf.at[slot, pl.ds(0, rows), pl.ds(p * width, width)],
                                 sem.at[k, slot])


def _gather(pt_ref, b, caches, bufs, sem, slot, n_pages, start):
    for k, (cache, buf) in enumerate(zip(caches, bufs)):
        for p in range(n_pages):
            cp = _page_copy(cache, buf, sem, k, slot, pt_ref[b, p] if start else 0, p)
            if start:
                cp.start()
            else:
                cp.wait()


def _sample_row_runner(pt_ref, step, rows_per_step, n_rows, caches, bufs, sem, n_pages, compute, before_first=None):
    def run(k):
        r = step * rows_per_step + k
        slot = lax.rem(r, 2)

        @pl.when(r == 0)
        def _():
            if before_first is not None:
                before_first()
            _gather(pt_ref, r, caches, bufs, sem, slot, n_pages, True)

        @pl.when(r + 1 < n_rows)
        def _():
            _gather(pt_ref, r + 1, caches, bufs, sem, 1 - slot, n_pages, True)

        _gather(pt_ref, r, caches, bufs, sem, slot, n_pages, False)
        compute(k, slot)

    return run


def _sample_softmax(tiles, s_new, vt_tiles, v_new):
    m = s_new
    for s in tiles:
        m = jnp.maximum(m, jnp.max(s, axis=1, keepdims=True))
    p_new = jnp.exp(s_new - m)
    l = p_new
    o = p_new * v_new
    for s, vt in zip(tiles, vt_tiles):
        p = jnp.exp(s - m)
        l = l + jnp.sum(p, axis=1, keepdims=True)
        o = o + _dot_nt(p.astype(BF16), vt)
    return o / l


N_KEY_CHUNKS = 8


def _dsa_sample_row(qi, wi, qbd, kin, kan, van, bik, bak, bav, slot, sc_s, n_sel, past):
    nq = N_KEY_CHUNKS
    ch = past // nq
    qib = qi.astype(BF16)
    wrow = wi * (IDX_DIM ** -0.5 * IDX_HEADS ** -0.5)
    eye = _iota((IDX_HEADS, IDX_HEADS), 0) == _iota((IDX_HEADS, IDX_HEADS), 1)
    wcol = jnp.sum(jnp.where(eye, jnp.broadcast_to(wrow, (IDX_HEADS, IDX_HEADS)), 0.0), axis=1, keepdims=True)
    lg_new = jnp.sum(qi * kin, axis=1, keepdims=True)
    s_new = jnp.sum(jnp.maximum(lg_new, 0.0) * wcol, axis=0, keepdims=True)

    for q in range(nq):
        kq = bik[slot, :, q * ch:(q + 1) * ch].astype(BF16)
        lg = jnp.maximum(_dot(qib, kq), 0.0)
        sc_s[q:q + 1, :] = jnp.sum(lg * wcol, axis=0, keepdims=True)

    sc = sc_s[...]
    kidx = _iota((nq, ch), 0) * ch + _iota((nq, ch), 1)

    def total(x):
        return jnp.sum(jnp.sum(x, axis=1, keepdims=True), axis=0, keepdims=True)

    def tmin(x):
        return jnp.min(jnp.min(x, axis=1, keepdims=True), axis=0, keepdims=True)

    def count_gt(x):
        return total(jnp.where(sc > x, 1.0, 0.0)) + jnp.where(s_new > x, 1.0, 0.0)

    hi0 = jnp.maximum(-tmin(-sc), s_new)
    lo0 = jnp.minimum(tmin(sc), s_new)

    lo, hi = lo0, hi0
    for _ in range(SAMPLE_SPLIT_ROUNDS):
        step = (hi - lo) * (1.0 / SAMPLE_SPLIT)
        new_lo, new_hi = lo, hi
        for j in range(1, SAMPLE_SPLIT):
            tj = lo + step * float(j)
            ge = count_gt(tj) >= n_sel
            new_lo = jnp.where(ge, jnp.maximum(new_lo, tj), new_lo)
            new_hi = jnp.where(ge, new_hi, jnp.minimum(new_hi, tj))
        lo, hi = new_lo, new_hi
    u0 = jnp.minimum(tmin(jnp.where(sc >= lo, sc, BIG)), jnp.where(s_new >= lo, s_new, BIG))

    def walk_cond(st):
        return st[2] == 0

    def walk(st):
        u = st[0]
        cnt = count_gt(u)
        nxt = jnp.minimum(tmin(jnp.where(sc > u, sc, BIG)), jnp.where(s_new > u, s_new, BIG))
        done = cnt < n_sel
        return jnp.where(done, u, nxt), cnt, jnp.min(jnp.where(done, 1.0, 0.0)).astype(jnp.int32)

    vstar, cgt, _ = lax.while_loop(walk_cond, walk, (u0, jnp.zeros((1, 1), F32), jnp.int32(0)))
    need = n_sel - cgt
    n_eq = total(jnp.where(sc == vstar, 1.0, 0.0)) + jnp.where(s_new == vstar, 1.0, 0.0)
    n_jsteps = jnp.where(jnp.max(n_eq - need) > 0.0, (past + 1).bit_length() + 1, 0).astype(jnp.int32)

    def jstep(_, jj):
        jlo, jhi = jj
        mid = lax.shift_right_logical(jlo + jhi, 1)
        e = total(jnp.where(sc == vstar, jnp.where(kidx < mid, 1.0, 0.0), 0.0)) + jnp.where(
            s_new == vstar, jnp.where(past < mid, 1.0, 0.0), 0.0)
        ge = e >= need
        return jnp.where(ge, jlo, mid), jnp.where(ge, mid, jhi)

    zero_i = jnp.zeros((1, 1), jnp.int32)
    _, jcut = lax.fori_loop(0, n_jsteps, jstep, (zero_i, zero_i + (past + 1)))
    bias = jnp.where(sc > vstar, 0.0, jnp.where(sc == vstar, jnp.where(kidx < jcut, 0.0, NEG), NEG))
    bias_new = jnp.where(s_new > vstar, 0.0, jnp.where(s_new == vstar, jnp.where(past < jcut, 0.0, NEG), NEG))

    qs = qbd * (HEAD_DIM ** -0.5)
    sn = jnp.sum(qs * kan, axis=1, keepdims=True) + bias_new
    qsb = qs.astype(BF16)
    tiles, vts = [], []
    for q in range(nq):
        kq = bak[slot, :, q * ch:(q + 1) * ch].astype(BF16)
        tiles.append(_dot(qsb, kq) + bias[q:q + 1, :])
        vts.append(bav[slot, :, q * ch:(q + 1) * ch].astype(BF16))
    return _sample_softmax(tiles, sn, vts, van)


def _moba_sample_row(qbd, kbn, vbn, bbk, bbv, slot, n_blk, past):
    nq = N_KEY_CHUNKS
    ch = past // nq
    nblk = past // MOBA_BLOCK
    bpc = ch // MOBA_BLOCK
    qs = qbd * (HEAD_DIM ** -0.5)
    qsb = qs.astype(BF16)
    sn = jnp.sum(qs * kbn, axis=1, keepdims=True)
    bcol = _iota((B_HEADS, nblk), 1).astype(F32)

    raw, vts = [], []
    gate = jnp.zeros((B_HEADS, nblk), F32)
    for q in range(nq):
        s = _dot(qsb, bbk[slot, :, q * ch:(q + 1) * ch].astype(BF16))
        raw.append(s)
        vts.append(bbv[slot, :, q * ch:(q + 1) * ch].astype(BF16))
        for k in range(bpc):
            g = jnp.sum(s[:, k * MOBA_BLOCK:(k + 1) * MOBA_BLOCK], axis=1, keepdims=True) * (1.0 / MOBA_BLOCK)
            gate = jnp.where(bcol == float(q * bpc + k), g, gate)
    selm = jnp.zeros((B_HEADS, nblk), F32)
    for _ in range(int(n_blk)):
        mx = jnp.max(gate, axis=1, keepdims=True)
        first = jnp.min(jnp.where(gate == mx, bcol, float(nblk)), axis=1, keepdims=True)
        hit = bcol == first
        selm = jnp.where(hit, 1.0, selm)
        gate = jnp.where(hit, 2.0 * NEG, gate)
    tiles = []
    for q in range(nq):
        bias = jnp.concatenate(
            [jnp.broadcast_to(jnp.where(selm[:, q * bpc + k:q * bpc + k + 1] > 0.0, 0.0, NEG),
                              (B_HEADS, MOBA_BLOCK)) for k in range(bpc)], axis=1)
        tiles.append(raw[q] + bias)
    return _sample_softmax(tiles, sn, vts, vbn)


def _rider(page_table, small_inputs, caches, batch, nq):
    db, n_pages = page_table.shape
    page = caches[0].shape[2]
    past = n_pages * page
    n_steps = batch * nq
    assert past % (N_KEY_CHUNKS * MOBA_BLOCK) == 0 and page % LANE == 0 and db % n_steps == 0
    rps = db // n_steps
    per_step = lambda a: pl.BlockSpec((rps,) + a.shape[1:], lambda b, i, pt: (b * nq + i, 0, 0))
    return dict(
        in_specs=[per_step(a) for a in small_inputs] + [pl.BlockSpec(memory_space=pl.ANY)] * len(caches),
        out_spec=pl.BlockSpec((rps, 8, LANE), lambda b, i, pt: (b * nq + i, 0, 0)),
        out_shape=jax.ShapeDtypeStruct((db, 8, LANE), F32),
        scratch=[pltpu.VMEM((2, LANE, past), F32) for _ in caches] + [pltpu.SemaphoreType.DMA((len(caches), 2))],
        static=dict(n_rows=db, rows_per_step=rps, n_pages=n_pages, past=past),
    )


def _merge_kernel(x_ref, oa_ref, ob_ref, sga_ref, sgb_ref, wpa_ref, wpb_ref, wo_ref, y_ref):
    pa = _dot(oa_ref[...].astype(BF16), wpa_ref[...])
    pb = _dot(ob_ref[...].astype(BF16), wpb_ref[...])
    merged = sga_ref[...] * pa + sgb_ref[...] * pb
    y_ref[...] = x_ref[...] + _dot(merged.astype(BF16), wo_ref[...])


def _merge(x2d, oa, ob, sga, sgb, wpa, wpb, wo):
    n, d = x2d.shape
    tm = min(ROW_BLOCK, n)
    row = lambda c: pl.BlockSpec((tm, c), lambda i: (i, 0))
    const = lambda a: pl.BlockSpec(a.shape, lambda i: (0, 0))
    return pl.pallas_call(
        _merge_kernel,
        grid=(n // tm,),
        in_specs=[row(d), row(A_W), row(B_W), row(d), row(d), const(wpa), const(wpb), const(wo)],
        out_specs=row(d),
        out_shape=jax.ShapeDtypeStruct((n, d), F32),
        compiler_params=_params(1),
        name="merge",
    )(x2d, oa, ob, sga, sgb, wpa, wpb, wo)


FFN_COL_CHUNK = 1408


def _ffn_prompt_kernel(x_ref, n2_ref, wup_ref, cw_ref, cb_ref, wdn_ref, prev_ref, y_ref, tail_ref, ext_s, *, d_ff):
    tm = x_ref.shape[0]
    i = pl.program_id(1)

    @pl.when(i == 0)
    def _():
        ext_s[6:8, :] = prev_ref[...]

    x = x_ref[...]
    xn = _rms(x, n2_ref[...]).astype(BF16)
    fc = FFN_COL_CHUNK
    for c in range(2 * d_ff // fc):
        ext_s[8:8 + tm, c * fc:(c + 1) * fc] = _dot(xn, wup_ref[:, c * fc:(c + 1) * fc])

    def conv(lo):
        cols = slice(lo, lo + fc)
        out = cb_ref[:, cols] + ext_s[6:6 + tm, cols] * cw_ref[0:1, cols]
        out = out + ext_s[7:7 + tm, cols] * cw_ref[1:2, cols]
        return out + ext_s[8:8 + tm, cols] * cw_ref[2:3, cols]

    y = x
    for j in range(d_ff // fc):
        a = conv(j * fc)
        g = conv(d_ff + j * fc)
        act = (a / (1.0 + jnp.exp(-a)) * g).astype(BF16)
        y = y + _dot(act, wdn_ref[j * fc:(j + 1) * fc, :])
    y_ref[...] = y
    tail = ext_s[tm + 6:tm + 8, :]
    tail_ref[...] = tail
    ext_s[6:8, :] = tail


def _ffn_prompt(x2d, n2, wup, cw, cb, wdn, prev, batch, t):
    n, d = x2d.shape
    d_ff = wdn.shape[0]
    assert d_ff % FFN_COL_CHUNK == 0
    tm = ROW_BLOCK
    nt = t // tm
    row = pl.BlockSpec((tm, d), lambda b, i: (b * nt + i, 0))
    const = lambda a: pl.BlockSpec(a.shape, lambda b, i: (0, 0), pipeline_mode=pl.Buffered(1))
    per_b = pl.BlockSpec((None, 2, 2 * d_ff), lambda b, i: (b, 0, 0))
    return pl.pallas_call(
        functools.partial(_ffn_prompt_kernel, d_ff=d_ff),
        grid=(batch, nt),
        in_specs=[row, const(n2), const(wup), const(cw), const(cb), const(wdn), per_b],
        out_specs=[row, per_b],
        out_shape=[jax.ShapeDtypeStruct((n, d), F32), jax.ShapeDtypeStruct((batch, 2, 2 * d_ff), F32)],
        scratch_shapes=[pltpu.VMEM((tm + 8, 2 * d_ff), F32)],
        compiler_params=_params(2),
        name="ffn_prompt",
    )(x2d, n2, wup, cw, cb, wdn, prev)


def _ffn_sample_kernel(x_ref, n2_ref, wup_ref, cw_ref, cb_ref, wdn_ref, s0_ref, s1_ref, y_ref, up_ref, *, d_ff):
    x = x_ref[...]
    xn = _rms(x, n2_ref[...]).astype(BF16)
    fc = FFN_COL_CHUNK
    for c in range(2 * d_ff // fc):
        up_ref[:, c * fc:(c + 1) * fc] = _dot(xn, wup_ref[:, c * fc:(c + 1) * fc])

    def conv(lo):
        cols = slice(lo, lo + fc)
        out = cb_ref[:, cols] + s0_ref[:, cols] * cw_ref[0:1, cols]
        out = out + s1_ref[:, cols] * cw_ref[1:2, cols]
        return out + up_ref[:, cols] * cw_ref[2:3, cols]

    y = x
    for j in range(d_ff // fc):
        a = conv(j * fc)
        g = conv(d_ff + j * fc)
        act = (a / (1.0 + jnp.exp(-a)) * g).astype(BF16)
        y = y + _dot(act, wdn_ref[j * fc:(j + 1) * fc, :])
    y_ref[...] = y


def _ffn_sample(x2d, n2, wup, cw, cb, wdn, s0, s1):
    n, d = x2d.shape
    d_ff = wdn.shape[0]
    full = lambda a: pl.BlockSpec(a.shape, lambda i: (0, 0), pipeline_mode=pl.Buffered(1))
    args = (x2d, n2, wup, cw, cb, wdn, s0, s1)
    return pl.pallas_call(
        functools.partial(_ffn_sample_kernel, d_ff=d_ff),
        grid=(1,),
        in_specs=[full(a) for a in args],
        out_specs=[pl.BlockSpec((n, d), lambda i: (0, 0)), pl.BlockSpec((n, 2 * d_ff), lambda i: (0, 0))],
        out_shape=[jax.ShapeDtypeStruct((n, d), F32), jax.ShapeDtypeStruct((n, 2 * d_ff), F32)],
        compiler_params=_params(1),
        name="ffn_sample",
    )(*args)


def _rope_tables(pos):
    half = HEAD_DIM // 2
    inv = ROPE_THETA ** (-jnp.arange(half, dtype=F32) / half)
    ang = pos.astype(F32)[:, None] * inv[None, :]
    cos = jnp.cos(ang)
    sin = jnp.sin(ang)
    return jnp.tile(cos, (1, 4)), jnp.tile(jnp.concatenate([-sin, sin], axis=1), (1, 2))


def _layout_w_in(w_in, d_model):
    splits = (A_W, A_KV_W, A_KV_W, I_W, IDX_DIM, IDX_HEADS, B_W, B_KV_W, B_KV_W, d_model, d_model)
    offs = [0]
    for s in splits:
        offs.append(offs[-1] + s)
    p = [w_in[:, offs[k]:offs[k + 1]] for k in range(len(splits))]
    qa, ka, va, qi, ki, wi, qb, kb, vb, ga, gb = p
    pad = jnp.zeros((w_in.shape[0], LANE - IDX_DIM - IDX_HEADS), w_in.dtype)
    return jnp.concatenate([qa, ka, va, qi, qb, kb, vb, ga, gb, ki, wi, pad], axis=1).astype(BF16)


def _layout_gains(q_norm_a, k_norm_a, k_norm_idx, q_norm_b, k_norm_b, d_model):
    one = lambda n: jnp.ones((n,), F32)
    return jnp.concatenate([
        jnp.tile(q_norm_a, A_HEADS), jnp.tile(k_norm_a, A_KV_HEADS), one(A_KV_W), one(I_W),
        jnp.tile(q_norm_b, B_HEADS), jnp.tile(k_norm_b, B_KV_HEADS), one(B_KV_W),
        one(2 * d_model), k_norm_idx, one(LANE - IDX_DIM)])[None, :]


def _block_diag_q(q, heads_per_kv):
    z = jnp.zeros_like(q)
    low = jnp.concatenate([q, z], axis=-1)
    high = jnp.concatenate([z, q], axis=-1)
    is_low = (jnp.arange(q.shape[1]) // heads_per_kv == 0)[None, :, None]
    return jnp.where(is_low, low, high)


def _pick_kv(o, heads_per_kv):
    n = o.shape[0]
    return jnp.concatenate([o[:, :heads_per_kv, :HEAD_DIM].reshape(n, -1),
                            o[:, heads_per_kv:, HEAD_DIM:].reshape(n, -1)], axis=1)


def _sample_operands(qa_s, ka_s, va_s, qi_s, qb_s, kb_s, vb_s, kw_s, c_ik, c_ak, c_av, c_bk, c_bv):
    db = qa_s.shape[0]
    n_pool, page = c_ak.shape[0], c_ak.shape[1]
    hpk_a = A_HEADS // A_KV_HEADS
    hpk_b = B_HEADS // B_KV_HEADS
    qi_pad = jnp.pad(qi_s.reshape(db, IDX_HEADS, IDX_DIM), ((0, 0), (0, 0), (0, LANE - IDX_DIM)))
    wi_s = kw_s[:, IDX_DIM:IDX_DIM + IDX_HEADS].reshape(db, 1, IDX_HEADS)
    qa_bd = _block_diag_q(qa_s.reshape(db, A_HEADS, HEAD_DIM), hpk_a)
    qb_bd = _block_diag_q(qb_s.reshape(db, B_HEADS, HEAD_DIM), hpk_b)
    kin = jnp.where(jnp.arange(LANE) < IDX_DIM, kw_s, 0.0).reshape(db, 1, LANE)
    pages_t = lambda c: jnp.moveaxis(c, 1, -1).reshape(n_pool, -1, page)
    dsa = ([qi_pad, wi_s, qa_bd, kin, ka_s.reshape(db, 1, LANE), va_s.reshape(db, 1, LANE)],
           [pages_t(c_ik), pages_t(c_ak), pages_t(c_av)])
    moba = ([qb_bd, kb_s.reshape(db, 1, LANE), vb_s.reshape(db, 1, LANE)], [pages_t(c_bk), pages_t(c_bv)])
    return dsa, moba


def kernel(x_prompt, x_sample, cache_a_k, cache_a_v, cache_idx_k, cache_b_k, cache_b_v, state_conv, page_table, norm1, w_in, q_norm_a, k_norm_a, k_norm_idx, q_norm_b, k_norm_b, w_proj_a, w_proj_b, w_out, norm2, w_up, conv_w, conv_b, w_down):
    batch, t, d = x_prompt.shape
    db, ds, _ = x_sample.shape
    depth = norm1.shape[0]
    assert depth == 1 and ds == 1
    n_pool, page = cache_a_k.shape[1], cache_a_k.shape[2]
    past = page_table.shape[1] * page
    l = 0
    w = _layout_w_in(w_in[l], d)
    gains = _layout_gains(q_norm_a[l], k_norm_a[l], k_norm_idx[l], q_norm_b[l], k_norm_b[l], d)
    wpa, wpb, wo = w_proj_a[l].astype(BF16), w_proj_b[l].astype(BF16), w_out[l].astype(BF16)
    wup, wdn = w_up[l].astype(BF16), w_down[l].astype(BF16)
    n1, n2 = norm1[l][None, :], norm2[l][None, :]
    cw, cb = conv_w[l], conv_b[l][None, :]
    d_ff = wdn.shape[0]

    xp = x_prompt.reshape(batch * t, d)
    cos_p, sin_p = _rope_tables(jnp.arange(t, dtype=jnp.int32))
    qa, ka, _, qi, qb, kb, _, sga, sgb, kw, kat, vat, kbt, vbt, kit = _proj(xp, n1, w, gains, cos_p, sin_p, t)
    xs = x_sample.reshape(db, d)
    cos_s, sin_s = _rope_tables(jnp.full((db,), past, jnp.int32))
    (qa_s, ka_s, va_s, qi_s, qb_s, kb_s, vb_s, sga_s, sgb_s, kw_s,
     kat_s, vat_s, kbt_s, vbt_s, kit_s) = _proj(xs, n1, w, gains, cos_s, sin_s, db)

    dsa_s, moba_s = _sample_operands(qa_s, ka_s, va_s, qi_s, qb_s, kb_s, vb_s, kw_s,
                                     cache_idx_k[l], cache_a_k[l], cache_a_v[l], cache_b_k[l], cache_b_v[l])
    oa, oa_s = _dsa_branch(qi, qa, kw, ka, vat, batch, t, page_table, *dsa_s)
    ob, ob_s = _moba_branch(qb, kb, vbt, batch, t, page_table, *moba_s)
    oa_s = _pick_kv(oa_s, A_HEADS // A_KV_HEADS)
    ob_s = _pick_kv(ob_s, B_HEADS // B_KV_HEADS)

    x1 = _merge(xp, oa, ob, sga, sgb, wpa, wpb, wo)
    yp, p_conv = _ffn_prompt(x1, n2, wup, cw, cb, wdn, jnp.zeros((batch, 2, 2 * d_ff), F32), batch, t)
    x1_s = _merge(xs, oa_s, ob_s, sga_s, sgb_s, wpa, wpb, wo)
    ys, up_s = _ffn_sample(x1_s, n2, wup, cw, cb, wdn, state_conv[l, :, 0], state_conv[l, :, 1])

    def rows5(a, n, s, h):
        a = a.reshape(a.shape[0], h, HEAD_DIM, a.shape[2])
        return jnp.transpose(a, (0, 3, 1, 2)).reshape(1, n, s, h, HEAD_DIM)

    def rows4(a, n, s):
        return jnp.transpose(a, (0, 2, 1)).reshape(1, n, s, IDX_DIM)

    return (
        yp.reshape(batch, t, d), ys.reshape(db, 1, d),
        rows5(kat, batch, t, A_KV_HEADS), rows5(vat, batch, t, A_KV_HEADS), rows4(kit, batch, t),
        rows5(kbt, batch, t, B_KV_HEADS), rows5(vbt, batch, t, B_KV_HEADS),
        p_conv[None],
        rows5(kat_s, db, 1, A_KV_HEADS), rows5(vat_s, db, 1, A_KV_HEADS), rows4(kit_s, db, 1),
        rows5(kbt_s, db, 1, B_KV_HEADS), rows5(vbt_s, db, 1, B_KV_HEADS),
        jnp.stack([state_conv[l, :, 1], up_s], axis=1)[None],
    )
```

```python
import functools

import jax
import jax.numpy as jnp
from jax import lax
from jax.experimental import pallas as pl
from jax.experimental.pallas import tpu as pltpu

HEAD_DIM = 64
A_HEADS = 8
A_KV_HEADS = 2
IDX_HEADS = 8
IDX_DIM = 64
IDX_TOPK = 256
B_HEADS = 8
B_KV_HEADS = 2
MOBA_BLOCK = 256
MOBA_TOPK = 3
ROPE_THETA = 10000.0
EPS = 1e-6
NEG = -1e30
BIG = 3e38

LANE = 128
Q_BLOCK = 256
ROW_BLOCK = 256
VMEM_LIMIT = 56 * 1024 * 1024
BISECT_STEPS = 13
SAMPLE_SPLIT = 16
SAMPLE_SPLIT_ROUNDS = 4

F32 = jnp.float32
BF16 = jnp.bfloat16

A_W = A_HEADS * HEAD_DIM
A_KV_W = A_KV_HEADS * HEAD_DIM
I_W = IDX_HEADS * IDX_DIM
B_W = B_HEADS * HEAD_DIM
B_KV_W = B_KV_HEADS * HEAD_DIM
assert A_W == 512 and I_W == 512 and B_W == 512 and A_KV_W == LANE and B_KV_W == LANE


def _params(n_grid):
    return pltpu.CompilerParams(dimension_semantics=("arbitrary",) * n_grid,
                                vmem_limit_bytes=VMEM_LIMIT)


def _dot(a, b):
    return jnp.dot(a, b, preferred_element_type=F32)


def _dot_nt(a, b):
    return lax.dot_general(a, b, (((1,), (1,)), ((), ())), preferred_element_type=F32)


def _iota(shape, dim):
    return lax.broadcasted_iota(jnp.int32, shape, dim)


def _rms(x, g):
    return x * lax.rsqrt(jnp.mean(x * x, axis=-1, keepdims=True) + EPS) * g


def _proj_kernel(x_ref, n1_ref, w_ref, g_ref, cos_ref, sin_ref,
                 qa_ref, ka_ref, va_ref, qi_ref, qb_ref, kb_ref, vb_ref, sga_ref, sgb_ref, kw_ref,
                 kat_ref, vat_ref, kbt_ref, vbt_ref, kit_ref, *, d_model):
    h = _rms(x_ref[...], n1_ref[...]).astype(BF16)
    gr = lax.shift_right_logical(_iota((2 * LANE, 2 * LANE), 0), 6)
    gc = lax.shift_right_logical(_iota((2 * LANE, 2 * LANE), 1), 6)
    gsum = jnp.where(gr == gc, 1.0, 0.0).astype(BF16)
    cos1 = cos_ref[...]
    sin1 = sin_ref[...]

    def seg(off, width):
        return _dot(h, w_ref[:, off:off + width])

    def head_norm(x, off):
        width = x.shape[1]
        ss = _dot((x * x).astype(BF16), gsum[:width, :width])
        return x * lax.rsqrt(ss * (1.0 / HEAD_DIM) + EPS) * g_ref[:, off:off + width]

    def rope(x):
        width = x.shape[1]
        rep = width // LANE
        cs = jnp.concatenate([cos1] * rep, axis=1) if rep > 1 else cos1
        sn = jnp.concatenate([sin1] * rep, axis=1) if rep > 1 else sin1
        hi = (_iota(x.shape, 1) & (HEAD_DIM // 2)) != 0
        swapped = jnp.where(hi, pltpu.roll(x, HEAD_DIM // 2, axis=1),
                            pltpu.roll(x, width - HEAD_DIM // 2, axis=1))
        return x * cs + swapped * sn

    o = 0
    for half in range(2):
        x = seg(o + half * 256, 256)
        qa_ref[:, half * 256:(half + 1) * 256] = rope(head_norm(x, o + half * 256))
    o = A_W
    x = seg(o, 256)
    xn = rope(head_norm(x, o))
    ka_ref[...] = xn[:, :LANE]
    va_ref[...] = x[:, LANE:]
    kat_ref[...] = xn[:, :LANE].T
    vat_ref[...] = x[:, LANE:].T
    o = A_W + 2 * LANE
    for half in range(2):
        qi_ref[:, half * 256:(half + 1) * 256] = rope(seg(o + half * 256, 256))
    o = A_W + 2 * LANE + I_W
    for half in range(2):
        x = seg(o + half * 256, 256)
        qb_ref[:, half * 256:(half + 1) * 256] = rope(head_norm(x, o + half * 256))
    o = A_W + 2 * LANE + I_W + B_W
    x = seg(o, 256)
    xn = rope(head_norm(x, o))
    kb_ref[...] = xn[:, :LANE]
    vb_ref[...] = x[:, LANE:]
    kbt_ref[...] = xn[:, :LANE].T
    vbt_ref[...] = x[:, LANE:].T
    o = A_W + 2 * LANE + I_W + B_W + 2 * LANE
    for part in range(d_model // 256):
        x = seg(o + part * 256, 256)
        sga_ref[:, part * 256:(part + 1) * 256] = (1.0 / (1.0 + jnp.exp(-x))).astype(sga_ref.dtype)
    o += d_model
    for part in range(d_model // 256):
        x = seg(o + part * 256, 256)
        sgb_ref[:, part * 256:(part + 1) * 256] = (1.0 / (1.0 + jnp.exp(-x))).astype(sgb_ref.dtype)
    o += d_model
    x = seg(o, LANE)
    xn = rope(head_norm(x, o))
    kw = jnp.where(_iota(x.shape, 1) < IDX_DIM, xn, x)
    kw_ref[...] = kw
    kit_ref[...] = kw.T[:IDX_DIM]


def _proj(x2d, n1, w, gains, cos, sin, rows_per_seq):
    n, d = x2d.shape
    tm = min(ROW_BLOCK, rows_per_seq)
    nt = rows_per_seq // tm
    nseq = n // rows_per_seq
    nw = w.shape[1]
    row = lambda c: pl.BlockSpec((tm, c), lambda i: (i, 0))
    const = lambda shape: pl.BlockSpec(shape, lambda i: (0, 0))
    tab = pl.BlockSpec((tm, LANE), lambda i: (i % nt, 0))
    chan = lambda c: pl.BlockSpec((None, c, tm), lambda i: (i // nt, 0, i % nt))
    rows = [A_W, LANE, LANE, I_W, B_W, LANE, LANE, d, d, LANE]
    chans = [LANE, LANE, LANE, LANE, IDX_DIM]
    return pl.pallas_call(
        functools.partial(_proj_kernel, d_model=d),
        grid=(n // tm,),
        in_specs=[row(d), const((1, d)), const((d, nw)), const((1, nw)), tab, tab],
        out_specs=[row(c) for c in rows] + [chan(c) for c in chans],
        out_shape=[jax.ShapeDtypeStruct((n, c), BF16 if k in (7, 8) else F32) for k, c in enumerate(rows)]
        + [jax.ShapeDtypeStruct((nseq, c, rows_per_seq), F32) for c in chans],
        compiler_params=_params(1),
        name="proj",
    )(x2d, n1, w, gains, cos, sin)


def _head_operand(q_ref, h, want_low, scale):
    ch = q_ref[:, LANE * (h // 2):LANE * (h // 2) + LANE]
    if (h % 2 == 0) != want_low:
        ch = pltpu.roll(ch, HEAD_DIM, axis=1)
    low = _iota(ch.shape, 1) < HEAD_DIM
    keep = low if want_low else jnp.logical_not(low)
    return (jnp.where(keep, ch, 0.0) * scale).astype(BF16)


def _fill_kv(k_ref, vt_ref, kb_s, vt_s, nchunk, tc):
    kb_s[...] = k_ref[...].astype(BF16)
    extra = jnp.where(_iota((V_ROWS - HEAD_DIM, tc), 0) == 0, 1.0, 0.0)
    for c in range(nchunk):
        for kv in range(2):
            vt = vt_ref[kv * HEAD_DIM:(kv + 1) * HEAD_DIM, c * tc:(c + 1) * tc]
            vt_s[kv, c] = jnp.concatenate([vt, extra], axis=0).astype(BF16)


def _attend_chunk(kc, c, q_s, vt_s, st, n_heads, heads_per_kv, tile_bias_of=None, row_bias_of=None):
    s_s, p_s, mc_s, a_s, m_s, acc_s = st
    for h in range(n_heads):
        s = _dot_nt(kc, q_s[h])
        if tile_bias_of is not None:
            s = s + tile_bias_of(h)
        s_s[h] = s
        mc_s[h] = jnp.max(s, axis=0, keepdims=True)
    for h in range(n_heads):
        m_old = m_s[h]
        if row_bias_of is None:
            m_new = jnp.maximum(m_old, mc_s[h])
            shift = m_new
        else:
            rb = row_bias_of(h)
            m_new = jnp.maximum(m_old, mc_s[h] + rb)
            shift = m_new - rb
        p_s[h] = jnp.exp2(s_s[h] - shift).astype(BF16)
        a_s[h] = jnp.exp2(m_old - m_new)
        m_s[h] = m_new
    for h in range(n_heads):
        acc_s[h] = a_s[h] * acc_s[h] + _dot(vt_s[h // heads_per_kv, c], p_s[h])


V_ROWS = HEAD_DIM + 16
LOG2E = 1.4426950408889634


def _attention_scratch(n_heads, tq):
    row = pltpu.VMEM((n_heads, 1, tq), F32)
    return [pltpu.VMEM((n_heads, tq, tq), F32), pltpu.VMEM((n_heads, tq, tq), BF16), row, row, row,
            pltpu.VMEM((n_heads, V_ROWS, tq), F32)]


def _init_softmax(st):
    m_s, acc_s = st[-2], st[-1]
    m_s[...] = jnp.full(m_s.shape, NEG, F32)
    acc_s[...] = jnp.zeros(acc_s.shape, F32)


def _write_heads(o_ref, st, n_heads):
    acc_s = st[-1]
    for j in range(n_heads // 2):
        pair = []
        for h in (2 * j, 2 * j + 1):
            acc = acc_s[h]
            pair.append(acc[:HEAD_DIM] / acc[HEAD_DIM:HEAD_DIM + 1])
        o_ref[:, LANE * j:LANE * (j + 1)] = jnp.concatenate(pair, axis=0).T.astype(o_ref.dtype)


def _select_topk(sweep, count_gt, n_sel, small, key_index, tq, t_total):
    row = lambda v: jnp.full((1, tq), v, F32)
    rowmax = sweep(lambda s, c, a: jnp.maximum(a, jnp.max(s, axis=0, keepdims=True)), row(NEG))
    rowmin = sweep(lambda s, c, a: jnp.minimum(
        a, jnp.min(jnp.where(s > 0.5 * NEG, s, BIG), axis=0, keepdims=True)), row(BIG))

    def bisect(_, lh):
        lo, hi = lh
        mid = 0.5 * (lo + hi)
        ge = count_gt(mid) >= n_sel
        return jnp.where(ge, mid, lo), jnp.where(ge, hi, mid)

    lo, _ = lax.fori_loop(0, BISECT_STEPS, bisect, (rowmin, rowmax))
    u0 = sweep(lambda s, c, a: jnp.minimum(
        a, jnp.min(jnp.where(s >= lo, s, BIG), axis=0, keepdims=True)), row(BIG))

    def walk_cond(st):
        return st[2] == 0

    def walk(st):
        u = st[0]

        def f(s, c, carry):
            cnt, nxt = carry
            gt = s > u
            cnt = cnt + jnp.sum(jnp.where(gt, 1.0, 0.0), axis=0, keepdims=True)
            nxt = jnp.minimum(nxt, jnp.min(jnp.where(gt, s, BIG), axis=0, keepdims=True))
            return cnt, nxt

        cnt, nxt = sweep(f, (row(0.0), row(BIG)))
        done = jnp.logical_or(cnt < n_sel, small)
        all_done = jnp.min(jnp.where(done, 1.0, 0.0)).astype(jnp.int32)
        return jnp.where(done, u, nxt), cnt, all_done

    u, cgt, _ = lax.while_loop(walk_cond, walk, (u0, row(0.0), jnp.int32(0)))
    vstar = jnp.where(small, NEG, u)
    need = jnp.where(small, 0.0, n_sel - cgt)
    n_eq = sweep(lambda s, c, a: a + jnp.sum(jnp.where(s == vstar, 1.0, 0.0), axis=0, keepdims=True), row(0.0))
    excess = jnp.max(jnp.where(jnp.logical_and(n_eq > need, jnp.logical_not(small)), 1.0, 0.0))

    n_jsteps = jnp.where(excess > 0.0, t_total.bit_length() + 1, 0).astype(jnp.int32)

    def jstep(_, jj):
        jlo, jhi = jj
        mid = lax.shift_right_logical(jlo + jhi, 1)
        e = sweep(lambda s, c, a: a + jnp.sum(
            jnp.where(s == vstar, jnp.where(key_index(c) < mid, 1.0, 0.0), 0.0), axis=0, keepdims=True), row(0.0))
        ge = e >= need
        return jnp.where(ge, jlo, mid), jnp.where(ge, mid, jhi)

    zero_i = jnp.zeros((1, tq), jnp.int32)
    _, jcut = lax.fori_loop(0, n_jsteps, jstep, (zero_i, zero_i + t_total))
    return vstar, jnp.where(small, 0, jcut)


def _dsa_prompt_kernel(pt_ref, qi_ref, qa_ref, kwq_ref, kwk_ref, ka_ref, vat_ref,
                       sqi_ref, swi_ref, sqbd_ref, skin_ref, skan_ref, svan_ref, cik, cak, cav,
                       o_ref, os_ref,
                       kib_s, kab_s, vt_s, qim_s, qam_s, sc_s, s_s, p_s, mc_s, a_s, m_s, acc_s,
                       bik, bak, bav, sem, scs_s, *, n_sel, t_total, n_sel_s, rider):
    st = (s_s, p_s, mc_s, a_s, m_s, acc_s)
    tq = qi_ref.shape[0]
    tc = tq
    nchunk = t_total // tc
    i = pl.program_id(1)

    def sample_compute(k, slot):
        os_ref[k] = _dsa_sample_row(sqi_ref[k], swi_ref[k], sqbd_ref[k], skin_ref[k], skan_ref[k], svan_ref[k],
                                    bik, bak, bav, slot, scs_s, n_sel_s, rider["past"])

    def zero_pad_rows():
        bik[...] = jnp.zeros(bik.shape, F32)

    rps = rider["rows_per_step"]
    run_row = _sample_row_runner(pt_ref, pl.program_id(0) * pl.num_programs(1) + i, rps, rider["n_rows"],
                                 (cik, cak, cav), (bik, bak, bav), sem, rider["n_pages"], sample_compute,
                                 zero_pad_rows)
    for k in range((rps + 1) // 2):
        run_row(k)

    @pl.when(i == 0)
    def _():
        kib_s[...] = kwk_ref[...].astype(BF16)
        _fill_kv(ka_ref, vat_ref, kab_s, vt_s, nchunk, tc)

    heads_per_kv = A_HEADS // A_KV_HEADS
    for h in range(IDX_HEADS):
        qim_s[h] = _head_operand(qi_ref, h, True, 1.0)
    for h in range(A_HEADS):
        qam_s[h] = _head_operand(qa_ref, h, (h // heads_per_kv) == 0, HEAD_DIM ** -0.5 * LOG2E)
    w8 = kwq_ref[...].T[IDX_DIM:IDX_DIM + IDX_HEADS, :] * (IDX_DIM ** -0.5 * IDX_HEADS ** -0.5)

    krow = _iota((tc, tq), 0)
    qcol = _iota((tc, tq), 1)

    def chunk(c):
        return pl.ds(pl.multiple_of(c * tc, tc), tc)

    def score_chunk(c, diag):
        kc = kib_s[chunk(c), :]
        a = jnp.zeros((tc, tq), F32)
        for h in range(IDX_HEADS):
            a = a + jnp.maximum(_dot_nt(kc, qim_s[h]), 0.0) * w8[h:h + 1, :]
        if diag:
            a = jnp.where(krow <= qcol, a, NEG)
        sc_s[chunk(c), :] = a

    def score_body(c, carry):
        score_chunk(c, False)
        return carry

    lax.fori_loop(0, i, score_body, 0)
    score_chunk(i, True)

    def sweep(fn, init):
        return lax.fori_loop(0, i + 1, lambda c, carry: fn(sc_s[chunk(c), :], c, carry), init)

    def count_gt(x):
        return sweep(lambda s, c, a: a + jnp.sum(jnp.where(s > x, 1.0, 0.0), axis=0, keepdims=True),
                     jnp.zeros((1, tq), F32))

    n_adm = i * tq + _iota((1, tq), 1) + 1
    small = n_adm <= n_sel
    vstar, jcut = _select_topk(sweep, count_gt, float(n_sel), small,
                               lambda c: c * tc + krow, tq, t_total)

    def bias_body(c, carry):
        s = sc_s[chunk(c), :]
        tie = jnp.where((c * tc + krow) < jcut, 0.0, NEG)
        sc_s[chunk(c), :] = jnp.where(s > vstar, 0.0, jnp.where(s == vstar, tie, NEG))
        return carry

    lax.fori_loop(0, i + 1, bias_body, 0)

    for k in range((rps + 1) // 2, rps):
        run_row(k)

    _init_softmax(st)

    def att_body(c, carry):
        _attend_chunk(kab_s[chunk(c), :], c, qam_s, vt_s, st, A_HEADS, heads_per_kv,
                      tile_bias_of=lambda h: sc_s[chunk(c), :])
        return carry

    lax.fori_loop(0, i + 1, att_body, 0)
    _write_heads(o_ref, st, A_HEADS)


def _dsa_branch(qi, qa, kw, ka, vat, batch, t, page_table, sample_inputs, caches):
    tq = Q_BLOCK
    nq = t // tq
    n_sel = min(IDX_TOPK, t // 4)
    qblk = lambda c: pl.BlockSpec((tq, c), lambda b, i, pt: (b * nq + i, 0))
    full = pl.BlockSpec((t, LANE), lambda b, i, pt: (b, 0))
    full_t = pl.BlockSpec((None, LANE, t), lambda b, i, pt: (b, 0, 0))
    rd = _rider(page_table, sample_inputs, caches, batch, nq)
    past = rd["static"]["past"]
    grid_spec = pltpu.PrefetchScalarGridSpec(
        num_scalar_prefetch=1,
        grid=(batch, nq),
        in_specs=[qblk(I_W), qblk(A_W), qblk(LANE), full, full, full_t] + rd["in_specs"],
        out_specs=[qblk(A_W), rd["out_spec"]],
        scratch_shapes=[
            pltpu.VMEM((t, LANE), BF16), pltpu.VMEM((t, LANE), BF16),
            pltpu.VMEM((2, nq, V_ROWS, tq), BF16),
            pltpu.VMEM((IDX_HEADS, tq, LANE), BF16), pltpu.VMEM((A_HEADS, tq, LANE), BF16),
            pltpu.VMEM((t, tq), F32),
        ] + _attention_scratch(A_HEADS, tq) + rd["scratch"]
        + [pltpu.VMEM((N_KEY_CHUNKS, past // N_KEY_CHUNKS), F32)],
    )
    return pl.pallas_call(
        functools.partial(_dsa_prompt_kernel, n_sel=n_sel, t_total=t,
                          n_sel_s=float(min(IDX_TOPK, (past + 1) // 4)), rider=rd["static"]),
        grid_spec=grid_spec,
        out_shape=[jax.ShapeDtypeStruct((batch * t, A_W), BF16), rd["out_shape"]],
        compiler_params=_params(2),
        name="dsa",
    )(page_table, qi, qa, kw, kw, ka, vat, *sample_inputs, *caches)


def _moba_prompt_kernel(pt_ref, qb_ref, kb_ref, vbt_ref, sqbd_ref, skbn_ref, svbn_ref, cbk, cbv,
                        o_ref, os_ref,
                        kbb_s, vt_s, kbar_s, qbm_s, selb_s, s_s, p_s, mc_s, a_s, m_s, acc_s,
                        bbk, bbv, sem, *, n_blk, t_total, n_blk_s, rider):
    st = (s_s, p_s, mc_s, a_s, m_s, acc_s)
    tq = qb_ref.shape[0]
    tc = tq
    nb = t_total // tc
    nbp = kbar_s.shape[0]
    i = pl.program_id(1)
    heads_per_kv = B_HEADS // B_KV_HEADS

    def sample_compute(k, slot):
        os_ref[k] = _moba_sample_row(sqbd_ref[k], skbn_ref[k], svbn_ref[k], bbk, bbv, slot, n_blk_s, rider["past"])

    rps = rider["rows_per_step"]
    run_row = _sample_row_runner(pt_ref, pl.program_id(0) * pl.num_programs(1) + i, rps, rider["n_rows"],
                                 (cbk, cbv), (bbk, bbv), sem, rider["n_pages"], sample_compute)
    for k in range((rps + 1) // 2):
        run_row(k)

    @pl.when(i == 0)
    def _():
        _fill_kv(kb_ref, vbt_ref, kbb_s, vt_s, nb, tc)
        kbar_s[...] = jnp.zeros(kbar_s.shape, F32)
        for n in range(nb):
            kbar_s[n:n + 1, :] = jnp.mean(kb_ref[n * tc:(n + 1) * tc, :], axis=0, keepdims=True)

    kbar = kbar_s[...].astype(BF16)
    blk = _iota((nbp, tq), 0)
    past = blk < i
    for h in range(B_HEADS):
        qm = _head_operand(qb_ref, h, (h // heads_per_kv) == 0, HEAD_DIM ** -0.5 * LOG2E)
        qbm_s[h] = qm
        gate = jnp.where(past, _dot_nt(kbar, qm), NEG)
        rank = jnp.zeros((nbp, tq), F32)
        for m in range(nb):
            gm = gate[m:m + 1, :]
            first = jnp.where(blk > m, 1.0, 0.0)
            rank = rank + jnp.where(gm > gate, 1.0, jnp.where(gm == gate, first, 0.0))
        sel = jnp.logical_and(past, rank < n_blk)
        selb_s[h] = jnp.where(sel, 0.0, NEG)

    _init_softmax(st)
    krow = _iota((tc, tq), 0)
    qcol = _iota((tc, tq), 1)

    def chunk(c):
        return pl.ds(pl.multiple_of(c * tc, tc), tc)

    _attend_chunk(kbb_s[chunk(i), :], i, qbm_s, vt_s, st, B_HEADS, heads_per_kv,
                  tile_bias_of=lambda h: jnp.where(krow <= qcol, 0.0, NEG))

    for k in range((rps + 1) // 2, rps):
        run_row(k)

    def att_body(c, carry):
        _attend_chunk(kbb_s[chunk(c), :], c, qbm_s, vt_s, st, B_HEADS, heads_per_kv,
                      row_bias_of=lambda h: selb_s[h, pl.ds(c, 1), :])
        return carry

    lax.fori_loop(0, i, att_body, 0)
    _write_heads(o_ref, st, B_HEADS)


def _moba_branch(qb, kb, vbt, batch, t, page_table, sample_inputs, caches):
    tq = Q_BLOCK
    assert tq == MOBA_BLOCK and t % tq == 0
    nq = t // tq
    n_blk = min(MOBA_TOPK, (t - 1) // MOBA_BLOCK)
    qblk = pl.BlockSpec((tq, B_W), lambda b, i, pt: (b * nq + i, 0))
    full = pl.BlockSpec((t, LANE), lambda b, i, pt: (b, 0))
    full_t = pl.BlockSpec((None, LANE, t), lambda b, i, pt: (b, 0, 0))
    rd = _rider(page_table, sample_inputs, caches, batch, nq)
    past = rd["static"]["past"]
    grid_spec = pltpu.PrefetchScalarGridSpec(
        num_scalar_prefetch=1,
        grid=(batch, nq),
        in_specs=[qblk, full, full_t] + rd["in_specs"],
        out_specs=[qblk, rd["out_spec"]],
        scratch_shapes=[
            pltpu.VMEM((t, LANE), BF16),
            pltpu.VMEM((2, nq, V_ROWS, tq), BF16),
            pltpu.VMEM((max(nq, 8), LANE), F32),
            pltpu.VMEM((B_HEADS, tq, LANE), BF16),
            pltpu.VMEM((B_HEADS, max(nq, 8), tq), F32),
        ] + _attention_scratch(B_HEADS, tq) + rd["scratch"],
    )
    return pl.pallas_call(
        functools.partial(_moba_prompt_kernel, n_blk=float(n_blk), t_total=t,
                          n_blk_s=float(min(MOBA_TOPK, past // MOBA_BLOCK)), rider=rd["static"]),
        grid_spec=grid_spec,
        out_shape=[jax.ShapeDtypeStruct((batch * t, B_W), BF16), rd["out_shape"]],
        compiler_params=_params(2),
        name="moba",
    )(page_table, qb, kb, vbt, *sample_inputs, *caches)


def _page_copy(cache, buf, sem, k, slot, page, p):
    rows, width = cache.shape[1], cache.shape[2]
    return pltpu.make_async_copy(cache.at[page], buf.at[slot, pl.ds(0, rows), pl.ds(p * width, width)],
                                 sem.at[k, slot])


def _gather(pt_ref, b, caches, bufs, sem, slot, n_pages, start):
    for k, (cache, buf) in enumerate(zip(caches, bufs)):
        for p in range(n_pages):
            cp = _page_copy(cache, buf, sem, k, slot, pt_ref[b, p] if start else 0, p)
            if start:
                cp.start()
            else:
                cp.wait()


def _sample_row_runner(pt_ref, step, rows_per_step, n_rows, caches, bufs, sem, n_pages, compute, before_first=None):
    def run(k):
        r = step * rows_per_step + k
        slot = lax.rem(r, 2)

        @pl.when(r == 0)
        def _():
            if before_first is not None:
                before_first()
            _gather(pt_ref, r, caches, bufs, sem, slot, n_pages, True)

        _gather(pt_ref, r, caches, bufs, sem, slot, n_pages, False)
        compute(k, slot)
        _gather(pt_ref, jnp.minimum(r + 1, n_rows - 1), caches, bufs, sem, 1 - slot, n_pages, True)

        @pl.when(r == n_rows - 1)
        def _():
            _gather(pt_ref, r, caches, bufs, sem, 1 - slot, n_pages, False)

    return run


def _sample_softmax(tiles, s_new, vt_tiles, v_new):
    m = s_new
    for s in tiles:
        m = jnp.maximum(m, jnp.max(s, axis=1, keepdims=True))
    p_new = jnp.exp(s_new - m)
    l = p_new
    o = p_new * v_new
    for s, vt in zip(tiles, vt_tiles):
        p = jnp.exp(s - m)
        l = l + jnp.sum(p, axis=1, keepdims=True)
        o = o + _dot_nt(p.astype(BF16), vt)
    return o / l


N_KEY_CHUNKS = 8


def _dsa_sample_row(qi, wi, qbd, kin, kan, van, bik, bak, bav, slot, sc_s, n_sel, past):
    nq = N_KEY_CHUNKS
    ch = past // nq
    qib = qi.astype(BF16)
    wrow = wi * (IDX_DIM ** -0.5 * IDX_HEADS ** -0.5)
    eye = _iota((IDX_HEADS, IDX_HEADS), 0) == _iota((IDX_HEADS, IDX_HEADS), 1)
    wcol = jnp.sum(jnp.where(eye, jnp.broadcast_to(wrow, (IDX_HEADS, IDX_HEADS)), 0.0), axis=1, keepdims=True)
    lg_new = jnp.sum(qi * kin, axis=1, keepdims=True)
    s_new = jnp.sum(jnp.maximum(lg_new, 0.0) * wcol, axis=0, keepdims=True)

    for q in range(nq):
        kq = bik[slot, :, q * ch:(q + 1) * ch].astype(BF16)
        lg = jnp.maximum(_dot(qib, kq), 0.0)
        sc_s[q:q + 1, :] = jnp.sum(lg * wcol, axis=0, keepdims=True)

    sc = sc_s[...]
    kidx = _iota((nq, ch), 0) * ch + _iota((nq, ch), 1)

    def total(x):
        return jnp.sum(jnp.sum(x, axis=1, keepdims=True), axis=0, keepdims=True)

    def tmin(x):
        return jnp.min(jnp.min(x, axis=1, keepdims=True), axis=0, keepdims=True)

    def count_gt(x):
        return total(jnp.where(sc > x, 1.0, 0.0)) + jnp.where(s_new > x, 1.0, 0.0)

    hi0 = jnp.maximum(-tmin(-sc), s_new)
    lo0 = jnp.minimum(tmin(sc), s_new)

    lo, hi = lo0, hi0
    for _ in range(SAMPLE_SPLIT_ROUNDS):
        step = (hi - lo) * (1.0 / SAMPLE_SPLIT)
        new_lo, new_hi = lo, hi
        for j in range(1, SAMPLE_SPLIT):
            tj = lo + step * float(j)
            ge = count_gt(tj) >= n_sel
            new_lo = jnp.where(ge, jnp.maximum(new_lo, tj), new_lo)
            new_hi = jnp.where(ge, new_hi, jnp.minimum(new_hi, tj))
        lo, hi = new_lo, new_hi
    u0 = jnp.minimum(tmin(jnp.where(sc >= lo, sc, BIG)), jnp.where(s_new >= lo, s_new, BIG))

    def walk_cond(st):
        return st[2] == 0

    def walk(st):
        u = st[0]
        cnt = count_gt(u)
        nxt = jnp.minimum(tmin(jnp.where(sc > u, sc, BIG)), jnp.where(s_new > u, s_new, BIG))
        done = cnt < n_sel
        return jnp.where(done, u, nxt), cnt, jnp.min(jnp.where(done, 1.0, 0.0)).astype(jnp.int32)

    vstar, cgt, _ = lax.while_loop(walk_cond, walk, (u0, jnp.zeros((1, 1), F32), jnp.int32(0)))
    need = n_sel - cgt
    n_eq = total(jnp.where(sc == vstar, 1.0, 0.0)) + jnp.where(s_new == vstar, 1.0, 0.0)
    n_jsteps = jnp.where(jnp.max(n_eq - need) > 0.0, (past + 1).bit_length() + 1, 0).astype(jnp.int32)

    def jstep(_, jj):
        jlo, jhi = jj
        mid = lax.shift_right_logical(jlo + jhi, 1)
        e = total(jnp.where(sc == vstar, jnp.where(kidx < mid, 1.0, 0.0), 0.0)) + jnp.where(
            s_new == vstar, jnp.where(past < mid, 1.0, 0.0), 0.0)
        ge = e >= need
        return jnp.where(ge, jlo, mid), jnp.where(ge, mid, jhi)

    zero_i = jnp.zeros((1, 1), jnp.int32)
    _, jcut = lax.fori_loop(0, n_jsteps, jstep, (zero_i, zero_i + (past + 1)))
    bias = jnp.where(sc > vstar, 0.0, jnp.where(sc == vstar, jnp.where(kidx < jcut, 0.0, NEG), NEG))
    bias_new = jnp.where(s_new > vstar, 0.0, jnp.where(s_new == vstar, jnp.where(past < jcut, 0.0, NEG), NEG))

    qs = qbd * (HEAD_DIM ** -0.5)
    sn = jnp.sum(qs * kan, axis=1, keepdims=True) + bias_new
    qsb = qs.astype(BF16)
    tiles, vts = [], []
    for q in range(nq):
        kq = bak[slot, :, q * ch:(q + 1) * ch].astype(BF16)
        tiles.append(_dot(qsb, kq) + bias[q:q + 1, :])
        vts.append(bav[slot, :, q * ch:(q + 1) * ch].astype(BF16))
    return _sample_softmax(tiles, sn, vts, van)


def _moba_sample_row(qbd, kbn, vbn, bbk, bbv, slot, n_blk, past):
    nq = N_KEY_CHUNKS
    ch = past // nq
    nblk = past // MOBA_BLOCK
    bpc = ch // MOBA_BLOCK
    qs = qbd * (HEAD_DIM ** -0.5)
    qsb = qs.astype(BF16)
    sn = jnp.sum(qs * kbn, axis=1, keepdims=True)
    bcol = _iota((B_HEADS, nblk), 1).astype(F32)

    raw, vts = [], []
    gate = jnp.zeros((B_HEADS, nblk), F32)
    for q in range(nq):
        s = _dot(qsb, bbk[slot, :, q * ch:(q + 1) * ch].astype(BF16))
        raw.append(s)
        vts.append(bbv[slot, :, q * ch:(q + 1) * ch].astype(BF16))
        for k in range(bpc):
            g = jnp.sum(s[:, k * MOBA_BLOCK:(k + 1) * MOBA_BLOCK], axis=1, keepdims=True) * (1.0 / MOBA_BLOCK)
            gate = jnp.where(bcol == float(q * bpc + k), g, gate)
    selm = jnp.zeros((B_HEADS, nblk), F32)
    for _ in range(int(n_blk)):
        mx = jnp.max(gate, axis=1, keepdims=True)
        first = jnp.min(jnp.where(gate == mx, bcol, float(nblk)), axis=1, keepdims=True)
        hit = bcol == first
        selm = jnp.where(hit, 1.0, selm)
        gate = jnp.where(hit, 2.0 * NEG, gate)
    tiles = []
    for q in range(nq):
        bias = jnp.concatenate(
            [jnp.broadcast_to(jnp.where(selm[:, q * bpc + k:q * bpc + k + 1] > 0.0, 0.0, NEG),
                              (B_HEADS, MOBA_BLOCK)) for k in range(bpc)], axis=1)
        tiles.append(raw[q] + bias)
    return _sample_softmax(tiles, sn, vts, vbn)


def _rider(page_table, small_inputs, caches, batch, nq):
    db, n_pages = page_table.shape
    page = caches[0].shape[2]
    past = n_pages * page
    n_steps = batch * nq
    assert past % (N_KEY_CHUNKS * MOBA_BLOCK) == 0 and page % LANE == 0 and db % n_steps == 0
    rps = db // n_steps
    per_step = lambda a: pl.BlockSpec((rps,) + a.shape[1:], lambda b, i, pt: (b * nq + i, 0, 0))
    return dict(
        in_specs=[per_step(a) for a in small_inputs] + [pl.BlockSpec(memory_space=pl.ANY)] * len(caches),
        out_spec=pl.BlockSpec((rps, 8, LANE), lambda b, i, pt: (b * nq + i, 0, 0)),
        out_shape=jax.ShapeDtypeStruct((db, 8, LANE), F32),
        scratch=[pltpu.VMEM((2, LANE, past), F32) for _ in caches] + [pltpu.SemaphoreType.DMA((len(caches), 2))],
        static=dict(n_rows=db, rows_per_step=rps, n_pages=n_pages, past=past),
    )


def _merge_kernel(x_ref, oa_ref, ob_ref, sga_ref, sgb_ref, wpa_ref, wpb_ref, wo_ref, y_ref):
    pa = _dot(oa_ref[...].astype(BF16), wpa_ref[...])
    pb = _dot(ob_ref[...].astype(BF16), wpb_ref[...])
    merged = sga_ref[...] * pa + sgb_ref[...] * pb
    y_ref[...] = x_ref[...] + _dot(merged.astype(BF16), wo_ref[...])


def _merge(x2d, oa, ob, sga, sgb, wpa, wpb, wo):
    n, d = x2d.shape
    tm = min(ROW_BLOCK, n)
    row = lambda c: pl.BlockSpec((tm, c), lambda i: (i, 0))
    const = lambda a: pl.BlockSpec(a.shape, lambda i: (0, 0))
    return pl.pallas_call(
        _merge_kernel,
        grid=(n // tm,),
        in_specs=[row(d), row(A_W), row(B_W), row(d), row(d), const(wpa), const(wpb), const(wo)],
        out_specs=row(d),
        out_shape=jax.ShapeDtypeStruct((n, d), F32),
        compiler_params=_params(1),
        name="merge",
    )(x2d, oa, ob, sga, sgb, wpa, wpb, wo)


FFN_COL_CHUNK = 1408


def _ffn_prompt_kernel(x_ref, oa_ref, ob_ref, sga_ref, sgb_ref, wpa_ref, wpb_ref, wo_ref,
                       n2_ref, wup_ref, cw_ref, cb_ref, wdn_ref, prev_ref, y_ref, tail_ref, ext_s, *, d_ff):
    tm = x_ref.shape[0]
    i = pl.program_id(1)

    @pl.when(i == 0)
    def _():
        ext_s[6:8, :] = prev_ref[...]

    pa = _dot(oa_ref[...].astype(BF16), wpa_ref[...])
    pb = _dot(ob_ref[...].astype(BF16), wpb_ref[...])
    merged = sga_ref[...] * pa + sgb_ref[...] * pb
    x = x_ref[...] + _dot(merged.astype(BF16), wo_ref[...])
    xn = _rms(x, n2_ref[...]).astype(BF16)
    fc = FFN_COL_CHUNK
    for c in range(2 * d_ff // fc):
        ext_s[8:8 + tm, c * fc:(c + 1) * fc] = _dot(xn, wup_ref[:, c * fc:(c + 1) * fc])

    def conv(lo):
        cols = slice(lo, lo + fc)
        out = cb_ref[:, cols] + ext_s[6:6 + tm, cols] * cw_ref[0:1, cols]
        out = out + ext_s[7:7 + tm, cols] * cw_ref[1:2, cols]
        return out + ext_s[8:8 + tm, cols] * cw_ref[2:3, cols]

    y = x
    for j in range(d_ff // fc):
        a = conv(j * fc)
        g = conv(d_ff + j * fc)
        act = (a / (1.0 + jnp.exp(-a)) * g).astype(BF16)
        y = y + _dot(act, wdn_ref[j * fc:(j + 1) * fc, :])
    y_ref[...] = y
    tail = ext_s[tm + 6:tm + 8, :]
    tail_ref[...] = tail
    ext_s[6:8, :] = tail


def _merge_ffn_prompt(x2d, oa, ob, sga, sgb, wpa, wpb, wo, n2, wup, cw, cb, wdn, prev, batch, t):
    n, d = x2d.shape
    d_ff = wdn.shape[0]
    assert d_ff % FFN_COL_CHUNK == 0
    tm = ROW_BLOCK
    nt = t // tm
    row = lambda c: pl.BlockSpec((tm, c), lambda b, i: (b * nt + i, 0))
    const = lambda a: pl.BlockSpec(a.shape, lambda b, i: (0, 0), pipeline_mode=pl.Buffered(1))
    per_b = pl.BlockSpec((None, 2, 2 * d_ff), lambda b, i: (b, 0, 0))
    return pl.pallas_call(
        functools.partial(_ffn_prompt_kernel, d_ff=d_ff),
        grid=(batch, nt),
        in_specs=[row(d), row(A_W), row(B_W), row(d), row(d), const(wpa), const(wpb), const(wo),
                  const(n2), const(wup), const(cw), const(cb), const(wdn), per_b],
        out_specs=[row(d), per_b],
        out_shape=[jax.ShapeDtypeStruct((n, d), F32), jax.ShapeDtypeStruct((batch, 2, 2 * d_ff), F32)],
        scratch_shapes=[pltpu.VMEM((tm + 8, 2 * d_ff), F32)],
        compiler_params=_params(2),
        name="merge_ffn_prompt",
    )(x2d, oa, ob, sga, sgb, wpa, wpb, wo, n2, wup, cw, cb, wdn, prev)


def _ffn_sample_kernel(x_ref, n2_ref, wup_ref, cw_ref, cb_ref, wdn_ref, s0_ref, s1_ref, y_ref, up_ref, *, d_ff):
    x = x_ref[...]
    xn = _rms(x, n2_ref[...]).astype(BF16)
    fc = FFN_COL_CHUNK
    for c in range(2 * d_ff // fc):
        up_ref[:, c * fc:(c + 1) * fc] = _dot(xn, wup_ref[:, c * fc:(c + 1) * fc])

    def conv(lo):
        cols = slice(lo, lo + fc)
        out = cb_ref[:, cols] + s0_ref[:, cols] * cw_ref[0:1, cols]
        out = out + s1_ref[:, cols] * cw_ref[1:2, cols]
        return out + up_ref[:, cols] * cw_ref[2:3, cols]

    y = x
    for j in range(d_ff // fc):
        a = conv(j * fc)
        g = conv(d_ff + j * fc)
        act = (a / (1.0 + jnp.exp(-a)) * g).astype(BF16)
        y = y + _dot(act, wdn_ref[j * fc:(j + 1) * fc, :])
    y_ref[...] = y


def _ffn_sample(x2d, n2, wup, cw, cb, wdn, s0, s1):
    n, d = x2d.shape
    d_ff = wdn.shape[0]
    full = lambda a: pl.BlockSpec(a.shape, lambda i: (0, 0), pipeline_mode=pl.Buffered(1))
    args = (x2d, n2, wup, cw, cb, wdn, s0, s1)
    return pl.pallas_call(
        functools.partial(_ffn_sample_kernel, d_ff=d_ff),
        grid=(1,),
        in_specs=[full(a) for a in args],
        out_specs=[pl.BlockSpec((n, d), lambda i: (0, 0)), pl.BlockSpec((n, 2 * d_ff), lambda i: (0, 0))],
        out_shape=[jax.ShapeDtypeStruct((n, d), F32), jax.ShapeDtypeStruct((n, 2 * d_ff), F32)],
        compiler_params=_params(1),
        name="ffn_sample",
    )(*args)


def _rope_tables(pos):
    half = HEAD_DIM // 2
    inv = ROPE_THETA ** (-jnp.arange(half, dtype=F32) / half)
    ang = pos.astype(F32)[:, None] * inv[None, :]
    cos = jnp.cos(ang)
    sin = jnp.sin(ang)
    return jnp.tile(cos, (1, 4)), jnp.tile(jnp.concatenate([-sin, sin], axis=1), (1, 2))


def _layout_w_in(w_in, d_model):
    splits = (A_W, A_KV_W, A_KV_W, I_W, IDX_DIM, IDX_HEADS, B_W, B_KV_W, B_KV_W, d_model, d_model)
    offs = [0]
    for s in splits:
        offs.append(offs[-1] + s)
    p = [w_in[:, offs[k]:offs[k + 1]] for k in range(len(splits))]
    qa, ka, va, qi, ki, wi, qb, kb, vb, ga, gb = p
    pad = jnp.zeros((w_in.shape[0], LANE - IDX_DIM - IDX_HEADS), w_in.dtype)
    return jnp.concatenate([qa, ka, va, qi, qb, kb, vb, ga, gb, ki, wi, pad], axis=1).astype(BF16)


def _layout_gains(q_norm_a, k_norm_a, k_norm_idx, q_norm_b, k_norm_b, d_model):
    one = lambda n: jnp.ones((n,), F32)
    return jnp.concatenate([
        jnp.tile(q_norm_a, A_HEADS), jnp.tile(k_norm_a, A_KV_HEADS), one(A_KV_W), one(I_W),
        jnp.tile(q_norm_b, B_HEADS), jnp.tile(k_norm_b, B_KV_HEADS), one(B_KV_W),
        one(2 * d_model), k_norm_idx, one(LANE - IDX_DIM)])[None, :]


def _block_diag_q(q, heads_per_kv):
    z = jnp.zeros_like(q)
    low = jnp.concatenate([q, z], axis=-1)
    high = jnp.concatenate([z, q], axis=-1)
    is_low = (jnp.arange(q.shape[1]) // heads_per_kv == 0)[None, :, None]
    return jnp.where(is_low, low, high)


def _pick_kv(o, heads_per_kv):
    n = o.shape[0]
    return jnp.concatenate([o[:, :heads_per_kv, :HEAD_DIM].reshape(n, -1),
                            o[:, heads_per_kv:, HEAD_DIM:].reshape(n, -1)], axis=1)


def _sample_operands(qa_s, ka_s, va_s, qi_s, qb_s, kb_s, vb_s, kw_s, c_ik, c_ak, c_av, c_bk, c_bv):
    db = qa_s.shape[0]
    n_pool, page = c_ak.shape[0], c_ak.shape[1]
    hpk_a = A_HEADS // A_KV_HEADS
    hpk_b = B_HEADS // B_KV_HEADS
    qi_pad = jnp.pad(qi_s.reshape(db, IDX_HEADS, IDX_DIM), ((0, 0), (0, 0), (0, LANE - IDX_DIM)))
    wi_s = kw_s[:, IDX_DIM:IDX_DIM + IDX_HEADS].reshape(db, 1, IDX_HEADS)
    qa_bd = _block_diag_q(qa_s.reshape(db, A_HEADS, HEAD_DIM), hpk_a)
    qb_bd = _block_diag_q(qb_s.reshape(db, B_HEADS, HEAD_DIM), hpk_b)
    kin = jnp.where(jnp.arange(LANE) < IDX_DIM, kw_s, 0.0).reshape(db, 1, LANE)
    pages_t = lambda c: jnp.moveaxis(c, 1, -1).reshape(n_pool, -1, page)
    dsa = ([qi_pad, wi_s, qa_bd, kin, ka_s.reshape(db, 1, LANE), va_s.reshape(db, 1, LANE)],
           [pages_t(c_ik), pages_t(c_ak), pages_t(c_av)])
    moba = ([qb_bd, kb_s.reshape(db, 1, LANE), vb_s.reshape(db, 1, LANE)], [pages_t(c_bk), pages_t(c_bv)])
    return dsa, moba


def kernel(x_prompt, x_sample, cache_a_k, cache_a_v, cache_idx_k, cache_b_k, cache_b_v, state_conv, page_table, norm1, w_in, q_norm_a, k_norm_a, k_norm_idx, q_norm_b, k_norm_b, w_proj_a, w_proj_b, w_out, norm2, w_up, conv_w, conv_b, w_down):
    batch, t, d = x_prompt.shape
    db, ds, _ = x_sample.shape
    depth = norm1.shape[0]
    assert depth == 1 and ds == 1
    n_pool, page = cache_a_k.shape[1], cache_a_k.shape[2]
    past = page_table.shape[1] * page
    l = 0
    w = _layout_w_in(w_in[l], d)
    gains = _layout_gains(q_norm_a[l], k_norm_a[l], k_norm_idx[l], q_norm_b[l], k_norm_b[l], d)
    wpa, wpb, wo = w_proj_a[l].astype(BF16), w_proj_b[l].astype(BF16), w_out[l].astype(BF16)
    wup, wdn = w_up[l].astype(BF16), w_down[l].astype(BF16)
    n1, n2 = norm1[l][None, :], norm2[l][None, :]
    cw, cb = conv_w[l], conv_b[l][None, :]
    d_ff = wdn.shape[0]

    xp = x_prompt.reshape(batch * t, d)
    cos_p, sin_p = _rope_tables(jnp.arange(t, dtype=jnp.int32))
    qa, ka, _, qi, qb, kb, _, sga, sgb, kw, kat, vat, kbt, vbt, kit = _proj(xp, n1, w, gains, cos_p, sin_p, t)
    xs = x_sample.reshape(db, d)
    cos_s, sin_s = _rope_tables(jnp.full((db,), past, jnp.int32))
    (qa_s, ka_s, va_s, qi_s, qb_s, kb_s, vb_s, sga_s, sgb_s, kw_s,
     kat_s, vat_s, kbt_s, vbt_s, kit_s) = _proj(xs, n1, w, gains, cos_s, sin_s, db)

    dsa_s, moba_s = _sample_operands(qa_s, ka_s, va_s, qi_s, qb_s, kb_s, vb_s, kw_s,
                                     cache_idx_k[l], cache_a_k[l], cache_a_v[l], cache_b_k[l], cache_b_v[l])
    oa, oa_s = _dsa_branch(qi, qa, kw, ka, vat, batch, t, page_table, *dsa_s)
    ob, ob_s = _moba_branch(qb, kb, vbt, batch, t, page_table, *moba_s)
    oa_s = _pick_kv(oa_s, A_HEADS // A_KV_HEADS)
    ob_s = _pick_kv(ob_s, B_HEADS // B_KV_HEADS)

    yp, p_conv = _merge_ffn_prompt(xp, oa, ob, sga, sgb, wpa, wpb, wo, n2, wup, cw, cb, wdn,
                                   jnp.zeros((batch, 2, 2 * d_ff), F32), batch, t)
    x1_s = _merge(xs, oa_s, ob_s, sga_s, sgb_s, wpa, wpb, wo)
    ys, up_s = _ffn_sample(x1_s, n2, wup, cw, cb, wdn, state_conv[l, :, 0], state_conv[l, :, 1])

    def rows5(a, n, s, h):
        a = a.reshape(a.shape[0], h, HEAD_DIM, a.shape[2])
        return jnp.transpose(a, (0, 3, 1, 2)).reshape(1, n, s, h, HEAD_DIM)

    def rows4(a, n, s):
        return jnp.transpose(a, (0, 2, 1)).reshape(1, n, s, IDX_DIM)

    return (
        yp.reshape(batch, t, d), ys.reshape(db, 1, d),
        rows5(kat, batch, t, A_KV_HEADS), rows5(vat, batch, t, A_KV_HEADS), rows4(kit, batch, t),
        rows5(kbt, batch, t, B_KV_HEADS), rows5(vbt, batch, t, B_KV_HEADS),
        p_conv[None],
        rows5(kat_s, db, 1, A_KV_HEADS), rows5(vat_s, db, 1, A_KV_HEADS), rows4(kit_s, db, 1),
        rows5(kbt_s, db, 1, B_KV_HEADS), rows5(vbt_s, db, 1, B_KV_HEADS),
        jnp.stack([state_conv[l, :, 1], up_s], axis=1)[None],
    )
```

```python
import functools

import jax
import jax.numpy as jnp
from jax import lax
from jax.experimental import pallas as pl
from jax.experimental.pallas import tpu as pltpu

HEAD_DIM = 64
A_HEADS = 8
A_KV_HEADS = 2
IDX_HEADS = 8
IDX_DIM = 64
IDX_TOPK = 256
B_HEADS = 8
B_KV_HEADS = 2
MOBA_BLOCK = 256
MOBA_TOPK = 3
ROPE_THETA = 10000.0
EPS = 1e-6
NEG = -1e30
BIG = 3e38

LANE = 128
Q_BLOCK = 256
ROW_BLOCK = 256
VMEM_LIMIT = 56 * 1024 * 1024
BISECT_STEPS = 13
SAMPLE_SPLIT = 16
SAMPLE_SPLIT_ROUNDS = 4

F32 = jnp.float32
BF16 = jnp.bfloat16

A_W = A_HEADS * HEAD_DIM
A_KV_W = A_KV_HEADS * HEAD_DIM
I_W = IDX_HEADS * IDX_DIM
B_W = B_HEADS * HEAD_DIM
B_KV_W = B_KV_HEADS * HEAD_DIM
assert A_W == 512 and I_W == 512 and B_W == 512 and A_KV_W == LANE and B_KV_W == LANE


def _params(n_grid):
    return pltpu.CompilerParams(dimension_semantics=("arbitrary",) * n_grid,
                                vmem_limit_bytes=VMEM_LIMIT)


def _dot(a, b):
    return jnp.dot(a, b, preferred_element_type=F32)


def _dot_nt(a, b):
    return lax.dot_general(a, b, (((1,), (1,)), ((), ())), preferred_element_type=F32)


def _iota(shape, dim):
    return lax.broadcasted_iota(jnp.int32, shape, dim)


def _rms(x, g):
    return x * lax.rsqrt(jnp.mean(x * x, axis=-1, keepdims=True) + EPS) * g


def _proj_kernel(x_ref, n1_ref, w_ref, g_ref, cos_ref, sin_ref,
                 qa_ref, ka_ref, va_ref, qi_ref, qb_ref, kb_ref, vb_ref, sga_ref, sgb_ref, kw_ref,
                 kat_ref, vat_ref, kbt_ref, vbt_ref, kit_ref, *, d_model):
    h = _rms(x_ref[...], n1_ref[...]).astype(BF16)
    gr = lax.shift_right_logical(_iota((2 * LANE, 2 * LANE), 0), 6)
    gc = lax.shift_right_logical(_iota((2 * LANE, 2 * LANE), 1), 6)
    gsum = jnp.where(gr == gc, 1.0, 0.0).astype(BF16)
    cos1 = cos_ref[...]
    sin1 = sin_ref[...]

    def seg(off, width):
        return _dot(h, w_ref[:, off:off + width])

    def head_norm(x, off):
        width = x.shape[1]
        ss = _dot((x * x).astype(BF16), gsum[:width, :width])
        return x * lax.rsqrt(ss * (1.0 / HEAD_DIM) + EPS) * g_ref[:, off:off + width]

    def rope(x):
        width = x.shape[1]
        rep = width // LANE
        cs = jnp.concatenate([cos1] * rep, axis=1) if rep > 1 else cos1
        sn = jnp.concatenate([sin1] * rep, axis=1) if rep > 1 else sin1
        hi = (_iota(x.shape, 1) & (HEAD_DIM // 2)) != 0
        swapped = jnp.where(hi, pltpu.roll(x, HEAD_DIM // 2, axis=1),
                            pltpu.roll(x, width - HEAD_DIM // 2, axis=1))
        return x * cs + swapped * sn

    o = 0
    for half in range(2):
        x = seg(o + half * 256, 256)
        qa_ref[:, half * 256:(half + 1) * 256] = rope(head_norm(x, o + half * 256))
    o = A_W
    x = seg(o, 256)
    xn = rope(head_norm(x, o))
    ka_ref[...] = xn[:, :LANE]
    va_ref[...] = x[:, LANE:]
    kat_ref[...] = xn[:, :LANE].T
    vat_ref[...] = x[:, LANE:].T
    o = A_W + 2 * LANE
    for half in range(2):
        qi_ref[:, half * 256:(half + 1) * 256] = rope(seg(o + half * 256, 256))
    o = A_W + 2 * LANE + I_W
    for half in range(2):
        x = seg(o + half * 256, 256)
        qb_ref[:, half * 256:(half + 1) * 256] = rope(head_norm(x, o + half * 256))
    o = A_W + 2 * LANE + I_W + B_W
    x = seg(o, 256)
    xn = rope(head_norm(x, o))
    kb_ref[...] = xn[:, :LANE]
    vb_ref[...] = x[:, LANE:]
    kbt_ref[...] = xn[:, :LANE].T
    vbt_ref[...] = x[:, LANE:].T
    o = A_W + 2 * LANE + I_W + B_W + 2 * LANE
    for part in range(d_model // 256):
        x = seg(o + part * 256, 256)
        sga_ref[:, part * 256:(part + 1) * 256] = (1.0 / (1.0 + jnp.exp(-x))).astype(sga_ref.dtype)
    o += d_model
    for part in range(d_model // 256):
        x = seg(o + part * 256, 256)
        sgb_ref[:, part * 256:(part + 1) * 256] = (1.0 / (1.0 + jnp.exp(-x))).astype(sgb_ref.dtype)
    o += d_model
    x = seg(o, LANE)
    xn = rope(head_norm(x, o))
    kw = jnp.where(_iota(x.shape, 1) < IDX_DIM, xn, x)
    kw_ref[...] = kw
    kit_ref[...] = kw.T[:IDX_DIM]


def _proj(x2d, n1, w, gains, cos, sin, rows_per_seq):
    n, d = x2d.shape
    tm = min(ROW_BLOCK, rows_per_seq)
    nt = rows_per_seq // tm
    nseq = n // rows_per_seq
    nw = w.shape[1]
    row = lambda c: pl.BlockSpec((tm, c), lambda i: (i, 0))
    const = lambda shape: pl.BlockSpec(shape, lambda i: (0, 0))
    tab = pl.BlockSpec((tm, LANE), lambda i: (i % nt, 0))
    chan = lambda c: pl.BlockSpec((None, c, tm), lambda i: (i // nt, 0, i % nt))
    rows = [A_W, LANE, LANE, I_W, B_W, LANE, LANE, d, d, LANE]
    chans = [LANE, LANE, LANE, LANE, IDX_DIM]
    return pl.pallas_call(
        functools.partial(_proj_kernel, d_model=d),
        grid=(n // tm,),
        in_specs=[row(d), const((1, d)), const((d, nw)), const((1, nw)), tab, tab],
        out_specs=[row(c) for c in rows] + [chan(c) for c in chans],
        out_shape=[jax.ShapeDtypeStruct((n, c), BF16 if k in (7, 8) else F32) for k, c in enumerate(rows)]
        + [jax.ShapeDtypeStruct((nseq, c, rows_per_seq), F32) for c in chans],
        compiler_params=_params(1),
        name="proj",
    )(x2d, n1, w, gains, cos, sin)


def _head_operand(q_ref, h, want_low, scale):
    ch = q_ref[:, LANE * (h // 2):LANE * (h // 2) + LANE]
    if (h % 2 == 0) != want_low:
        ch = pltpu.roll(ch, HEAD_DIM, axis=1)
    low = _iota(ch.shape, 1) < HEAD_DIM
    keep = low if want_low else jnp.logical_not(low)
    return (jnp.where(keep, ch, 0.0) * scale).astype(BF16)


def _fill_kv(k_ref, vt_ref, kb_s, vt_s, nchunk, tc):
    kb_s[...] = k_ref[...].astype(BF16)
    extra = jnp.where(_iota((V_ROWS - HEAD_DIM, tc), 0) == 0, 1.0, 0.0)
    for c in range(nchunk):
        for kv in range(2):
            vt = vt_ref[kv * HEAD_DIM:(kv + 1) * HEAD_DIM, c * tc:(c + 1) * tc]
            vt_s[kv, c] = jnp.concatenate([vt, extra], axis=0).astype(BF16)


def _attend_chunk(kc, c, q_s, vt_s, st, n_heads, heads_per_kv, tile_bias_of=None, row_bias_of=None):
    s_s, p_s, mc_s, a_s, m_s, acc_s = st
    for h in range(n_heads):
        s = _dot_nt(kc, q_s[h])
        if tile_bias_of is not None:
            s = s + tile_bias_of(h)
        s_s[h] = s
        mc_s[h] = jnp.max(s, axis=0, keepdims=True)
    for h in range(n_heads):
        m_old = m_s[h]
        if row_bias_of is None:
            m_new = jnp.maximum(m_old, mc_s[h])
            shift = m_new
        else:
            rb = row_bias_of(h)
            m_new = jnp.maximum(m_old, mc_s[h] + rb)
            shift = m_new - rb
        p_s[h] = jnp.exp2(s_s[h] - shift).astype(BF16)
        a_s[h] = jnp.exp2(m_old - m_new)
        m_s[h] = m_new
    for h in range(n_heads):
        acc_s[h] = a_s[h] * acc_s[h] + _dot(vt_s[h // heads_per_kv, c], p_s[h])


V_ROWS = HEAD_DIM + 16
LOG2E = 1.4426950408889634


def _attention_scratch(n_heads, tq):
    row = pltpu.VMEM((n_heads, 1, tq), F32)
    return [pltpu.VMEM((n_heads, tq, tq), F32), pltpu.VMEM((n_heads, tq, tq), BF16), row, row, row,
            pltpu.VMEM((n_heads, V_ROWS, tq), F32)]


def _init_softmax(st):
    m_s, acc_s = st[-2], st[-1]
    m_s[...] = jnp.full(m_s.shape, NEG, F32)
    acc_s[...] = jnp.zeros(acc_s.shape, F32)


def _write_heads(o_ref, st, n_heads):
    acc_s = st[-1]
    for j in range(n_heads // 2):
        pair = []
        for h in (2 * j, 2 * j + 1):
            acc = acc_s[h]
            pair.append(acc[:HEAD_DIM] / acc[HEAD_DIM:HEAD_DIM + 1])
        o_ref[:, LANE * j:LANE * (j + 1)] = jnp.concatenate(pair, axis=0).T.astype(o_ref.dtype)


def _select_topk(sweep, count_gt, n_sel, small, tq, t_total):
    row = lambda v: jnp.full((1, tq), v, F32)
    rowmax = sweep(lambda s, c, a: jnp.maximum(a, jnp.max(s, axis=0, keepdims=True)), row(NEG))
    rowmin = sweep(lambda s, c, a: jnp.minimum(
        a, jnp.min(jnp.where(s > 0.5 * NEG, s, BIG), axis=0, keepdims=True)), row(BIG))

    def bisect(_, lh):
        lo, hi = lh
        mid = 0.5 * (lo + hi)
        ge = count_gt(mid) >= n_sel
        return jnp.where(ge, mid, lo), jnp.where(ge, hi, mid)

    lo, _ = lax.fori_loop(0, BISECT_STEPS, bisect, (rowmin, rowmax))
    u0 = sweep(lambda s, c, a: jnp.minimum(
        a, jnp.min(jnp.where(s >= lo, s, BIG), axis=0, keepdims=True)), row(BIG))

    def walk_cond(st):
        return st[2] == 0

    def walk(st):
        u = st[0]

        def f(s, c, carry):
            cnt, nxt = carry
            gt = s > u
            cnt = cnt + jnp.sum(jnp.where(gt, 1.0, 0.0), axis=0, keepdims=True)
            nxt = jnp.minimum(nxt, jnp.min(jnp.where(gt, s, BIG), axis=0, keepdims=True))
            return cnt, nxt

        cnt, nxt = sweep(f, (row(0.0), row(BIG)))
        done = jnp.logical_or(cnt < n_sel, small)
        all_done = jnp.min(jnp.where(done, 1.0, 0.0)).astype(jnp.int32)
        return jnp.where(done, u, nxt), cnt, all_done

    u, cgt, _ = lax.while_loop(walk_cond, walk, (u0, row(0.0), jnp.int32(0)))
    vstar = jnp.where(small, NEG, u)
    need = jnp.where(small, 0.0, n_sel - cgt)
    n_eq = sweep(lambda s, c, a: a + jnp.sum(jnp.where(s == vstar, 1.0, 0.0), axis=0, keepdims=True), row(0.0))
    excess = jnp.max(jnp.where(jnp.logical_and(n_eq > need, jnp.logical_not(small)), 1.0, 0.0))

    n_jsteps = jnp.where(excess > 0.0, t_total.bit_length() + 1, 0).astype(jnp.int32)

    def jstep(_, jj):
        jlo, jhi = jj
        mid = lax.shift_right_logical(jlo + jhi, 1)
        e = sweep(lambda s, c, a: a + jnp.sum(
            jnp.where(s == vstar, jnp.where(c < mid, 1.0, 0.0), 0.0), axis=0, keepdims=True), row(0.0))
        ge = e >= need
        return jnp.where(ge, jlo, mid), jnp.where(ge, mid, jhi)

    zero_i = jnp.zeros((1, tq), jnp.int32)
    _, jcut = lax.fori_loop(0, n_jsteps, jstep, (zero_i, zero_i + t_total))
    return vstar, jnp.where(small, 0, jcut)


def _dsa_prompt_kernel(pt_ref, qi_ref, qa_ref, kwq_ref, kwk_ref, ka_ref, vat_ref,
                       ssc_ref, ssn_ref, svs_ref, sjc_ref, sqbd_ref, skan_ref, svan_ref, cak, cav,
                       o_ref, os_ref,
                       kib_s, kab_s, vt_s, qim_s, qam_s, sc_s, s_s, p_s, mc_s, a_s, m_s, acc_s,
                       bak, bav, sem, *, n_sel, t_total, rider):
    st = (s_s, p_s, mc_s, a_s, m_s, acc_s)
    tq = qi_ref.shape[0]
    tc = tq
    nchunk = t_total // tc
    i = pl.program_id(1)

    def sample_compute(k, slot):
        os_ref[k] = _dsa_sample_attend(ssc_ref[k], ssn_ref[k][:, 0:1], svs_ref[k][:, 0:1], sjc_ref[k][:, 0:1],
                                       sqbd_ref[k], skan_ref[k], svan_ref[k], bak, bav, slot, rider["past"])

    rps = rider["rows_per_step"]
    run_row = _sample_row_runner(pt_ref, pl.program_id(0) * pl.num_programs(1) + i, rps, rider["n_rows"],
                                 (cak, cav), (bak, bav), sem, rider["n_pages"], sample_compute)
    for k in range((rps + 1) // 2):
        run_row(k)

    @pl.when(i == 0)
    def _():
        kib_s[...] = kwk_ref[...].astype(BF16)
        _fill_kv(ka_ref, vat_ref, kab_s, vt_s, nchunk, tc)

    heads_per_kv = A_HEADS // A_KV_HEADS
    for h in range(IDX_HEADS):
        qim_s[h] = _head_operand(qi_ref, h, True, 1.0)
    for h in range(A_HEADS):
        qam_s[h] = _head_operand(qa_ref, h, (h // heads_per_kv) == 0, HEAD_DIM ** -0.5 * LOG2E)
    w8 = kwq_ref[...].T[IDX_DIM:IDX_DIM + IDX_HEADS, :] * (IDX_DIM ** -0.5 * IDX_HEADS ** -0.5)

    krow = _iota((tc, tq), 0)
    qcol = _iota((tc, tq), 1)

    def chunk(c):
        return pl.ds(pl.multiple_of(c * tc, tc), tc)

    def score_chunk(c, diag):
        kc = kib_s[chunk(c), :]
        a = jnp.zeros((tc, tq), F32)
        for h in range(IDX_HEADS):
            a = a + jnp.maximum(_dot_nt(kc, qim_s[h]), 0.0) * w8[h:h + 1, :]
        if diag:
            a = jnp.where(krow <= qcol, a, NEG)
        sc_s[chunk(c), :] = a

    def score_body(c, carry):
        score_chunk(c, False)
        return carry

    lax.fori_loop(0, i, score_body, 0)
    score_chunk(i, True)

    def sweep(fn, init):
        return lax.fori_loop(0, i + 1, lambda c, carry: fn(sc_s[chunk(c), :], c * tc + krow, carry), init)

    def count_gt(x):
        return sweep(lambda s, c, a: a + jnp.sum(jnp.where(s > x, 1.0, 0.0), axis=0, keepdims=True),
                     jnp.zeros((1, tq), F32))

    n_adm = i * tq + _iota((1, tq), 1) + 1
    small = n_adm <= n_sel
    vstar, jcut = _select_topk(sweep, count_gt, float(n_sel), small, tq, t_total)

    def bias_body(c, carry):
        s = sc_s[chunk(c), :]
        tie = jnp.where((c * tc + krow) < jcut, 0.0, NEG)
        sc_s[chunk(c), :] = jnp.where(s > vstar, 0.0, jnp.where(s == vstar, tie, NEG))
        return carry

    lax.fori_loop(0, i + 1, bias_body, 0)

    for k in range((rps + 1) // 2, rps):
        run_row(k)

    _init_softmax(st)

    def att_body(c, carry):
        _attend_chunk(kab_s[chunk(c), :], c, qam_s, vt_s, st, A_HEADS, heads_per_kv,
                      tile_bias_of=lambda h: sc_s[chunk(c), :])
        return carry

    lax.fori_loop(0, i + 1, att_body, 0)
    _write_heads(o_ref, st, A_HEADS)


def _dsa_branch(qi, qa, kw, ka, vat, batch, t, page_table, sample_inputs, caches):
    tq = Q_BLOCK
    nq = t // tq
    n_sel = min(IDX_TOPK, t // 4)
    qblk = lambda c: pl.BlockSpec((tq, c), lambda b, i, pt: (b * nq + i, 0))
    full = pl.BlockSpec((t, LANE), lambda b, i, pt: (b, 0))
    full_t = pl.BlockSpec((None, LANE, t), lambda b, i, pt: (b, 0, 0))
    rd = _rider(page_table, sample_inputs, caches, batch, nq)
    past = rd["static"]["past"]
    grid_spec = pltpu.PrefetchScalarGridSpec(
        num_scalar_prefetch=1,
        grid=(batch, nq),
        in_specs=[qblk(I_W), qblk(A_W), qblk(LANE), full, full, full_t] + rd["in_specs"],
        out_specs=[qblk(A_W), rd["out_spec"]],
        scratch_shapes=[
            pltpu.VMEM((t, LANE), BF16), pltpu.VMEM((t, LANE), BF16),
            pltpu.VMEM((2, nq, V_ROWS, tq), BF16),
            pltpu.VMEM((IDX_HEADS, tq, LANE), BF16), pltpu.VMEM((A_HEADS, tq, LANE), BF16),
            pltpu.VMEM((t, tq), F32),
        ] + _attention_scratch(A_HEADS, tq) + rd["scratch"],
    )
    return pl.pallas_call(
        functools.partial(_dsa_prompt_kernel, n_sel=n_sel, t_total=t, rider=rd["static"]),
        grid_spec=grid_spec,
        out_shape=[jax.ShapeDtypeStruct((batch * t, A_W), BF16), rd["out_shape"]],
        compiler_params=_params(2),
        name="dsa",
    )(page_table, qi, qa, kw, kw, ka, vat, *sample_inputs, *caches)


def _moba_prompt_kernel(pt_ref, qb_ref, kb_ref, vbt_ref, sqbd_ref, skbn_ref, svbn_ref, sqi_ref, swi_ref, skin_ref,
                        cbk, cbv, cik, o_ref, os_ref, osc_ref, osn_ref,
                        kbb_s, vt_s, kbar_s, qbm_s, selb_s, s_s, p_s, mc_s, a_s, m_s, acc_s,
                        bbk, bbv, bik, sem, *, n_blk, t_total, n_blk_s, rider):
    st = (s_s, p_s, mc_s, a_s, m_s, acc_s)
    tq = qb_ref.shape[0]
    tc = tq
    nb = t_total // tc
    nbp = kbar_s.shape[0]
    i = pl.program_id(1)
    heads_per_kv = B_HEADS // B_KV_HEADS

    def sample_compute(k, slot):
        os_ref[k] = _moba_sample_row(sqbd_ref[k], skbn_ref[k], svbn_ref[k], bbk, bbv, slot, n_blk_s, rider["past"])
        _dsa_sample_scores(sqi_ref[k], swi_ref[k], skin_ref[k], bik, slot, osc_ref.at[k], osn_ref.at[k],
                           rider["past"])

    def zero_pad_rows():
        bik[...] = jnp.zeros(bik.shape, F32)

    rps = rider["rows_per_step"]
    run_row = _sample_row_runner(pt_ref, pl.program_id(0) * pl.num_programs(1) + i, rps, rider["n_rows"],
                                 (cbk, cbv, cik), (bbk, bbv, bik), sem, rider["n_pages"], sample_compute,
                                 zero_pad_rows)
    for k in range((rps + 1) // 2):
        run_row(k)

    @pl.when(i == 0)
    def _():
        _fill_kv(kb_ref, vbt_ref, kbb_s, vt_s, nb, tc)
        kbar_s[...] = jnp.zeros(kbar_s.shape, F32)
        for n in range(nb):
            kbar_s[n:n + 1, :] = jnp.mean(kb_ref[n * tc:(n + 1) * tc, :], axis=0, keepdims=True)

    kbar = kbar_s[...].astype(BF16)
    blk = _iota((nbp, tq), 0)
    past = blk < i
    for h in range(B_HEADS):
        qm = _head_operand(qb_ref, h, (h // heads_per_kv) == 0, HEAD_DIM ** -0.5 * LOG2E)
        qbm_s[h] = qm
        gate = jnp.where(past, _dot_nt(kbar, qm), NEG)
        rank = jnp.zeros((nbp, tq), F32)
        for m in range(nb):
            gm = gate[m:m + 1, :]
            first = jnp.where(blk > m, 1.0, 0.0)
            rank = rank + jnp.where(gm > gate, 1.0, jnp.where(gm == gate, first, 0.0))
        sel = jnp.logical_and(past, rank < n_blk)
        selb_s[h] = jnp.where(sel, 0.0, NEG)

    _init_softmax(st)
    krow = _iota((tc, tq), 0)
    qcol = _iota((tc, tq), 1)

    def chunk(c):
        return pl.ds(pl.multiple_of(c * tc, tc), tc)

    _attend_chunk(kbb_s[chunk(i), :], i, qbm_s, vt_s, st, B_HEADS, heads_per_kv,
                  tile_bias_of=lambda h: jnp.where(krow <= qcol, 0.0, NEG))

    for k in range((rps + 1) // 2, rps):
        run_row(k)

    def att_body(c, carry):
        _attend_chunk(kbb_s[chunk(c), :], c, qbm_s, vt_s, st, B_HEADS, heads_per_kv,
                      row_bias_of=lambda h: selb_s[h, pl.ds(c, 1), :])
        return carry

    lax.fori_loop(0, i, att_body, 0)
    _write_heads(o_ref, st, B_HEADS)


def _moba_branch(qb, kb, vbt, batch, t, page_table, sample_inputs, caches):
    tq = Q_BLOCK
    assert tq == MOBA_BLOCK and t % tq == 0
    nq = t // tq
    n_blk = min(MOBA_TOPK, (t - 1) // MOBA_BLOCK)
    qblk = pl.BlockSpec((tq, B_W), lambda b, i, pt: (b * nq + i, 0))
    full = pl.BlockSpec((t, LANE), lambda b, i, pt: (b, 0))
    full_t = pl.BlockSpec((None, LANE, t), lambda b, i, pt: (b, 0, 0))
    rd = _rider(page_table, sample_inputs, caches, batch, nq)
    past, rps, db = rd["static"]["past"], rd["static"]["rows_per_step"], rd["static"]["n_rows"]
    ch = past // N_KEY_CHUNKS
    per_step = lambda shape: pl.BlockSpec((rps,) + shape, lambda b, i, pt: (b * nq + i, 0, 0))
    grid_spec = pltpu.PrefetchScalarGridSpec(
        num_scalar_prefetch=1,
        grid=(batch, nq),
        in_specs=[qblk, full, full_t] + rd["in_specs"],
        out_specs=[qblk, rd["out_spec"], per_step((N_KEY_CHUNKS, ch)), per_step((1, LANE))],
        scratch_shapes=[
            pltpu.VMEM((t, LANE), BF16),
            pltpu.VMEM((2, nq, V_ROWS, tq), BF16),
            pltpu.VMEM((max(nq, 8), LANE), F32),
            pltpu.VMEM((B_HEADS, tq, LANE), BF16),
            pltpu.VMEM((B_HEADS, max(nq, 8), tq), F32),
        ] + _attention_scratch(B_HEADS, tq) + rd["scratch"],
    )
    return pl.pallas_call(
        functools.partial(_moba_prompt_kernel, n_blk=float(n_blk), t_total=t,
                          n_blk_s=float(min(MOBA_TOPK, past // MOBA_BLOCK)), rider=rd["static"]),
        grid_spec=grid_spec,
        out_shape=[jax.ShapeDtypeStruct((batch * t, B_W), BF16), rd["out_shape"],
                   jax.ShapeDtypeStruct((db, N_KEY_CHUNKS, ch), F32), jax.ShapeDtypeStruct((db, 1, LANE), F32)],
        compiler_params=_params(2),
        name="moba",
    )(page_table, qb, kb, vbt, *sample_inputs, *caches)


def _page_copy(cache, buf, sem, k, slot, page, p):
    rows, width = cache.shape[1], cache.shape[2]
    return pltpu.make_async_copy(cache.at[page], buf.at[slot, pl.ds(0, rows), pl.ds(p * width, width)],
                                 sem.at[k, slot])


def _gather(pt_ref, b, caches, bufs, sem, slot, n_pages, start):
    for k, (cache, buf) in enumerate(zip(caches, bufs)):
        for p in range(n_pages):
            cp = _page_copy(cache, buf, sem, k, slot, pt_ref[b, p] if start else 0, p)
            if start:
                cp.start()
            else:
                cp.wait()


def _sample_row_runner(pt_ref, step, rows_per_step, n_rows, caches, bufs, sem, n_pages, compute, before_first=None):
    def run(k):
        r = step * rows_per_step + k
        slot = lax.rem(r, 2)

        @pl.when(r == 0)
        def _():
            if before_first is not None:
                before_first()
            _gather(pt_ref, r, caches, bufs, sem, slot, n_pages, True)

        @pl.when(r + 1 < n_rows)
        def _():
            _gather(pt_ref, r + 1, caches, bufs, sem, 1 - slot, n_pages, True)

        _gather(pt_ref, r, caches, bufs, sem, slot, n_pages, False)
        compute(k, slot)

    return run


def _sample_softmax(tiles, s_new, vt_tiles, v_new):
    m = s_new
    for s in tiles:
        m = jnp.maximum(m, jnp.max(s, axis=1, keepdims=True))
    p_new = jnp.exp(s_new - m)
    l = p_new
    o = p_new * v_new
    for s, vt in zip(tiles, vt_tiles):
        p = jnp.exp(s - m)
        l = l + jnp.sum(p, axis=1, keepdims=True)
        o = o + _dot_nt(p.astype(BF16), vt)
    return o / l


N_KEY_CHUNKS = 8


def _dsa_sample_scores(qi, wi, kin, bik, slot, sc_out, sn_out, past):
    nq = N_KEY_CHUNKS
    ch = past // nq
    qib = qi.astype(BF16)
    wrow = wi * (IDX_DIM ** -0.5 * IDX_HEADS ** -0.5)
    eye = _iota((IDX_HEADS, IDX_HEADS), 0) == _iota((IDX_HEADS, IDX_HEADS), 1)
    wcol = jnp.sum(jnp.where(eye, jnp.broadcast_to(wrow, (IDX_HEADS, IDX_HEADS)), 0.0), axis=1, keepdims=True)
    lg_new = jnp.sum(qi * kin, axis=1, keepdims=True)
    s_new = jnp.sum(jnp.maximum(lg_new, 0.0) * wcol, axis=0, keepdims=True)
    sn_out[...] = jnp.broadcast_to(s_new, sn_out.shape)
    for q in range(nq):
        kq = bik[slot, :, q * ch:(q + 1) * ch].astype(BF16)
        lg = jnp.maximum(_dot(qib, kq), 0.0)
        sc_out[q:q + 1, :] = jnp.sum(lg * wcol, axis=0, keepdims=True)


def _dsa_select_kernel(sc_ref, sn_ref, vstar_ref, jcut_ref, sct_s, extra_s, *, n_sel, past):
    db = sc_ref.shape[0]
    tc = Q_BLOCK
    nchunk = past // tc
    for c in range(nchunk):
        sct_s[c * tc:(c + 1) * tc, :] = sc_ref[:, c * tc:(c + 1) * tc].T
    new_row = sn_ref[...].T[0:1, :]
    extra_s[...] = jnp.where(_iota(extra_s.shape, 0) == 0, jnp.broadcast_to(new_row, extra_s.shape), NEG)
    krow = _iota((tc, db), 0)
    erow = _iota(extra_s.shape, 0)

    def sweep(fn, init):
        def body(c, carry):
            return fn(sct_s[pl.ds(pl.multiple_of(c * tc, tc), tc), :], c * tc + krow, carry)
        return fn(extra_s[...], past + erow, lax.fori_loop(0, nchunk, body, init))

    def count_gt(x):
        return sweep(lambda s, c, a: a + jnp.sum(jnp.where(s > x, 1.0, 0.0), axis=0, keepdims=True),
                     jnp.zeros((1, db), F32))

    small = jnp.zeros((1, db), jnp.int32) > 0
    vstar, jcut = _select_topk(sweep, count_gt, n_sel, small, db, past + 1)
    vstar_ref[...] = vstar
    jcut_ref[...] = jcut


def _dsa_select(scores, snew, past):
    db = scores.shape[0]
    n_sel = float(min(IDX_TOPK, (past + 1) // 4))
    full = lambda a: pl.BlockSpec(a.shape, lambda i: (0, 0))
    return pl.pallas_call(
        functools.partial(_dsa_select_kernel, n_sel=n_sel, past=past),
        grid=(1,),
        in_specs=[full(scores), full(snew)],
        out_specs=[pl.BlockSpec((1, db), lambda i: (0, 0)), pl.BlockSpec((1, db), lambda i: (0, 0))],
        out_shape=[jax.ShapeDtypeStruct((1, db), F32), jax.ShapeDtypeStruct((1, db), jnp.int32)],
        scratch_shapes=[pltpu.VMEM((past, db), F32), pltpu.VMEM((8, db), F32)],
        compiler_params=_params(1),
        name="dsa_select",
    )(scores, snew)


def _dsa_sample_attend(sc, s_new, vstar, jcut, qbd, kan, van, bak, bav, slot, past):
    nq = N_KEY_CHUNKS
    ch = past // nq
    kidx = _iota((nq, ch), 0) * ch + _iota((nq, ch), 1)
    bias = jnp.where(sc > vstar, 0.0, jnp.where(sc == vstar, jnp.where(kidx < jcut, 0.0, NEG), NEG))
    bias_new = jnp.where(s_new > vstar, 0.0, jnp.where(s_new == vstar, jnp.where(past < jcut, 0.0, NEG), NEG))

    qs = qbd * (HEAD_DIM ** -0.5)
    sn = jnp.sum(qs * kan, axis=1, keepdims=True) + bias_new
    qsb = qs.astype(BF16)
    tiles, vts = [], []
    for q in range(nq):
        kq = bak[slot, :, q * ch:(q + 1) * ch].astype(BF16)
        tiles.append(_dot(qsb, kq) + bias[q:q + 1, :])
        vts.append(bav[slot, :, q * ch:(q + 1) * ch].astype(BF16))
    return _sample_softmax(tiles, sn, vts, van)


def _moba_sample_row(qbd, kbn, vbn, bbk, bbv, slot, n_blk, past):
    nq = N_KEY_CHUNKS
    ch = past // nq
    nblk = past // MOBA_BLOCK
    bpc = ch // MOBA_BLOCK
    qs = qbd * (HEAD_DIM ** -0.5)
    qsb = qs.astype(BF16)
    sn = jnp.sum(qs * kbn, axis=1, keepdims=True)
    bcol = _iota((B_HEADS, nblk), 1).astype(F32)

    raw, vts = [], []
    gate = jnp.zeros((B_HEADS, nblk), F32)
    for q in range(nq):
        s = _dot(qsb, bbk[slot, :, q * ch:(q + 1) * ch].astype(BF16))
        raw.append(s)
        vts.append(bbv[slot, :, q * ch:(q + 1) * ch].astype(BF16))
        for k in range(bpc):
            g = jnp.sum(s[:, k * MOBA_BLOCK:(k + 1) * MOBA_BLOCK], axis=1, keepdims=True) * (1.0 / MOBA_BLOCK)
            gate = jnp.where(bcol == float(q * bpc + k), g, gate)
    rank = jnp.zeros((B_HEADS, nblk), F32)
    for m in range(nblk):
        gm = gate[:, m:m + 1]
        rank = rank + jnp.where(gm > gate, 1.0, jnp.where(gm == gate, jnp.where(bcol > float(m), 1.0, 0.0), 0.0))
    selm = jnp.where(rank < n_blk, 1.0, 0.0)
    tiles = []
    for q in range(nq):
        bias = jnp.concatenate(
            [jnp.broadcast_to(jnp.where(selm[:, q * bpc + k:q * bpc + k + 1] > 0.0, 0.0, NEG),
                              (B_HEADS, MOBA_BLOCK)) for k in range(bpc)], axis=1)
        tiles.append(raw[q] + bias)
    return _sample_softmax(tiles, sn, vts, vbn)


def _rider(page_table, small_inputs, caches, batch, nq):
    db, n_pages = page_table.shape
    page = caches[0].shape[2]
    past = n_pages * page
    n_steps = batch * nq
    assert past % (N_KEY_CHUNKS * MOBA_BLOCK) == 0 and page % LANE == 0 and db % n_steps == 0
    rps = db // n_steps
    per_step = lambda a: pl.BlockSpec((rps,) + a.shape[1:], lambda b, i, pt: (b * nq + i, 0, 0))
    return dict(
        in_specs=[per_step(a) for a in small_inputs] + [pl.BlockSpec(memory_space=pl.ANY)] * len(caches),
        out_spec=pl.BlockSpec((rps, 8, LANE), lambda b, i, pt: (b * nq + i, 0, 0)),
        out_shape=jax.ShapeDtypeStruct((db, 8, LANE), F32),
        scratch=[pltpu.VMEM((2, LANE, past), F32) for _ in caches] + [pltpu.SemaphoreType.DMA((len(caches), 2))],
        static=dict(n_rows=db, rows_per_step=rps, n_pages=n_pages, past=past),
    )


def _merge_kernel(x_ref, oa_ref, ob_ref, sga_ref, sgb_ref, wpa_ref, wpb_ref, wo_ref, y_ref):
    pa = _dot(oa_ref[...].astype(BF16), wpa_ref[...])
    pb = _dot(ob_ref[...].astype(BF16), wpb_ref[...])
    merged = sga_ref[...] * pa + sgb_ref[...] * pb
    y_ref[...] = x_ref[...] + _dot(merged.astype(BF16), wo_ref[...])


def _merge(x2d, oa, ob, sga, sgb, wpa, wpb, wo):
    n, d = x2d.shape
    tm = min(ROW_BLOCK, n)
    row = lambda c: pl.BlockSpec((tm, c), lambda i: (i, 0))
    const = lambda a: pl.BlockSpec(a.shape, lambda i: (0, 0))
    return pl.pallas_call(
        _merge_kernel,
        grid=(n // tm,),
        in_specs=[row(d), row(A_W), row(B_W), row(d), row(d), const(wpa), const(wpb), const(wo)],
        out_specs=row(d),
        out_shape=jax.ShapeDtypeStruct((n, d), F32),
        compiler_params=_params(1),
        name="merge",
    )(x2d, oa, ob, sga, sgb, wpa, wpb, wo)


FFN_COL_CHUNK = 1408


def _ffn_prompt_kernel(x_ref, oa_ref, ob_ref, sga_ref, sgb_ref, wpa_ref, wpb_ref, wo_ref,
                       n2_ref, wup_ref, cw_ref, cb_ref, wdn_ref, prev_ref, y_ref, tail_ref, ext_s, *, d_ff):
    tm = x_ref.shape[0]
    i = pl.program_id(1)

    @pl.when(i == 0)
    def _():
        ext_s[6:8, :] = prev_ref[...]

    pa = _dot(oa_ref[...].astype(BF16), wpa_ref[...])
    pb = _dot(ob_ref[...].astype(BF16), wpb_ref[...])
    merged = sga_ref[...] * pa + sgb_ref[...] * pb
    x = x_ref[...] + _dot(merged.astype(BF16), wo_ref[...])
    xn = _rms(x, n2_ref[...]).astype(BF16)
    fc = FFN_COL_CHUNK
    for c in range(2 * d_ff // fc):
        ext_s[8:8 + tm, c * fc:(c + 1) * fc] = _dot(xn, wup_ref[:, c * fc:(c + 1) * fc])

    def conv(lo):
        cols = slice(lo, lo + fc)
        out = cb_ref[:, cols] + ext_s[6:6 + tm, cols] * cw_ref[0:1, cols]
        out = out + ext_s[7:7 + tm, cols] * cw_ref[1:2, cols]
        return out + ext_s[8:8 + tm, cols] * cw_ref[2:3, cols]

    y = x
    for j in range(d_ff // fc):
        a = conv(j * fc)
        g = conv(d_ff + j * fc)
        act = (a / (1.0 + jnp.exp(-a)) * g).astype(BF16)
        y = y + _dot(act, wdn_ref[j * fc:(j + 1) * fc, :])
    y_ref[...] = y
    tail = ext_s[tm + 6:tm + 8, :]
    tail_ref[...] = tail
    ext_s[6:8, :] = tail


def _merge_ffn_prompt(x2d, oa, ob, sga, sgb, wpa, wpb, wo, n2, wup, cw, cb, wdn, prev, batch, t):
    n, d = x2d.shape
    d_ff = wdn.shape[0]
    assert d_ff % FFN_COL_CHUNK == 0
    tm = ROW_BLOCK
    nt = t // tm
    row = lambda c: pl.BlockSpec((tm, c), lambda b, i: (b * nt + i, 0))
    const = lambda a: pl.BlockSpec(a.shape, lambda b, i: (0, 0), pipeline_mode=pl.Buffered(1))
    per_b = pl.BlockSpec((None, 2, 2 * d_ff), lambda b, i: (b, 0, 0))
    return pl.pallas_call(
        functools.partial(_ffn_prompt_kernel, d_ff=d_ff),
        grid=(batch, nt),
        in_specs=[row(d), row(A_W), row(B_W), row(d), row(d), const(wpa), const(wpb), const(wo),
                  const(n2), const(wup), const(cw), const(cb), const(wdn), per_b],
        out_specs=[row(d), per_b],
        out_shape=[jax.ShapeDtypeStruct((n, d), F32), jax.ShapeDtypeStruct((batch, 2, 2 * d_ff), F32)],
        scratch_shapes=[pltpu.VMEM((tm + 8, 2 * d_ff), F32)],
        compiler_params=_params(2),
        name="merge_ffn_prompt",
    )(x2d, oa, ob, sga, sgb, wpa, wpb, wo, n2, wup, cw, cb, wdn, prev)


def _ffn_sample_kernel(x_ref, n2_ref, wup_ref, cw_ref, cb_ref, wdn_ref, s0_ref, s1_ref, y_ref, up_ref, *, d_ff):
    x = x_ref[...]
    xn = _rms(x, n2_ref[...]).astype(BF16)
    fc = FFN_COL_CHUNK
    for c in range(2 * d_ff // fc):
        up_ref[:, c * fc:(c + 1) * fc] = _dot(xn, wup_ref[:, c * fc:(c + 1) * fc])

    def conv(lo):
        cols = slice(lo, lo + fc)
        out = cb_ref[:, cols] + s0_ref[:, cols] * cw_ref[0:1, cols]
        out = out + s1_ref[:, cols] * cw_ref[1:2, cols]
        return out + up_ref[:, cols] * cw_ref[2:3, cols]

    y = x
    for j in range(d_ff // fc):
        a = conv(j * fc)
        g = conv(d_ff + j * fc)
        act = (a / (1.0 + jnp.exp(-a)) * g).astype(BF16)
        y = y + _dot(act, wdn_ref[j * fc:(j + 1) * fc, :])
    y_ref[...] = y


def _ffn_sample(x2d, n2, wup, cw, cb, wdn, s0, s1):
    n, d = x2d.shape
    d_ff = wdn.shape[0]
    full = lambda a: pl.BlockSpec(a.shape, lambda i: (0, 0), pipeline_mode=pl.Buffered(1))
    args = (x2d, n2, wup, cw, cb, wdn, s0, s1)
    return pl.pallas_call(
        functools.partial(_ffn_sample_kernel, d_ff=d_ff),
        grid=(1,),
        in_specs=[full(a) for a in args],
        out_specs=[pl.BlockSpec((n, d), lambda i: (0, 0)), pl.BlockSpec((n, 2 * d_ff), lambda i: (0, 0))],
        out_shape=[jax.ShapeDtypeStruct((n, d), F32), jax.ShapeDtypeStruct((n, 2 * d_ff), F32)],
        compiler_params=_params(1),
        name="ffn_sample",
    )(*args)


def _rope_tables(pos):
    half = HEAD_DIM // 2
    inv = ROPE_THETA ** (-jnp.arange(half, dtype=F32) / half)
    ang = pos.astype(F32)[:, None] * inv[None, :]
    cos = jnp.cos(ang)
    sin = jnp.sin(ang)
    return jnp.tile(cos, (1, 4)), jnp.tile(jnp.concatenate([-sin, sin], axis=1), (1, 2))


def _layout_w_in(w_in, d_model):
    splits = (A_W, A_KV_W, A_KV_W, I_W, IDX_DIM, IDX_HEADS, B_W, B_KV_W, B_KV_W, d_model, d_model)
    offs = [0]
    for s in splits:
        offs.append(offs[-1] + s)
    p = [w_in[:, offs[k]:offs[k + 1]] for k in range(len(splits))]
    qa, ka, va, qi, ki, wi, qb, kb, vb, ga, gb = p
    pad = jnp.zeros((w_in.shape[0], LANE - IDX_DIM - IDX_HEADS), w_in.dtype)
    return jnp.concatenate([qa, ka, va, qi, qb, kb, vb, ga, gb, ki, wi, pad], axis=1).astype(BF16)


def _layout_gains(q_norm_a, k_norm_a, k_norm_idx, q_norm_b, k_norm_b, d_model):
    one = lambda n: jnp.ones((n,), F32)
    return jnp.concatenate([
        jnp.tile(q_norm_a, A_HEADS), jnp.tile(k_norm_a, A_KV_HEADS), one(A_KV_W), one(I_W),
        jnp.tile(q_norm_b, B_HEADS), jnp.tile(k_norm_b, B_KV_HEADS), one(B_KV_W),
        one(2 * d_model), k_norm_idx, one(LANE - IDX_DIM)])[None, :]


def _block_diag_q(q, heads_per_kv):
    z = jnp.zeros_like(q)
    low = jnp.concatenate([q, z], axis=-1)
    high = jnp.concatenate([z, q], axis=-1)
    is_low = (jnp.arange(q.shape[1]) // heads_per_kv == 0)[None, :, None]
    return jnp.where(is_low, low, high)


def _pick_kv(o, heads_per_kv):
    n = o.shape[0]
    return jnp.concatenate([o[:, :heads_per_kv, :HEAD_DIM].reshape(n, -1),
                            o[:, heads_per_kv:, HEAD_DIM:].reshape(n, -1)], axis=1)


def _sample_operands(qa_s, ka_s, va_s, qi_s, qb_s, kb_s, vb_s, kw_s, c_ik, c_ak, c_av, c_bk, c_bv):
    db = qa_s.shape[0]
    n_pool, page = c_ak.shape[0], c_ak.shape[1]
    hpk_a = A_HEADS // A_KV_HEADS
    hpk_b = B_HEADS // B_KV_HEADS
    qi_pad = jnp.pad(qi_s.reshape(db, IDX_HEADS, IDX_DIM), ((0, 0), (0, 0), (0, LANE - IDX_DIM)))
    wi_s = kw_s[:, IDX_DIM:IDX_DIM + IDX_HEADS].reshape(db, 1, IDX_HEADS)
    qa_bd = _block_diag_q(qa_s.reshape(db, A_HEADS, HEAD_DIM), hpk_a)
    qb_bd = _block_diag_q(qb_s.reshape(db, B_HEADS, HEAD_DIM), hpk_b)
    kin = jnp.where(jnp.arange(LANE) < IDX_DIM, kw_s, 0.0).reshape(db, 1, LANE)
    pages_t = lambda c: jnp.moveaxis(c, 1, -1).reshape(n_pool, -1, page)
    dsa = ([qa_bd, ka_s.reshape(db, 1, LANE), va_s.reshape(db, 1, LANE)], [pages_t(c_ak), pages_t(c_av)])
    moba = ([qb_bd, kb_s.reshape(db, 1, LANE), vb_s.reshape(db, 1, LANE), qi_pad, wi_s, kin],
            [pages_t(c_bk), pages_t(c_bv), pages_t(c_ik)])
    return dsa, moba


def kernel(x_prompt, x_sample, cache_a_k, cache_a_v, cache_idx_k, cache_b_k, cache_b_v, state_conv, page_table, norm1, w_in, q_norm_a, k_norm_a, k_norm_idx, q_norm_b, k_norm_b, w_proj_a, w_proj_b, w_out, norm2, w_up, conv_w, conv_b, w_down):
    batch, t, d = x_prompt.shape
    db, ds, _ = x_sample.shape
    depth = norm1.shape[0]
    assert depth == 1 and ds == 1
    n_pool, page = cache_a_k.shape[1], cache_a_k.shape[2]
    past = page_table.shape[1] * page
    l = 0
    w = _layout_w_in(w_in[l], d)
    gains = _layout_gains(q_norm_a[l], k_norm_a[l], k_norm_idx[l], q_norm_b[l], k_norm_b[l], d)
    wpa, wpb, wo = w_proj_a[l].astype(BF16), w_proj_b[l].astype(BF16), w_out[l].astype(BF16)
    wup, wdn = w_up[l].astype(BF16), w_down[l].astype(BF16)
    n1, n2 = norm1[l][None, :], norm2[l][None, :]
    cw, cb = conv_w[l], conv_b[l][None, :]
    d_ff = wdn.shape[0]

    xp = x_prompt.reshape(batch * t, d)
    cos_p, sin_p = _rope_tables(jnp.arange(t, dtype=jnp.int32))
    qa, ka, _, qi, qb, kb, _, sga, sgb, kw, kat, vat, kbt, vbt, kit = _proj(xp, n1, w, gains, cos_p, sin_p, t)
    xs = x_sample.reshape(db, d)
    cos_s, sin_s = _rope_tables(jnp.full((db,), past, jnp.int32))
    (qa_s, ka_s, va_s, qi_s, qb_s, kb_s, vb_s, sga_s, sgb_s, kw_s,
     kat_s, vat_s, kbt_s, vbt_s, kit_s) = _proj(xs, n1, w, gains, cos_s, sin_s, db)

    dsa_s, moba_s = _sample_operands(qa_s, ka_s, va_s, qi_s, qb_s, kb_s, vb_s, kw_s,
                                     cache_idx_k[l], cache_a_k[l], cache_a_v[l], cache_b_k[l], cache_b_v[l])
    ob, ob_s, sc_s, sn_s = _moba_branch(qb, kb, vbt, batch, t, page_table, *moba_s)
    vstar_s, jcut_s = _dsa_select(sc_s.reshape(db, past), sn_s.reshape(db, LANE), past)
    per_row = lambda a: jnp.broadcast_to(a.reshape(db, 1, 1), (db, 1, LANE))
    oa, oa_s = _dsa_branch(qi, qa, kw, ka, vat, batch, t, page_table,
                           [sc_s, sn_s, per_row(vstar_s), per_row(jcut_s)] + dsa_s[0], dsa_s[1])
    oa_s = _pick_kv(oa_s, A_HEADS // A_KV_HEADS)
    ob_s = _pick_kv(ob_s, B_HEADS // B_KV_HEADS)

    yp, p_conv = _merge_ffn_prompt(xp, oa, ob, sga, sgb, wpa, wpb, wo, n2, wup, cw, cb, wdn,
                                   jnp.zeros((batch, 2, 2 * d_ff), F32), batch, t)
    x1_s = _merge(xs, oa_s, ob_s, sga_s, sgb_s, wpa, wpb, wo)
    ys, up_s = _ffn_sample(x1_s, n2, wup, cw, cb, wdn, state_conv[l, :, 0], state_conv[l, :, 1])

    def rows5(a, n, s, h):
        a = a.reshape(a.shape[0], h, HEAD_DIM, a.shape[2])
        return jnp.transpose(a, (0, 3, 1, 2)).reshape(1, n, s, h, HEAD_DIM)

    def rows4(a, n, s):
        return jnp.transpose(a, (0, 2, 1)).reshape(1, n, s, IDX_DIM)

    return (
        yp.reshape(batch, t, d), ys.reshape(db, 1, d),
        rows5(kat, batch, t, A_KV_HEADS), rows5(vat, batch, t, A_KV_HEADS), rows4(kit, batch, t),
        rows5(kbt, batch, t, B_KV_HEADS), rows5(vbt, batch, t, B_KV_HEADS),
        p_conv[None],
        rows5(kat_s, db, 1, A_KV_HEADS), rows5(vat_s, db, 1, A_KV_HEADS), rows4(kit_s, db, 1),
        rows5(kbt_s, db, 1, B_KV_HEADS), rows5(vbt_s, db, 1, B_KV_HEADS),
        jnp.stack([state_conv[l, :, 1], up_s], axis=1)[None],
    )
```

```python
import functools

import jax
import jax.numpy as jnp
from jax import lax
from jax.experimental import pallas as pl
from jax.experimental.pallas import tpu as pltpu

HEAD_DIM = 64
A_HEADS = 8
A_KV_HEADS = 2
IDX_HEADS = 8
IDX_DIM = 64
IDX_TOPK = 256
B_HEADS = 8
B_KV_HEADS = 2
MOBA_BLOCK = 256
MOBA_TOPK = 3
ROPE_THETA = 10000.0
EPS = 1e-6
NEG = -1e30
BIG = 3e38

LANE = 128
Q_BLOCK = 256
ROW_BLOCK = 256
VMEM_LIMIT = 56 * 1024 * 1024
BISECT_STEPS = 13
SAMPLE_SPLIT = 16
SAMPLE_SPLIT_ROUNDS = 4

F32 = jnp.float32
BF16 = jnp.bfloat16

A_W = A_HEADS * HEAD_DIM
A_KV_W = A_KV_HEADS * HEAD_DIM
I_W = IDX_HEADS * IDX_DIM
B_W = B_HEADS * HEAD_DIM
B_KV_W = B_KV_HEADS * HEAD_DIM
assert A_W == 512 and I_W == 512 and B_W == 512 and A_KV_W == LANE and B_KV_W == LANE


def _params(n_grid):
    return pltpu.CompilerParams(dimension_semantics=("arbitrary",) * n_grid,
                                vmem_limit_bytes=VMEM_LIMIT)


def _dot(a, b):
    return jnp.dot(a, b, preferred_element_type=F32)


def _dot_nt(a, b):
    return lax.dot_general(a, b, (((1,), (1,)), ((), ())), preferred_element_type=F32)


def _iota(shape, dim):
    return lax.broadcasted_iota(jnp.int32, shape, dim)


def _rms(x, g):
    return x * lax.rsqrt(jnp.mean(x * x, axis=-1, keepdims=True) + EPS) * g


def _proj_kernel(x_ref, n1_ref, w_ref, g_ref, cos_ref, sin_ref,
                 qa_ref, ka_ref, va_ref, qi_ref, qb_ref, kb_ref, vb_ref, sga_ref, sgb_ref, kw_ref,
                 kat_ref, vat_ref, kbt_ref, vbt_ref, kit_ref, *, d_model):
    h = _rms(x_ref[...], n1_ref[...]).astype(BF16)
    gr = lax.shift_right_logical(_iota((2 * LANE, 2 * LANE), 0), 6)
    gc = lax.shift_right_logical(_iota((2 * LANE, 2 * LANE), 1), 6)
    gsum = jnp.where(gr == gc, 1.0, 0.0).astype(BF16)
    cos1 = cos_ref[...]
    sin1 = sin_ref[...]

    def seg(off, width):
        return _dot(h, w_ref[:, off:off + width])

    def head_norm(x, off):
        width = x.shape[1]
        ss = _dot((x * x).astype(BF16), gsum[:width, :width])
        return x * lax.rsqrt(ss * (1.0 / HEAD_DIM) + EPS) * g_ref[:, off:off + width]

    def rope(x):
        width = x.shape[1]
        rep = width // LANE
        cs = jnp.concatenate([cos1] * rep, axis=1) if rep > 1 else cos1
        sn = jnp.concatenate([sin1] * rep, axis=1) if rep > 1 else sin1
        hi = (_iota(x.shape, 1) & (HEAD_DIM // 2)) != 0
        swapped = jnp.where(hi, pltpu.roll(x, HEAD_DIM // 2, axis=1),
                            pltpu.roll(x, width - HEAD_DIM // 2, axis=1))
        return x * cs + swapped * sn

    o = 0
    for half in range(2):
        x = seg(o + half * 256, 256)
        qa_ref[:, half * 256:(half + 1) * 256] = rope(head_norm(x, o + half * 256))
    o = A_W
    x = seg(o, 256)
    xn = rope(head_norm(x, o))
    ka_ref[...] = xn[:, :LANE]
    va_ref[...] = x[:, LANE:]
    kat_ref[...] = xn[:, :LANE].T
    vat_ref[...] = x[:, LANE:].T
    o = A_W + 2 * LANE
    for half in range(2):
        qi_ref[:, half * 256:(half + 1) * 256] = rope(seg(o + half * 256, 256))
    o = A_W + 2 * LANE + I_W
    for half in range(2):
        x = seg(o + half * 256, 256)
        qb_ref[:, half * 256:(half + 1) * 256] = rope(head_norm(x, o + half * 256))
    o = A_W + 2 * LANE + I_W + B_W
    x = seg(o, 256)
    xn = rope(head_norm(x, o))
    kb_ref[...] = xn[:, :LANE]
    vb_ref[...] = x[:, LANE:]
    kbt_ref[...] = xn[:, :LANE].T
    vbt_ref[...] = x[:, LANE:].T
    o = A_W + 2 * LANE + I_W + B_W + 2 * LANE
    for part in range(d_model // 256):
        x = seg(o + part * 256, 256)
        sga_ref[:, part * 256:(part + 1) * 256] = (1.0 / (1.0 + jnp.exp(-x))).astype(sga_ref.dtype)
    o += d_model
    for part in range(d_model // 256):
        x = seg(o + part * 256, 256)
        sgb_ref[:, part * 256:(part + 1) * 256] = (1.0 / (1.0 + jnp.exp(-x))).astype(sgb_ref.dtype)
    o += d_model
    x = seg(o, LANE)
    xn = rope(head_norm(x, o))
    kw = jnp.where(_iota(x.shape, 1) < IDX_DIM, xn, x)
    kw_ref[...] = kw
    kit_ref[...] = kw.T[:IDX_DIM]


def _proj(x2d, n1, w, gains, cos, sin, rows_per_seq):
    n, d = x2d.shape
    tm = min(ROW_BLOCK, rows_per_seq)
    nt = rows_per_seq // tm
    nseq = n // rows_per_seq
    nw = w.shape[1]
    row = lambda c: pl.BlockSpec((tm, c), lambda i: (i, 0))
    const = lambda shape: pl.BlockSpec(shape, lambda i: (0, 0))
    tab = pl.BlockSpec((tm, LANE), lambda i: (i % nt, 0))
    chan = lambda c: pl.BlockSpec((None, c, tm), lambda i: (i // nt, 0, i % nt))
    rows = [A_W, LANE, LANE, I_W, B_W, LANE, LANE, d, d, LANE]
    chans = [LANE, LANE, LANE, LANE, IDX_DIM]
    return pl.pallas_call(
        functools.partial(_proj_kernel, d_model=d),
        grid=(n // tm,),
        in_specs=[row(d), const((1, d)), const((d, nw)), const((1, nw)), tab, tab],
        out_specs=[row(c) for c in rows] + [chan(c) for c in chans],
        out_shape=[jax.ShapeDtypeStruct((n, c), BF16 if k in (7, 8) else F32) for k, c in enumerate(rows)]
        + [jax.ShapeDtypeStruct((nseq, c, rows_per_seq), F32) for c in chans],
        compiler_params=_params(1),
        name="proj",
    )(x2d, n1, w, gains, cos, sin)


def _head_operand(q_ref, h, want_low, scale):
    ch = q_ref[:, LANE * (h // 2):LANE * (h // 2) + LANE]
    if (h % 2 == 0) != want_low:
        ch = pltpu.roll(ch, HEAD_DIM, axis=1)
    low = _iota(ch.shape, 1) < HEAD_DIM
    keep = low if want_low else jnp.logical_not(low)
    return (jnp.where(keep, ch, 0.0) * scale).astype(BF16)


def _fill_kv(k_ref, vt_ref, kb_s, vt_s, nchunk, tc):
    kb_s[...] = k_ref[...].astype(BF16)
    extra = jnp.where(_iota((V_ROWS - HEAD_DIM, tc), 0) == 0, 1.0, 0.0)
    for c in range(nchunk):
        for kv in range(2):
            vt = vt_ref[kv * HEAD_DIM:(kv + 1) * HEAD_DIM, c * tc:(c + 1) * tc]
            vt_s[kv, c] = jnp.concatenate([vt, extra], axis=0).astype(BF16)


def _attend_chunk(kc, c, q_s, vt_s, st, n_heads, heads_per_kv, tile_bias_of=None, row_bias_of=None):
    s_s, p_s, mc_s, a_s, m_s, acc_s = st
    for h in range(n_heads):
        s = _dot_nt(kc, q_s[h])
        if tile_bias_of is not None:
            s = s + tile_bias_of(h)
        s_s[h] = s
        mc_s[h] = jnp.max(s, axis=0, keepdims=True)
    for h in range(n_heads):
        m_old = m_s[h]
        if row_bias_of is None:
            m_new = jnp.maximum(m_old, mc_s[h])
            shift = m_new
        else:
            rb = row_bias_of(h)
            m_new = jnp.maximum(m_old, mc_s[h] + rb)
            shift = m_new - rb
        p_s[h] = jnp.exp2(s_s[h] - shift).astype(BF16)
        a_s[h] = jnp.exp2(m_old - m_new)
        m_s[h] = m_new
    for h in range(n_heads):
        acc_s[h] = a_s[h] * acc_s[h] + _dot(vt_s[h // heads_per_kv, c], p_s[h])


V_ROWS = HEAD_DIM + 16
LOG2E = 1.4426950408889634


def _attention_scratch(n_heads, tq):
    row = pltpu.VMEM((n_heads, 1, tq), F32)
    return [pltpu.VMEM((n_heads, tq, tq), F32), pltpu.VMEM((n_heads, tq, tq), BF16), row, row, row,
            pltpu.VMEM((n_heads, V_ROWS, tq), F32)]


def _init_softmax(st):
    m_s, acc_s = st[-2], st[-1]
    m_s[...] = jnp.full(m_s.shape, NEG, F32)
    acc_s[...] = jnp.zeros(acc_s.shape, F32)


def _write_heads(o_ref, st, n_heads):
    acc_s = st[-1]
    for j in range(n_heads // 2):
        pair = []
        for h in (2 * j, 2 * j + 1):
            acc = acc_s[h]
            pair.append(acc[:HEAD_DIM] / acc[HEAD_DIM:HEAD_DIM + 1])
        o_ref[:, LANE * j:LANE * (j + 1)] = jnp.concatenate(pair, axis=0).T.astype(o_ref.dtype)


def _select_topk(sweep, count_gt, n_sel, small, tq, t_total, score_range=None):
    row = lambda v: jnp.full((1, tq), v, F32)
    if score_range is not None:
        rowmax, rowmin = score_range
    else:
        rowmax = sweep(lambda s, c, a: jnp.maximum(a, jnp.max(s, axis=0, keepdims=True)), row(NEG))
        rowmin = sweep(lambda s, c, a: jnp.minimum(
            a, jnp.min(jnp.where(s > 0.5 * NEG, s, BIG), axis=0, keepdims=True)), row(BIG))

    def bisect(_, lh):
        lo, hi = lh
        mid = 0.5 * (lo + hi)
        ge = count_gt(mid) >= n_sel
        return jnp.where(ge, mid, lo), jnp.where(ge, hi, mid)

    lo, _ = lax.fori_loop(0, BISECT_STEPS, bisect, (rowmin, rowmax))
    u0 = sweep(lambda s, c, a: jnp.minimum(
        a, jnp.min(jnp.where(s >= lo, s, BIG), axis=0, keepdims=True)), row(BIG))

    def walk_cond(st):
        return st[2] == 0

    def walk(st):
        u = st[0]

        def f(s, c, carry):
            cnt, nxt = carry
            gt = s > u
            cnt = cnt + jnp.sum(jnp.where(gt, 1.0, 0.0), axis=0, keepdims=True)
            nxt = jnp.minimum(nxt, jnp.min(jnp.where(gt, s, BIG), axis=0, keepdims=True))
            return cnt, nxt

        cnt, nxt = sweep(f, (row(0.0), row(BIG)))
        done = jnp.logical_or(cnt < n_sel, small)
        all_done = jnp.min(jnp.where(done, 1.0, 0.0)).astype(jnp.int32)
        return jnp.where(done, u, nxt), cnt, all_done

    u, cgt, _ = lax.while_loop(walk_cond, walk, (u0, row(0.0), jnp.int32(0)))
    vstar = jnp.where(small, NEG, u)
    need = jnp.where(small, 0.0, n_sel - cgt)
    n_eq = sweep(lambda s, c, a: a + jnp.sum(jnp.where(s == vstar, 1.0, 0.0), axis=0, keepdims=True), row(0.0))
    excess = jnp.max(jnp.where(jnp.logical_and(n_eq > need, jnp.logical_not(small)), 1.0, 0.0))

    n_jsteps = jnp.where(excess > 0.0, t_total.bit_length() + 1, 0).astype(jnp.int32)

    def jstep(_, jj):
        jlo, jhi = jj
        mid = lax.shift_right_logical(jlo + jhi, 1)
        e = sweep(lambda s, c, a: a + jnp.sum(
            jnp.where(s == vstar, jnp.where(c < mid, 1.0, 0.0), 0.0), axis=0, keepdims=True), row(0.0))
        ge = e >= need
        return jnp.where(ge, jlo, mid), jnp.where(ge, mid, jhi)

    zero_i = jnp.zeros((1, tq), jnp.int32)
    _, jcut = lax.fori_loop(0, n_jsteps, jstep, (zero_i, zero_i + t_total))
    return vstar, jnp.where(small, 0, jcut)


def _dsa_prompt_kernel(pt_ref, qi_ref, qa_ref, kwq_ref, kwk_ref, ka_ref, vat_ref,
                       ssc_ref, ssn_ref, svs_ref, sjc_ref, sqbd_ref, skan_ref, svan_ref, cak, cav,
                       o_ref, os_ref,
                       kib_s, kab_s, vt_s, qim_s, qam_s, sc_s, s_s, p_s, mc_s, a_s, m_s, acc_s,
                       bak, bav, sem, *, n_sel, t_total, rider):
    st = (s_s, p_s, mc_s, a_s, m_s, acc_s)
    tq = qi_ref.shape[0]
    tc = tq
    nchunk = t_total // tc
    i = pl.program_id(1)

    def sample_compute(k, slot):
        os_ref[k] = _dsa_sample_attend(ssc_ref[k], ssn_ref[k][:, 0:1], svs_ref[k][:, 0:1], sjc_ref[k][:, 0:1],
                                       sqbd_ref[k], skan_ref[k], svan_ref[k], bak, bav, slot, rider["past"])

    rps = rider["rows_per_step"]
    run_row = _sample_row_runner(pt_ref, pl.program_id(0) * pl.num_programs(1) + i, rps, rider["n_rows"],
                                 (cak, cav), (bak, bav), sem, rider["n_pages"], sample_compute)
    for k in range((rps + 1) // 2):
        run_row(k)

    @pl.when(i == 0)
    def _():
        kib_s[...] = kwk_ref[...].astype(BF16)
        _fill_kv(ka_ref, vat_ref, kab_s, vt_s, nchunk, tc)

    heads_per_kv = A_HEADS // A_KV_HEADS
    for h in range(IDX_HEADS):
        qim_s[h] = _head_operand(qi_ref, h, True, 1.0)
    for h in range(A_HEADS):
        qam_s[h] = _head_operand(qa_ref, h, (h // heads_per_kv) == 0, HEAD_DIM ** -0.5 * LOG2E)
    w8 = kwq_ref[...].T[IDX_DIM:IDX_DIM + IDX_HEADS, :] * (IDX_DIM ** -0.5 * IDX_HEADS ** -0.5)

    krow = _iota((tc, tq), 0)
    qcol = _iota((tc, tq), 1)

    def chunk(c):
        return pl.ds(pl.multiple_of(c * tc, tc), tc)

    def score_chunk(c, diag, rng):
        kc = kib_s[chunk(c), :]
        a = jnp.zeros((tc, tq), F32)
        for h in range(IDX_HEADS):
            a = a + jnp.maximum(_dot_nt(kc, qim_s[h]), 0.0) * w8[h:h + 1, :]
        lo_src = a
        if diag:
            lo_src = jnp.where(krow <= qcol, a, BIG)
            a = jnp.where(krow <= qcol, a, NEG)
        sc_s[chunk(c), :] = a
        return (jnp.maximum(rng[0], jnp.max(a, axis=0, keepdims=True)),
                jnp.minimum(rng[1], jnp.min(lo_src, axis=0, keepdims=True)))

    rng0 = (jnp.full((1, tq), NEG, F32), jnp.full((1, tq), BIG, F32))
    rowmax, rowmin = score_chunk(i, True, lax.fori_loop(0, i, lambda c, rng: score_chunk(c, False, rng), rng0))

    def sweep(fn, init):
        return lax.fori_loop(0, i + 1, lambda c, carry: fn(sc_s[chunk(c), :], c * tc + krow, carry), init)

    def count_gt(x):
        part = sweep(lambda s, c, a: a + jnp.sum(jnp.where(s > x, 1.0, 0.0).reshape(tc // 8, 8, tq), axis=0),
                     jnp.zeros((8, tq), F32))
        return jnp.sum(part, axis=0, keepdims=True)

    n_adm = i * tq + _iota((1, tq), 1) + 1
    small = n_adm <= n_sel
    vstar, jcut = _select_topk(sweep, count_gt, float(n_sel), small, tq, t_total, (rowmax, rowmin))

    def bias_body(c, carry):
        s = sc_s[chunk(c), :]
        tie = jnp.where((c * tc + krow) < jcut, 0.0, NEG)
        sc_s[chunk(c), :] = jnp.where(s > vstar, 0.0, jnp.where(s == vstar, tie, NEG))
        return carry

    lax.fori_loop(0, i + 1, bias_body, 0)

    for k in range((rps + 1) // 2, rps):
        run_row(k)

    _init_softmax(st)

    def att_body(c, carry):
        _attend_chunk(kab_s[chunk(c), :], c, qam_s, vt_s, st, A_HEADS, heads_per_kv,
                      tile_bias_of=lambda h: sc_s[chunk(c), :])
        return carry

    lax.fori_loop(0, i + 1, att_body, 0)
    _write_heads(o_ref, st, A_HEADS)


def _dsa_branch(qi, qa, kw, ka, vat, batch, t, page_table, sample_inputs, caches):
    tq = Q_BLOCK
    nq = t // tq
    n_sel = min(IDX_TOPK, t // 4)
    qblk = lambda c: pl.BlockSpec((tq, c), lambda b, i, pt: (b * nq + i, 0))
    full = pl.BlockSpec((t, LANE), lambda b, i, pt: (b, 0))
    full_t = pl.BlockSpec((None, LANE, t), lambda b, i, pt: (b, 0, 0))
    rd = _rider(page_table, sample_inputs, caches, batch, nq)
    past = rd["static"]["past"]
    grid_spec = pltpu.PrefetchScalarGridSpec(
        num_scalar_prefetch=1,
        grid=(batch, nq),
        in_specs=[qblk(I_W), qblk(A_W), qblk(LANE), full, full, full_t] + rd["in_specs"],
        out_specs=[qblk(A_W), rd["out_spec"]],
        scratch_shapes=[
            pltpu.VMEM((t, LANE), BF16), pltpu.VMEM((t, LANE), BF16),
            pltpu.VMEM((2, nq, V_ROWS, tq), BF16),
            pltpu.VMEM((IDX_HEADS, tq, LANE), BF16), pltpu.VMEM((A_HEADS, tq, LANE), BF16),
            pltpu.VMEM((t, tq), F32),
        ] + _attention_scratch(A_HEADS, tq) + rd["scratch"],
    )
    return pl.pallas_call(
        functools.partial(_dsa_prompt_kernel, n_sel=n_sel, t_total=t, rider=rd["static"]),
        grid_spec=grid_spec,
        out_shape=[jax.ShapeDtypeStruct((batch * t, A_W), BF16), rd["out_shape"]],
        compiler_params=_params(2),
        name="dsa",
    )(page_table, qi, qa, kw, kw, ka, vat, *sample_inputs, *caches)


def _moba_prompt_kernel(pt_ref, qb_ref, kb_ref, vbt_ref, sqbd_ref, skbn_ref, svbn_ref, sqi_ref, swi_ref, skin_ref,
                        cbk, cbv, cik, o_ref, os_ref, osc_ref, osn_ref,
                        kbb_s, vt_s, kbar_s, qbm_s, selb_s, s_s, p_s, mc_s, a_s, m_s, acc_s,
                        bbk, bbv, bik, sem, *, n_blk, t_total, n_blk_s, rider):
    st = (s_s, p_s, mc_s, a_s, m_s, acc_s)
    tq = qb_ref.shape[0]
    tc = tq
    nb = t_total // tc
    nbp = kbar_s.shape[0]
    i = pl.program_id(1)
    heads_per_kv = B_HEADS // B_KV_HEADS

    def sample_compute(k, slot):
        os_ref[k] = _moba_sample_row(sqbd_ref[k], skbn_ref[k], svbn_ref[k], bbk, bbv, slot, n_blk_s, rider["past"])
        _dsa_sample_scores(sqi_ref[k], swi_ref[k], skin_ref[k], bik, slot, osc_ref.at[k], osn_ref.at[k],
                           rider["past"])

    def zero_pad_rows():
        bik[...] = jnp.zeros(bik.shape, F32)

    rps = rider["rows_per_step"]
    run_row = _sample_row_runner(pt_ref, pl.program_id(0) * pl.num_programs(1) + i, rps, rider["n_rows"],
                                 (cbk, cbv, cik), (bbk, bbv, bik), sem, rider["n_pages"], sample_compute,
                                 zero_pad_rows)
    for k in range((rps + 1) // 2):
        run_row(k)

    @pl.when(i == 0)
    def _():
        _fill_kv(kb_ref, vbt_ref, kbb_s, vt_s, nb, tc)
        kbar_s[...] = jnp.zeros(kbar_s.shape, F32)
        for n in range(nb):
            kbar_s[n:n + 1, :] = jnp.mean(kb_ref[n * tc:(n + 1) * tc, :], axis=0, keepdims=True)

    kbar = kbar_s[...].astype(BF16)
    blk = _iota((nbp, tq), 0)
    past = blk < i
    for h in range(B_HEADS):
        qm = _head_operand(qb_ref, h, (h // heads_per_kv) == 0, HEAD_DIM ** -0.5 * LOG2E)
        qbm_s[h] = qm
        gate = jnp.where(past, _dot_nt(kbar, qm), NEG)
        rank = jnp.zeros((nbp, tq), F32)
        for m in range(nb):
            gm = gate[m:m + 1, :]
            first = jnp.where(blk > m, 1.0, 0.0)
            rank = rank + jnp.where(gm > gate, 1.0, jnp.where(gm == gate, first, 0.0))
        sel = jnp.logical_and(past, rank < n_blk)
        selb_s[h] = jnp.where(sel, 0.0, NEG)

    _init_softmax(st)
    krow = _iota((tc, tq), 0)
    qcol = _iota((tc, tq), 1)

    def chunk(c):
        return pl.ds(pl.multiple_of(c * tc, tc), tc)

    _attend_chunk(kbb_s[chunk(i), :], i, qbm_s, vt_s, st, B_HEADS, heads_per_kv,
                  tile_bias_of=lambda h: jnp.where(krow <= qcol, 0.0, NEG))

    for k in range((rps + 1) // 2, rps):
        run_row(k)

    def att_body(c, carry):
        _attend_chunk(kbb_s[chunk(c), :], c, qbm_s, vt_s, st, B_HEADS, heads_per_kv,
                      row_bias_of=lambda h: selb_s[h, pl.ds(c, 1), :])
        return carry

    lax.fori_loop(0, i, att_body, 0)
    _write_heads(o_ref, st, B_HEADS)


def _moba_branch(qb, kb, vbt, batch, t, page_table, sample_inputs, caches):
    tq = Q_BLOCK
    assert tq == MOBA_BLOCK and t % tq == 0
    nq = t // tq
    n_blk = min(MOBA_TOPK, (t - 1) // MOBA_BLOCK)
    qblk = pl.BlockSpec((tq, B_W), lambda b, i, pt: (b * nq + i, 0))
    full = pl.BlockSpec((t, LANE), lambda b, i, pt: (b, 0))
    full_t = pl.BlockSpec((None, LANE, t), lambda b, i, pt: (b, 0, 0))
    rd = _rider(page_table, sample_inputs, caches, batch, nq)
    past, rps, db = rd["static"]["past"], rd["static"]["rows_per_step"], rd["static"]["n_rows"]
    ch = past // N_KEY_CHUNKS
    per_step = lambda shape: pl.BlockSpec((rps,) + shape, lambda b, i, pt: (b * nq + i, 0, 0))
    grid_spec = pltpu.PrefetchScalarGridSpec(
        num_scalar_prefetch=1,
        grid=(batch, nq),
        in_specs=[qblk, full, full_t] + rd["in_specs"],
        out_specs=[qblk, rd["out_spec"], per_step((N_KEY_CHUNKS, ch)), per_step((1, LANE))],
        scratch_shapes=[
            pltpu.VMEM((t, LANE), BF16),
            pltpu.VMEM((2, nq, V_ROWS, tq), BF16),
            pltpu.VMEM((max(nq, 8), LANE), F32),
            pltpu.VMEM((B_HEADS, tq, LANE), BF16),
            pltpu.VMEM((B_HEADS, max(nq, 8), tq), F32),
        ] + _attention_scratch(B_HEADS, tq) + rd["scratch"],
    )
    return pl.pallas_call(
        functools.partial(_moba_prompt_kernel, n_blk=float(n_blk), t_total=t,
                          n_blk_s=float(min(MOBA_TOPK, past // MOBA_BLOCK)), rider=rd["static"]),
        grid_spec=grid_spec,
        out_shape=[jax.ShapeDtypeStruct((batch * t, B_W), BF16), rd["out_shape"],
                   jax.ShapeDtypeStruct((db, N_KEY_CHUNKS, ch), F32), jax.ShapeDtypeStruct((db, 1, LANE), F32)],
        compiler_params=_params(2),
        name="moba",
    )(page_table, qb, kb, vbt, *sample_inputs, *caches)


def _page_copy(cache, buf, sem, k, slot, page, p):
    rows, width = cache.shape[1], cache.shape[2]
    return pltpu.make_async_copy(cache.at[page], buf.at[slot, pl.ds(0, rows), pl.ds(p * width, width)],
                                 sem.at[k, slot])


def _gather(pt_ref, b, caches, bufs, sem, slot, n_pages, start):
    for k, (cache, buf) in enumerate(zip(caches, bufs)):
        for p in range(n_pages):
            cp = _page_copy(cache, buf, sem, k, slot, pt_ref[b, p] if start else 0, p)
            if start:
                cp.start()
            else:
                cp.wait()


def _sample_row_runner(pt_ref, step, rows_per_step, n_rows, caches, bufs, sem, n_pages, compute, before_first=None):
    def run(k):
        r = step * rows_per_step + k
        slot = lax.rem(r, 2)

        @pl.when(r == 0)
        def _():
            if before_first is not None:
                before_first()
            _gather(pt_ref, r, caches, bufs, sem, slot, n_pages, True)

        @pl.when(r + 1 < n_rows)
        def _():
            _gather(pt_ref, r + 1, caches, bufs, sem, 1 - slot, n_pages, True)

        _gather(pt_ref, r, caches, bufs, sem, slot, n_pages, False)
        compute(k, slot)

    return run


def _sample_softmax(tiles, s_new, vt_tiles, v_new):
    m = s_new
    for s in tiles:
        m = jnp.maximum(m, jnp.max(s, axis=1, keepdims=True))
    p_new = jnp.exp(s_new - m)
    l = p_new
    o = p_new * v_new
    for s, vt in zip(tiles, vt_tiles):
        p = jnp.exp(s - m)
        l = l + jnp.sum(p, axis=1, keepdims=True)
        o = o + _dot_nt(p.astype(BF16), vt)
    return o / l


N_KEY_CHUNKS = 8


def _dsa_sample_scores(qi, wi, kin, bik, slot, sc_out, sn_out, past):
    nq = N_KEY_CHUNKS
    ch = past // nq
    qib = qi.astype(BF16)
    wrow = wi * (IDX_DIM ** -0.5 * IDX_HEADS ** -0.5)
    eye = _iota((IDX_HEADS, IDX_HEADS), 0) == _iota((IDX_HEADS, IDX_HEADS), 1)
    wcol = jnp.sum(jnp.where(eye, jnp.broadcast_to(wrow, (IDX_HEADS, IDX_HEADS)), 0.0), axis=1, keepdims=True)
    lg_new = jnp.sum(qi * kin, axis=1, keepdims=True)
    s_new = jnp.sum(jnp.maximum(lg_new, 0.0) * wcol, axis=0, keepdims=True)
    sn_out[...] = jnp.broadcast_to(s_new, sn_out.shape)
    for q in range(nq):
        kq = bik[slot, :, q * ch:(q + 1) * ch].astype(BF16)
        lg = jnp.maximum(_dot(qib, kq), 0.0)
        sc_out[q:q + 1, :] = jnp.sum(lg * wcol, axis=0, keepdims=True)


def _dsa_select_kernel(sc_ref, sn_ref, vstar_ref, jcut_ref, sct_s, extra_s, *, n_sel, past):
    db = sc_ref.shape[0]
    tc = Q_BLOCK
    nchunk = past // tc
    for c in range(nchunk):
        sct_s[c * tc:(c + 1) * tc, :] = sc_ref[:, c * tc:(c + 1) * tc].T
    new_row = sn_ref[...].T[0:1, :]
    extra_s[...] = jnp.where(_iota(extra_s.shape, 0) == 0, jnp.broadcast_to(new_row, extra_s.shape), NEG)
    krow = _iota((tc, db), 0)
    erow = _iota(extra_s.shape, 0)

    def sweep(fn, init):
        def body(c, carry):
            return fn(sct_s[pl.ds(pl.multiple_of(c * tc, tc), tc), :], c * tc + krow, carry)
        return fn(extra_s[...], past + erow, lax.fori_loop(0, nchunk, body, init))

    def count_gt(x):
        return sweep(lambda s, c, a: a + jnp.sum(jnp.where(s > x, 1.0, 0.0), axis=0, keepdims=True),
                     jnp.zeros((1, db), F32))

    small = jnp.zeros((1, db), jnp.int32) > 0
    vstar, jcut = _select_topk(sweep, count_gt, n_sel, small, db, past + 1)
    vstar_ref[...] = vstar
    jcut_ref[...] = jcut


def _dsa_select(scores, snew, past):
    db = scores.shape[0]
    n_sel = float(min(IDX_TOPK, (past + 1) // 4))
    full = lambda a: pl.BlockSpec(a.shape, lambda i: (0, 0))
    return pl.pallas_call(
        functools.partial(_dsa_select_kernel, n_sel=n_sel, past=past),
        grid=(1,),
        in_specs=[full(scores), full(snew)],
        out_specs=[pl.BlockSpec((1, db), lambda i: (0, 0)), pl.BlockSpec((1, db), lambda i: (0, 0))],
        out_shape=[jax.ShapeDtypeStruct((1, db), F32), jax.ShapeDtypeStruct((1, db), jnp.int32)],
        scratch_shapes=[pltpu.VMEM((past, db), F32), pltpu.VMEM((8, db), F32)],
        compiler_params=_params(1),
        name="dsa_select",
    )(scores, snew)


def _dsa_sample_attend(sc, s_new, vstar, jcut, qbd, kan, van, bak, bav, slot, past):
    nq = N_KEY_CHUNKS
    ch = past // nq
    kidx = _iota((nq, ch), 0) * ch + _iota((nq, ch), 1)
    bias = jnp.where(sc > vstar, 0.0, jnp.where(sc == vstar, jnp.where(kidx < jcut, 0.0, NEG), NEG))
    bias_new = jnp.where(s_new > vstar, 0.0, jnp.where(s_new == vstar, jnp.where(past < jcut, 0.0, NEG), NEG))

    qs = qbd * (HEAD_DIM ** -0.5)
    sn = jnp.sum(qs * kan, axis=1, keepdims=True) + bias_new
    qsb = qs.astype(BF16)
    tiles, vts = [], []
    for q in range(nq):
        kq = bak[slot, :, q * ch:(q + 1) * ch].astype(BF16)
        tiles.append(_dot(qsb, kq) + bias[q:q + 1, :])
        vts.append(bav[slot, :, q * ch:(q + 1) * ch].astype(BF16))
    return _sample_softmax(tiles, sn, vts, van)


def _moba_sample_row(qbd, kbn, vbn, bbk, bbv, slot, n_blk, past):
    nq = N_KEY_CHUNKS
    ch = past // nq
    nblk = past // MOBA_BLOCK
    bpc = ch // MOBA_BLOCK
    qs = qbd * (HEAD_DIM ** -0.5)
    qsb = qs.astype(BF16)
    sn = jnp.sum(qs * kbn, axis=1, keepdims=True)
    bcol = _iota((B_HEADS, nblk), 1).astype(F32)

    raw, vts = [], []
    gate = jnp.zeros((B_HEADS, nblk), F32)
    for q in range(nq):
        s = _dot(qsb, bbk[slot, :, q * ch:(q + 1) * ch].astype(BF16))
        raw.append(s)
        vts.append(bbv[slot, :, q * ch:(q + 1) * ch].astype(BF16))
        for k in range(bpc):
            g = jnp.sum(s[:, k * MOBA_BLOCK:(k + 1) * MOBA_BLOCK], axis=1, keepdims=True) * (1.0 / MOBA_BLOCK)
            gate = jnp.where(bcol == float(q * bpc + k), g, gate)
    rank = jnp.zeros((B_HEADS, nblk), F32)
    for m in range(nblk):
        gm = gate[:, m:m + 1]
        rank = rank + jnp.where(gm > gate, 1.0, jnp.where(gm == gate, jnp.where(bcol > float(m), 1.0, 0.0), 0.0))
    selm = jnp.where(rank < n_blk, 1.0, 0.0)
    tiles = []
    for q in range(nq):
        bias = jnp.concatenate(
            [jnp.broadcast_to(jnp.where(selm[:, q * bpc + k:q * bpc + k + 1] > 0.0, 0.0, NEG),
                              (B_HEADS, MOBA_BLOCK)) for k in range(bpc)], axis=1)
        tiles.append(raw[q] + bias)
    return _sample_softmax(tiles, sn, vts, vbn)


def _rider(page_table, small_inputs, caches, batch, nq):
    db, n_pages = page_table.shape
    page = caches[0].shape[2]
    past = n_pages * page
    n_steps = batch * nq
    assert past % (N_KEY_CHUNKS * MOBA_BLOCK) == 0 and page % LANE == 0 and db % n_steps == 0
    rps = db // n_steps
    per_step = lambda a: pl.BlockSpec((rps,) + a.shape[1:], lambda b, i, pt: (b * nq + i, 0, 0))
    return dict(
        in_specs=[per_step(a) for a in small_inputs] + [pl.BlockSpec(memory_space=pl.ANY)] * len(caches),
        out_spec=pl.BlockSpec((rps, 8, LANE), lambda b, i, pt: (b * nq + i, 0, 0)),
        out_shape=jax.ShapeDtypeStruct((db, 8, LANE), F32),
        scratch=[pltpu.VMEM((2, LANE, past), F32) for _ in caches] + [pltpu.SemaphoreType.DMA((len(caches), 2))],
        static=dict(n_rows=db, rows_per_step=rps, n_pages=n_pages, past=past),
    )


def _merge_kernel(x_ref, oa_ref, ob_ref, sga_ref, sgb_ref, wpa_ref, wpb_ref, wo_ref, y_ref):
    pa = _dot(oa_ref[...].astype(BF16), wpa_ref[...])
    pb = _dot(ob_ref[...].astype(BF16), wpb_ref[...])
    merged = sga_ref[...] * pa + sgb_ref[...] * pb
    y_ref[...] = x_ref[...] + _dot(merged.astype(BF16), wo_ref[...])


def _merge(x2d, oa, ob, sga, sgb, wpa, wpb, wo):
    n, d = x2d.shape
    tm = min(ROW_BLOCK, n)
    row = lambda c: pl.BlockSpec((tm, c), lambda i: (i, 0))
    const = lambda a: pl.BlockSpec(a.shape, lambda i: (0, 0))
    return pl.pallas_call(
        _merge_kernel,
        grid=(n // tm,),
        in_specs=[row(d), row(A_W), row(B_W), row(d), row(d), const(wpa), const(wpb), const(wo)],
        out_specs=row(d),
        out_shape=jax.ShapeDtypeStruct((n, d), F32),
        compiler_params=_params(1),
        name="merge",
    )(x2d, oa, ob, sga, sgb, wpa, wpb, wo)


FFN_COL_CHUNK = 1408


def _ffn_prompt_kernel(x_ref, oa_ref, ob_ref, sga_ref, sgb_ref, wpa_ref, wpb_ref, wo_ref,
                       n2_ref, wup_ref, cw_ref, cb_ref, wdn_ref, prev_ref, y_ref, tail_ref, ext_s, *, d_ff):
    tm = x_ref.shape[0]
    i = pl.program_id(1)

    @pl.when(i == 0)
    def _():
        ext_s[6:8, :] = prev_ref[...]

    pa = _dot(oa_ref[...].astype(BF16), wpa_ref[...])
    pb = _dot(ob_ref[...].astype(BF16), wpb_ref[...])
    merged = sga_ref[...] * pa + sgb_ref[...] * pb
    x = x_ref[...] + _dot(merged.astype(BF16), wo_ref[...])
    xn = _rms(x, n2_ref[...]).astype(BF16)
    fc = FFN_COL_CHUNK
    for c in range(2 * d_ff // fc):
        ext_s[8:8 + tm, c * fc:(c + 1) * fc] = _dot(xn, wup_ref[:, c * fc:(c + 1) * fc])

    def conv(lo):
        cols = slice(lo, lo + fc)
        out = cb_ref[:, cols] + ext_s[6:6 + tm, cols] * cw_ref[0:1, cols]
        out = out + ext_s[7:7 + tm, cols] * cw_ref[1:2, cols]
        return out + ext_s[8:8 + tm, cols] * cw_ref[2:3, cols]

    y = x
    for j in range(d_ff // fc):
        a = conv(j * fc)
        g = conv(d_ff + j * fc)
        act = (a / (1.0 + jnp.exp(-a)) * g).astype(BF16)
        y = y + _dot(act, wdn_ref[j * fc:(j + 1) * fc, :])
    y_ref[...] = y
    tail = ext_s[tm + 6:tm + 8, :]
    tail_ref[...] = tail
    ext_s[6:8, :] = tail


def _merge_ffn_prompt(x2d, oa, ob, sga, sgb, wpa, wpb, wo, n2, wup, cw, cb, wdn, prev, batch, t):
    n, d = x2d.shape
    d_ff = wdn.shape[0]
    assert d_ff % FFN_COL_CHUNK == 0
    tm = ROW_BLOCK
    nt = t // tm
    row = lambda c: pl.BlockSpec((tm, c), lambda b, i: (b * nt + i, 0))
    const = lambda a: pl.BlockSpec(a.shape, lambda b, i: (0, 0), pipeline_mode=pl.Buffered(1))
    per_b = pl.BlockSpec((None, 2, 2 * d_ff), lambda b, i: (b, 0, 0))
    return pl.pallas_call(
        functools.partial(_ffn_prompt_kernel, d_ff=d_ff),
        grid=(batch, nt),
        in_specs=[row(d), row(A_W), row(B_W), row(d), row(d), const(wpa), const(wpb), const(wo),
                  const(n2), const(wup), const(cw), const(cb), const(wdn), per_b],
        out_specs=[row(d), per_b],
        out_shape=[jax.ShapeDtypeStruct((n, d), F32), jax.ShapeDtypeStruct((batch, 2, 2 * d_ff), F32)],
        scratch_shapes=[pltpu.VMEM((tm + 8, 2 * d_ff), F32)],
        compiler_params=_params(2),
        name="merge_ffn_prompt",
    )(x2d, oa, ob, sga, sgb, wpa, wpb, wo, n2, wup, cw, cb, wdn, prev)


def _ffn_sample_kernel(x_ref, n2_ref, wup_ref, cw_ref, cb_ref, wdn_ref, s0_ref, s1_ref, y_ref, up_ref, *, d_ff):
    x = x_ref[...]
    xn = _rms(x, n2_ref[...]).astype(BF16)
    fc = FFN_COL_CHUNK
    for c in range(2 * d_ff // fc):
        up_ref[:, c * fc:(c + 1) * fc] = _dot(xn, wup_ref[:, c * fc:(c + 1) * fc])

    def conv(lo):
        cols = slice(lo, lo + fc)
        out = cb_ref[:, cols] + s0_ref[:, cols] * cw_ref[0:1, cols]
        out = out + s1_ref[:, cols] * cw_ref[1:2, cols]
        return out + up_ref[:, cols] * cw_ref[2:3, cols]

    y = x
    for j in range(d_ff // fc):
        a = conv(j * fc)
        g = conv(d_ff + j * fc)
        act = (a / (1.0 + jnp.exp(-a)) * g).astype(BF16)
        y = y + _dot(act, wdn_ref[j * fc:(j + 1) * fc, :])
    y_ref[...] = y


def _ffn_sample(x2d, n2, wup, cw, cb, wdn, s0, s1):
    n, d = x2d.shape
    d_ff = wdn.shape[0]
    full = lambda a: pl.BlockSpec(a.shape, lambda i: (0, 0), pipeline_mode=pl.Buffered(1))
    args = (x2d, n2, wup, cw, cb, wdn, s0, s1)
    return pl.pallas_call(
        functools.partial(_ffn_sample_kernel, d_ff=d_ff),
        grid=(1,),
        in_specs=[full(a) for a in args],
        out_specs=[pl.BlockSpec((n, d), lambda i: (0, 0)), pl.BlockSpec((n, 2 * d_ff), lambda i: (0, 0))],
        out_shape=[jax.ShapeDtypeStruct((n, d), F32), jax.ShapeDtypeStruct((n, 2 * d_ff), F32)],
        compiler_params=_params(1),
        name="ffn_sample",
    )(*args)


def _rope_tables(pos):
    half = HEAD_DIM // 2
    inv = ROPE_THETA ** (-jnp.arange(half, dtype=F32) / half)
    ang = pos.astype(F32)[:, None] * inv[None, :]
    cos = jnp.cos(ang)
    sin = jnp.sin(ang)
    return jnp.tile(cos, (1, 4)), jnp.tile(jnp.concatenate([-sin, sin], axis=1), (1, 2))


def _layout_w_in(w_in, d_model):
    splits = (A_W, A_KV_W, A_KV_W, I_W, IDX_DIM, IDX_HEADS, B_W, B_KV_W, B_KV_W, d_model, d_model)
    offs = [0]
    for s in splits:
        offs.append(offs[-1] + s)
    p = [w_in[:, offs[k]:offs[k + 1]] for k in range(len(splits))]
    qa, ka, va, qi, ki, wi, qb, kb, vb, ga, gb = p
    pad = jnp.zeros((w_in.shape[0], LANE - IDX_DIM - IDX_HEADS), w_in.dtype)
    return jnp.concatenate([qa, ka, va, qi, qb, kb, vb, ga, gb, ki, wi, pad], axis=1).astype(BF16)


def _layout_gains(q_norm_a, k_norm_a, k_norm_idx, q_norm_b, k_norm_b, d_model):
    one = lambda n: jnp.ones((n,), F32)
    return jnp.concatenate([
        jnp.tile(q_norm_a, A_HEADS), jnp.tile(k_norm_a, A_KV_HEADS), one(A_KV_W), one(I_W),
        jnp.tile(q_norm_b, B_HEADS), jnp.tile(k_norm_b, B_KV_HEADS), one(B_KV_W),
        one(2 * d_model), k_norm_idx, one(LANE - IDX_DIM)])[None, :]


def _block_diag_q(q, heads_per_kv):
    z = jnp.zeros_like(q)
    low = jnp.concatenate([q, z], axis=-1)
    high = jnp.concatenate([z, q], axis=-1)
    is_low = (jnp.arange(q.shape[1]) // heads_per_kv == 0)[None, :, None]
    return jnp.where(is_low, low, high)


def _pick_kv(o, heads_per_kv):
    n = o.shape[0]
    return jnp.concatenate([o[:, :heads_per_kv, :HEAD_DIM].reshape(n, -1),
                            o[:, heads_per_kv:, HEAD_DIM:].reshape(n, -1)], axis=1)


def _sample_operands(qa_s, ka_s, va_s, qi_s, qb_s, kb_s, vb_s, kw_s, c_ik, c_ak, c_av, c_bk, c_bv):
    db = qa_s.shape[0]
    n_pool, page = c_ak.shape[0], c_ak.shape[1]
    hpk_a = A_HEADS // A_KV_HEADS
    hpk_b = B_HEADS // B_KV_HEADS
    qi_pad = jnp.pad(qi_s.reshape(db, IDX_HEADS, IDX_DIM), ((0, 0), (0, 0), (0, LANE - IDX_DIM)))
    wi_s = kw_s[:, IDX_DIM:IDX_DIM + IDX_HEADS].reshape(db, 1, IDX_HEADS)
    qa_bd = _block_diag_q(qa_s.reshape(db, A_HEADS, HEAD_DIM), hpk_a)
    qb_bd = _block_diag_q(qb_s.reshape(db, B_HEADS, HEAD_DIM), hpk_b)
    kin = jnp.where(jnp.arange(LANE) < IDX_DIM, kw_s, 0.0).reshape(db, 1, LANE)
    pages_t = lambda c: jnp.moveaxis(c, 1, -1).reshape(n_pool, -1, page)
    dsa = ([qa_bd, ka_s.reshape(db, 1, LANE), va_s.reshape(db, 1, LANE)], [pages_t(c_ak), pages_t(c_av)])
    moba = ([qb_bd, kb_s.reshape(db, 1, LANE), vb_s.reshape(db, 1, LANE), qi_pad, wi_s, kin],
            [pages_t(c_bk), pages_t(c_bv), pages_t(c_ik)])
    return dsa, moba


def kernel(x_prompt, x_sample, cache_a_k, cache_a_v, cache_idx_k, cache_b_k, cache_b_v, state_conv, page_table, norm1, w_in, q_norm_a, k_norm_a, k_norm_idx, q_norm_b, k_norm_b, w_proj_a, w_proj_b, w_out, norm2, w_up, conv_w, conv_b, w_down):
    batch, t, d = x_prompt.shape
    db, ds, _ = x_sample.shape
    depth = norm1.shape[0]
    assert depth == 1 and ds == 1
    n_pool, page = cache_a_k.shape[1], cache_a_k.shape[2]
    past = page_table.shape[1] * page
    l = 0
    w = _layout_w_in(w_in[l], d)
    gains = _layout_gains(q_norm_a[l], k_norm_a[l], k_norm_idx[l], q_norm_b[l], k_norm_b[l], d)
    wpa, wpb, wo = w_proj_a[l].astype(BF16), w_proj_b[l].astype(BF16), w_out[l].astype(BF16)
    wup, wdn = w_up[l].astype(BF16), w_down[l].astype(BF16)
    n1, n2 = norm1[l][None, :], norm2[l][None, :]
    cw, cb = conv_w[l], conv_b[l][None, :]
    d_ff = wdn.shape[0]

    xp = x_prompt.reshape(batch * t, d)
    cos_p, sin_p = _rope_tables(jnp.arange(t, dtype=jnp.int32))
    qa, ka, _, qi, qb, kb, _, sga, sgb, kw, kat, vat, kbt, vbt, kit = _proj(xp, n1, w, gains, cos_p, sin_p, t)
    xs = x_sample.reshape(db, d)
    cos_s, sin_s = _rope_tables(jnp.full((db,), past, jnp.int32))
    (qa_s, ka_s, va_s, qi_s, qb_s, kb_s, vb_s, sga_s, sgb_s, kw_s,
     kat_s, vat_s, kbt_s, vbt_s, kit_s) = _proj(xs, n1, w, gains, cos_s, sin_s, db)

    dsa_s, moba_s = _sample_operands(qa_s, ka_s, va_s, qi_s, qb_s, kb_s, vb_s, kw_s,
                                     cache_idx_k[l], cache_a_k[l], cache_a_v[l], cache_b_k[l], cache_b_v[l])
    ob, ob_s, sc_s, sn_s = _moba_branch(qb, kb, vbt, batch, t, page_table, *moba_s)
    vstar_s, jcut_s = _dsa_select(sc_s.reshape(db, past), sn_s.reshape(db, LANE), past)
    per_row = lambda a: jnp.broadcast_to(a.reshape(db, 1, 1), (db, 1, LANE))
    oa, oa_s = _dsa_branch(qi, qa, kw, ka, vat, batch, t, page_table,
                           [sc_s, sn_s, per_row(vstar_s), per_row(jcut_s)] + dsa_s[0], dsa_s[1])
    oa_s = _pick_kv(oa_s, A_HEADS // A_KV_HEADS)
    ob_s = _pick_kv(ob_s, B_HEADS // B_KV_HEADS)

    yp, p_conv = _merge_ffn_prompt(xp, oa, ob, sga, sgb, wpa, wpb, wo, n2, wup, cw, cb, wdn,
                                   jnp.zeros((batch, 2, 2 * d_ff), F32), batch, t)
    x1_s = _merge(xs, oa_s, ob_s, sga_s, sgb_s, wpa, wpb, wo)
    ys, up_s = _ffn_sample(x1_s, n2, wup, cw, cb, wdn, state_conv[l, :, 0], state_conv[l, :, 1])

    def rows5(a, n, s, h):
        a = a.reshape(a.shape[0], h, HEAD_DIM, a.shape[2])
        return jnp.transpose(a, (0, 3, 1, 2)).reshape(1, n, s, h, HEAD_DIM)

    def rows4(a, n, s):
        return jnp.transpose(a, (0, 2, 1)).reshape(1, n, s, IDX_DIM)

    return (
        yp.reshape(batch, t, d), ys.reshape(db, 1, d),
        rows5(kat, batch, t, A_KV_HEADS), rows5(vat, batch, t, A_KV_HEADS), rows4(kit, batch, t),
        rows5(kbt, batch, t, B_KV_HEADS), rows5(vbt, batch, t, B_KV_HEADS),
        p_conv[None],
        rows5(kat_s, db, 1, A_KV_HEADS), rows5(vat_s, db, 1, A_KV_HEADS), rows4(kit_s, db, 1),
        rows5(kbt_s, db, 1, B_KV_HEADS), rows5(vbt_s, db, 1, B_KV_HEADS),
        jnp.stack([state_conv[l, :, 1], up_s], axis=1)[None],
    )
```

```python
import functools

import jax
import jax.numpy as jnp
from jax import lax
from jax.experimental import pallas as pl
from jax.experimental.pallas import tpu as pltpu

HEAD_DIM = 64
A_HEADS = 8
A_KV_HEADS = 2
IDX_HEADS = 8
IDX_DIM = 64
IDX_TOPK = 256
B_HEADS = 8
B_KV_HEADS = 2
MOBA_BLOCK = 256
MOBA_TOPK = 3
ROPE_THETA = 10000.0
EPS = 1e-6
NEG = -1e30
BIG = 3e38

LANE = 128
Q_BLOCK = 256
ROW_BLOCK = 256
VMEM_LIMIT = 56 * 1024 * 1024
BISECT_STEPS = 13
SAMPLE_SPLIT = 16
SAMPLE_SPLIT_ROUNDS = 4

F32 = jnp.float32
BF16 = jnp.bfloat16

A_W = A_HEADS * HEAD_DIM
A_KV_W = A_KV_HEADS * HEAD_DIM
I_W = IDX_HEADS * IDX_DIM
B_W = B_HEADS * HEAD_DIM
B_KV_W = B_KV_HEADS * HEAD_DIM
assert A_W == 512 and I_W == 512 and B_W == 512 and A_KV_W == LANE and B_KV_W == LANE


def _params(n_grid):
    return pltpu.CompilerParams(dimension_semantics=("arbitrary",) * n_grid,
                                vmem_limit_bytes=VMEM_LIMIT)


def _dot(a, b):
    return jnp.dot(a, b, preferred_element_type=F32)


def _dot_nt(a, b):
    return lax.dot_general(a, b, (((1,), (1,)), ((), ())), preferred_element_type=F32)


def _iota(shape, dim):
    return lax.broadcasted_iota(jnp.int32, shape, dim)


def _rms(x, g):
    return x * lax.rsqrt(jnp.mean(x * x, axis=-1, keepdims=True) + EPS) * g


def _proj_kernel(x_ref, n1_ref, w_ref, g_ref, cos_ref, sin_ref,
                 qa_ref, ka_ref, va_ref, qi_ref, qb_ref, kb_ref, vb_ref, sga_ref, sgb_ref, kw_ref,
                 kat_ref, vat_ref, kbt_ref, vbt_ref, kit_ref, *, d_model):
    h = _rms(x_ref[...], n1_ref[...]).astype(BF16)
    gr = lax.shift_right_logical(_iota((2 * LANE, 2 * LANE), 0), 6)
    gc = lax.shift_right_logical(_iota((2 * LANE, 2 * LANE), 1), 6)
    gsum = jnp.where(gr == gc, 1.0, 0.0).astype(BF16)
    cos1 = cos_ref[...]
    sin1 = sin_ref[...]

    def seg(off, width):
        return _dot(h, w_ref[:, off:off + width])

    def head_norm(x, off):
        width = x.shape[1]
        ss = _dot((x * x).astype(BF16), gsum[:width, :width])
        return x * lax.rsqrt(ss * (1.0 / HEAD_DIM) + EPS) * g_ref[:, off:off + width]

    def rope(x):
        width = x.shape[1]
        rep = width // LANE
        cs = jnp.concatenate([cos1] * rep, axis=1) if rep > 1 else cos1
        sn = jnp.concatenate([sin1] * rep, axis=1) if rep > 1 else sin1
        hi = (_iota(x.shape, 1) & (HEAD_DIM // 2)) != 0
        swapped = jnp.where(hi, pltpu.roll(x, HEAD_DIM // 2, axis=1),
                            pltpu.roll(x, width - HEAD_DIM // 2, axis=1))
        return x * cs + swapped * sn

    o = 0
    for half in range(2):
        x = seg(o + half * 256, 256)
        qa_ref[:, half * 256:(half + 1) * 256] = rope(head_norm(x, o + half * 256))
    o = A_W
    x = seg(o, 256)
    xn = rope(head_norm(x, o))
    ka_ref[...] = xn[:, :LANE]
    va_ref[...] = x[:, LANE:]
    kat_ref[...] = xn[:, :LANE].T
    vat_ref[...] = x[:, LANE:].T
    o = A_W + 2 * LANE
    for half in range(2):
        qi_ref[:, half * 256:(half + 1) * 256] = rope(seg(o + half * 256, 256))
    o = A_W + 2 * LANE + I_W
    for half in range(2):
        x = seg(o + half * 256, 256)
        qb_ref[:, half * 256:(half + 1) * 256] = rope(head_norm(x, o + half * 256))
    o = A_W + 2 * LANE + I_W + B_W
    x = seg(o, 256)
    xn = rope(head_norm(x, o))
    kb_ref[...] = xn[:, :LANE]
    vb_ref[...] = x[:, LANE:]
    kbt_ref[...] = xn[:, :LANE].T
    vbt_ref[...] = x[:, LANE:].T
    o = A_W + 2 * LANE + I_W + B_W + 2 * LANE
    for part in range(d_model // 256):
        x = seg(o + part * 256, 256)
        sga_ref[:, part * 256:(part + 1) * 256] = (1.0 / (1.0 + jnp.exp(-x))).astype(sga_ref.dtype)
    o += d_model
    for part in range(d_model // 256):
        x = seg(o + part * 256, 256)
        sgb_ref[:, part * 256:(part + 1) * 256] = (1.0 / (1.0 + jnp.exp(-x))).astype(sgb_ref.dtype)
    o += d_model
    x = seg(o, LANE)
    xn = rope(head_norm(x, o))
    kw = jnp.where(_iota(x.shape, 1) < IDX_DIM, xn, x)
    kw_ref[...] = kw
    kit_ref[...] = kw.T[:IDX_DIM]


def _proj(x2d, n1, w, gains, cos, sin, rows_per_seq):
    n, d = x2d.shape
    tm = min(ROW_BLOCK, rows_per_seq)
    nt = rows_per_seq // tm
    nseq = n // rows_per_seq
    nw = w.shape[1]
    row = lambda c: pl.BlockSpec((tm, c), lambda i: (i, 0))
    const = lambda shape: pl.BlockSpec(shape, lambda i: (0, 0))
    tab = pl.BlockSpec((tm, LANE), lambda i: (i % nt, 0))
    chan = lambda c: pl.BlockSpec((None, c, tm), lambda i: (i // nt, 0, i % nt))
    rows = [A_W, LANE, LANE, I_W, B_W, LANE, LANE, d, d, LANE]
    chans = [LANE, LANE, LANE, LANE, IDX_DIM]
    return pl.pallas_call(
        functools.partial(_proj_kernel, d_model=d),
        grid=(n // tm,),
        in_specs=[row(d), const((1, d)), const((d, nw)), const((1, nw)), tab, tab],
        out_specs=[row(c) for c in rows] + [chan(c) for c in chans],
        out_shape=[jax.ShapeDtypeStruct((n, c), BF16 if k in (7, 8) else F32) for k, c in enumerate(rows)]
        + [jax.ShapeDtypeStruct((nseq, c, rows_per_seq), F32) for c in chans],
        compiler_params=_params(1),
        name="proj",
    )(x2d, n1, w, gains, cos, sin)


def _head_operand(q_ref, h, want_low, scale):
    ch = q_ref[:, LANE * (h // 2):LANE * (h // 2) + LANE]
    if (h % 2 == 0) != want_low:
        ch = pltpu.roll(ch, HEAD_DIM, axis=1)
    low = _iota(ch.shape, 1) < HEAD_DIM
    keep = low if want_low else jnp.logical_not(low)
    return (jnp.where(keep, ch, 0.0) * scale).astype(BF16)


def _fill_kv(k_ref, vt_ref, kb_s, vt_s, nchunk, tc):
    kb_s[...] = k_ref[...].astype(BF16)
    extra = jnp.where(_iota((V_ROWS - HEAD_DIM, tc), 0) == 0, 1.0, 0.0)
    for c in range(nchunk):
        for kv in range(2):
            vt = vt_ref[kv * HEAD_DIM:(kv + 1) * HEAD_DIM, c * tc:(c + 1) * tc]
            vt_s[kv, c] = jnp.concatenate([vt, extra], axis=0).astype(BF16)


def _attend_chunk(kc, c, q_s, vt_s, st, n_heads, heads_per_kv, tile_bias_of=None, row_bias_of=None):
    s_s, p_s, mc_s, a_s, m_s, acc_s = st
    for h in range(n_heads):
        s = _dot_nt(kc, q_s[h])
        if tile_bias_of is not None:
            s = s + tile_bias_of(h)
        s_s[h] = s
        mc_s[h] = jnp.max(s, axis=0, keepdims=True)
    for h in range(n_heads):
        m_old = m_s[h]
        if row_bias_of is None:
            m_new = jnp.maximum(m_old, mc_s[h])
            shift = m_new
        else:
            rb = row_bias_of(h)
            m_new = jnp.maximum(m_old, mc_s[h] + rb)
            shift = m_new - rb
        p_s[h] = jnp.exp2(s_s[h] - shift).astype(BF16)
        a_s[h] = jnp.exp2(m_old - m_new)
        m_s[h] = m_new
    for h in range(n_heads):
        acc_s[h] = a_s[h] * acc_s[h] + _dot(vt_s[h // heads_per_kv, c], p_s[h])


V_ROWS = HEAD_DIM + 16
LOG2E = 1.4426950408889634


def _attention_scratch(n_heads, tq):
    row = pltpu.VMEM((n_heads, 1, tq), F32)
    return [pltpu.VMEM((n_heads, tq, tq), F32), pltpu.VMEM((n_heads, tq, tq), BF16), row, row, row,
            pltpu.VMEM((n_heads, V_ROWS, tq), F32)]


def _init_softmax(st):
    m_s, acc_s = st[-2], st[-1]
    m_s[...] = jnp.full(m_s.shape, NEG, F32)
    acc_s[...] = jnp.zeros(acc_s.shape, F32)


def _write_heads(o_ref, st, n_heads):
    acc_s = st[-1]
    for j in range(n_heads // 2):
        pair = []
        for h in (2 * j, 2 * j + 1):
            acc = acc_s[h]
            pair.append(acc[:HEAD_DIM] / acc[HEAD_DIM:HEAD_DIM + 1])
        o_ref[:, LANE * j:LANE * (j + 1)] = jnp.concatenate(pair, axis=0).T.astype(o_ref.dtype)


def _select_topk(sweep, count_gt, n_sel, small, tq, t_total, score_range=None):
    row = lambda v: jnp.full((1, tq), v, F32)
    if score_range is not None:
        rowmax, rowmin = score_range
    else:
        rowmax = sweep(lambda s, c, a: jnp.maximum(a, jnp.max(s, axis=0, keepdims=True)), row(NEG))
        rowmin = sweep(lambda s, c, a: jnp.minimum(
            a, jnp.min(jnp.where(s > 0.5 * NEG, s, BIG), axis=0, keepdims=True)), row(BIG))

    def bisect(_, lh):
        lo, hi = lh
        mid = 0.5 * (lo + hi)
        ge = count_gt(mid) >= n_sel
        return jnp.where(ge, mid, lo), jnp.where(ge, hi, mid)

    lo, _ = lax.fori_loop(0, BISECT_STEPS, bisect, (rowmin, rowmax))
    u0 = sweep(lambda s, c, a: jnp.minimum(
        a, jnp.min(jnp.where(s >= lo, s, BIG), axis=0, keepdims=True)), row(BIG))

    def walk_cond(st):
        return st[2] == 0

    def walk(st):
        u = st[0]

        def f(s, c, carry):
            cnt, nxt = carry
            gt = s > u
            cnt = cnt + jnp.sum(jnp.where(gt, 1.0, 0.0), axis=0, keepdims=True)
            nxt = jnp.minimum(nxt, jnp.min(jnp.where(gt, s, BIG), axis=0, keepdims=True))
            return cnt, nxt

        cnt, nxt = sweep(f, (row(0.0), row(BIG)))
        done = jnp.logical_or(cnt < n_sel, small)
        all_done = jnp.min(jnp.where(done, 1.0, 0.0)).astype(jnp.int32)
        return jnp.where(done, u, nxt), cnt, all_done

    u, cgt, _ = lax.while_loop(walk_cond, walk, (u0, row(0.0), jnp.int32(0)))
    vstar = jnp.where(small, NEG, u)
    need = jnp.where(small, 0.0, n_sel - cgt)
    n_eq = sweep(lambda s, c, a: a + jnp.sum(jnp.where(s == vstar, 1.0, 0.0), axis=0, keepdims=True), row(0.0))
    excess = jnp.max(jnp.where(jnp.logical_and(n_eq > need, jnp.logical_not(small)), 1.0, 0.0))

    n_jsteps = jnp.where(excess > 0.0, t_total.bit_length() + 1, 0).astype(jnp.int32)

    def jstep(_, jj):
        jlo, jhi = jj
        mid = lax.shift_right_logical(jlo + jhi, 1)
        e = sweep(lambda s, c, a: a + jnp.sum(
            jnp.where(s == vstar, jnp.where(c < mid, 1.0, 0.0), 0.0), axis=0, keepdims=True), row(0.0))
        ge = e >= need
        return jnp.where(ge, jlo, mid), jnp.where(ge, mid, jhi)

    zero_i = jnp.zeros((1, tq), jnp.int32)
    _, jcut = lax.fori_loop(0, n_jsteps, jstep, (zero_i, zero_i + t_total))
    return vstar, jnp.where(small, 0, jcut)


def _dsa_prompt_kernel(pt_ref, qi_ref, qa_ref, kwq_ref, kwk_ref, ka_ref, vat_ref,
                       ssc_ref, ssn_ref, svs_ref, sjc_ref, sqbd_ref, skan_ref, svan_ref, cak, cav,
                       o_ref, os_ref,
                       kib_s, kab_s, vt_s, qim_s, qam_s, sc_s, s_s, p_s, mc_s, a_s, m_s, acc_s,
                       bak, bav, sem, *, n_sel, t_total, rider):
    st = (s_s, p_s, mc_s, a_s, m_s, acc_s)
    tq = qi_ref.shape[0]
    tc = tq
    nchunk = t_total // tc
    i = pl.program_id(1)

    def sample_compute(r, slot):
        os_ref[r] = _dsa_sample_attend(ssc_ref[r], ssn_ref[r][:, 0:1], svs_ref[r][:, 0:1], sjc_ref[r][:, 0:1],
                                       sqbd_ref[r], skan_ref[r], svan_ref[r], bak, bav, slot, rider["past"])

    rps = rider["max_rows"]
    run_row = _sample_row_runner(pt_ref, rider, (cak, cav), (bak, bav), sem, sample_compute)
    for k in range((rps + 1) // 2):
        run_row(k)

    @pl.when(i == 0)
    def _():
        kib_s[...] = kwk_ref[...].astype(BF16)
        _fill_kv(ka_ref, vat_ref, kab_s, vt_s, nchunk, tc)

    heads_per_kv = A_HEADS // A_KV_HEADS
    for h in range(IDX_HEADS):
        qim_s[h] = _head_operand(qi_ref, h, True, 1.0)
    for h in range(A_HEADS):
        qam_s[h] = _head_operand(qa_ref, h, (h // heads_per_kv) == 0, HEAD_DIM ** -0.5 * LOG2E)
    w8 = kwq_ref[...].T[IDX_DIM:IDX_DIM + IDX_HEADS, :] * (IDX_DIM ** -0.5 * IDX_HEADS ** -0.5)

    krow = _iota((tc, tq), 0)
    qcol = _iota((tc, tq), 1)

    def chunk(c):
        return pl.ds(pl.multiple_of(c * tc, tc), tc)

    def score_chunk(c, diag, rng):
        kc = kib_s[chunk(c), :]
        a = jnp.zeros((tc, tq), F32)
        for h in range(IDX_HEADS):
            a = a + jnp.maximum(_dot_nt(kc, qim_s[h]), 0.0) * w8[h:h + 1, :]
        lo_src = a
        if diag:
            lo_src = jnp.where(krow <= qcol, a, BIG)
            a = jnp.where(krow <= qcol, a, NEG)
        sc_s[chunk(c), :] = a
        return (jnp.maximum(rng[0], jnp.max(a, axis=0, keepdims=True)),
                jnp.minimum(rng[1], jnp.min(lo_src, axis=0, keepdims=True)))

    rng0 = (jnp.full((1, tq), NEG, F32), jnp.full((1, tq), BIG, F32))
    rowmax, rowmin = score_chunk(i, True, lax.fori_loop(0, i, lambda c, rng: score_chunk(c, False, rng), rng0))

    def sweep(fn, init):
        return lax.fori_loop(0, i + 1, lambda c, carry: fn(sc_s[chunk(c), :], c * tc + krow, carry), init)

    def count_gt(x):
        part = sweep(lambda s, c, a: a + jnp.sum(jnp.where(s > x, 1.0, 0.0).reshape(tc // 8, 8, tq), axis=0),
                     jnp.zeros((8, tq), F32))
        return jnp.sum(part, axis=0, keepdims=True)

    n_adm = i * tq + _iota((1, tq), 1) + 1
    small = n_adm <= n_sel
    vstar, jcut = _select_topk(sweep, count_gt, float(n_sel), small, tq, t_total, (rowmax, rowmin))

    def bias_body(c, carry):
        s = sc_s[chunk(c), :]
        tie = jnp.where((c * tc + krow) < jcut, 0.0, NEG)
        sc_s[chunk(c), :] = jnp.where(s > vstar, 0.0, jnp.where(s == vstar, tie, NEG))
        return carry

    lax.fori_loop(0, i + 1, bias_body, 0)

    for k in range((rps + 1) // 2, rps):
        run_row(k)

    _init_softmax(st)

    def att_body(c, carry):
        _attend_chunk(kab_s[chunk(c), :], c, qam_s, vt_s, st, A_HEADS, heads_per_kv,
                      tile_bias_of=lambda h: sc_s[chunk(c), :])
        return carry

    lax.fori_loop(0, i + 1, att_body, 0)
    _write_heads(o_ref, st, A_HEADS)


def _dsa_branch(qi, qa, kw, ka, vat, batch, t, page_table, sample_inputs, caches):
    tq = Q_BLOCK
    nq = t // tq
    n_sel = min(IDX_TOPK, t // 4)
    qblk = lambda c: pl.BlockSpec((tq, c), lambda b, i, pt: (b * nq + i, 0))
    full = pl.BlockSpec((t, LANE), lambda b, i, pt: (b, 0))
    full_t = pl.BlockSpec((None, LANE, t), lambda b, i, pt: (b, 0, 0))
    rd = _rider(page_table, sample_inputs, caches, batch, nq)
    past = rd["static"]["past"]
    grid_spec = pltpu.PrefetchScalarGridSpec(
        num_scalar_prefetch=1,
        grid=(batch, nq),
        in_specs=[qblk(I_W), qblk(A_W), qblk(LANE), full, full, full_t] + rd["in_specs"],
        out_specs=[qblk(A_W), rd["out_spec"]],
        scratch_shapes=[
            pltpu.VMEM((t, LANE), BF16), pltpu.VMEM((t, LANE), BF16),
            pltpu.VMEM((2, nq, V_ROWS, tq), BF16),
            pltpu.VMEM((IDX_HEADS, tq, LANE), BF16), pltpu.VMEM((A_HEADS, tq, LANE), BF16),
            pltpu.VMEM((t, tq), F32),
        ] + _attention_scratch(A_HEADS, tq) + rd["scratch"],
    )
    return pl.pallas_call(
        functools.partial(_dsa_prompt_kernel, n_sel=n_sel, t_total=t, rider=rd["static"]),
        grid_spec=grid_spec,
        out_shape=[jax.ShapeDtypeStruct((batch * t, A_W), BF16), rd["out_shape"]],
        compiler_params=_params(2),
        name="dsa",
    )(page_table, qi, qa, kw, kw, ka, vat, *sample_inputs, *caches)


def _moba_prompt_kernel(pt_ref, qb_ref, kb_ref, vbt_ref, sqbd_ref, skbn_ref, svbn_ref, sqi_ref, swi_ref, skin_ref,
                        cbk, cbv, cik, o_ref, os_ref, osc_ref, osn_ref,
                        kbb_s, vt_s, kbar_s, qbm_s, selb_s, s_s, p_s, mc_s, a_s, m_s, acc_s,
                        bbk, bbv, bik, sem, *, n_blk, t_total, n_blk_s, rider):
    st = (s_s, p_s, mc_s, a_s, m_s, acc_s)
    tq = qb_ref.shape[0]
    tc = tq
    nb = t_total // tc
    nbp = kbar_s.shape[0]
    i = pl.program_id(1)
    heads_per_kv = B_HEADS // B_KV_HEADS

    def sample_compute(r, slot):
        os_ref[r] = _moba_sample_row(sqbd_ref[r], skbn_ref[r], svbn_ref[r], bbk, bbv, slot, n_blk_s, rider["past"])
        _dsa_sample_scores(sqi_ref[r], swi_ref[r], skin_ref[r], bik, slot, osc_ref.at[r], osn_ref.at[r],
                           rider["past"])

    def zero_pad_rows():
        bik[...] = jnp.zeros(bik.shape, F32)

    rps = rider["max_rows"]
    run_row = _sample_row_runner(pt_ref, rider, (cbk, cbv, cik), (bbk, bbv, bik), sem, sample_compute,
                                 zero_pad_rows)
    for k in range((rps + 1) // 2):
        run_row(k)

    @pl.when(i == 0)
    def _():
        _fill_kv(kb_ref, vbt_ref, kbb_s, vt_s, nb, tc)
        kbar_s[...] = jnp.zeros(kbar_s.shape, F32)
        for n in range(nb):
            kbar_s[n:n + 1, :] = jnp.mean(kb_ref[n * tc:(n + 1) * tc, :], axis=0, keepdims=True)

    kbar = kbar_s[...].astype(BF16)
    blk = _iota((nbp, tq), 0)
    past = blk < i
    for h in range(B_HEADS):
        qm = _head_operand(qb_ref, h, (h // heads_per_kv) == 0, HEAD_DIM ** -0.5 * LOG2E)
        qbm_s[h] = qm
        gate = jnp.where(past, _dot_nt(kbar, qm), NEG)
        rank = jnp.zeros((nbp, tq), F32)
        for m in range(nb):
            gm = gate[m:m + 1, :]
            first = jnp.where(blk > m, 1.0, 0.0)
            rank = rank + jnp.where(gm > gate, 1.0, jnp.where(gm == gate, first, 0.0))
        sel = jnp.logical_and(past, rank < n_blk)
        selb_s[h] = jnp.where(sel, 0.0, NEG)

    _init_softmax(st)
    krow = _iota((tc, tq), 0)
    qcol = _iota((tc, tq), 1)

    def chunk(c):
        return pl.ds(pl.multiple_of(c * tc, tc), tc)

    _attend_chunk(kbb_s[chunk(i), :], i, qbm_s, vt_s, st, B_HEADS, heads_per_kv,
                  tile_bias_of=lambda h: jnp.where(krow <= qcol, 0.0, NEG))

    for k in range((rps + 1) // 2, rps):
        run_row(k)

    def att_body(c, carry):
        _attend_chunk(kbb_s[chunk(c), :], c, qbm_s, vt_s, st, B_HEADS, heads_per_kv,
                      row_bias_of=lambda h: selb_s[h, pl.ds(c, 1), :])
        return carry

    lax.fori_loop(0, i, att_body, 0)
    _write_heads(o_ref, st, B_HEADS)


def _moba_branch(qb, kb, vbt, batch, t, page_table, sample_inputs, caches):
    tq = Q_BLOCK
    assert tq == MOBA_BLOCK and t % tq == 0
    nq = t // tq
    n_blk = min(MOBA_TOPK, (t - 1) // MOBA_BLOCK)
    qblk = pl.BlockSpec((tq, B_W), lambda b, i, pt: (b * nq + i, 0))
    full = pl.BlockSpec((t, LANE), lambda b, i, pt: (b, 0))
    full_t = pl.BlockSpec((None, LANE, t), lambda b, i, pt: (b, 0, 0))
    rd = _rider(page_table, sample_inputs, caches, batch, nq)
    past, db = rd["static"]["past"], rd["static"]["n_rows"]
    ch = past // N_KEY_CHUNKS
    grid_spec = pltpu.PrefetchScalarGridSpec(
        num_scalar_prefetch=1,
        grid=(batch, nq),
        in_specs=[qblk, full, full_t] + rd["in_specs"],
        out_specs=[qblk, rd["out_spec"], rd["whole"]((db, N_KEY_CHUNKS, ch)), rd["whole"]((db, 1, LANE))],
        scratch_shapes=[
            pltpu.VMEM((t, LANE), BF16),
            pltpu.VMEM((2, nq, V_ROWS, tq), BF16),
            pltpu.VMEM((max(nq, 8), LANE), F32),
            pltpu.VMEM((B_HEADS, tq, LANE), BF16),
            pltpu.VMEM((B_HEADS, max(nq, 8), tq), F32),
        ] + _attention_scratch(B_HEADS, tq) + rd["scratch"],
    )
    return pl.pallas_call(
        functools.partial(_moba_prompt_kernel, n_blk=float(n_blk), t_total=t,
                          n_blk_s=float(min(MOBA_TOPK, past // MOBA_BLOCK)), rider=rd["static"]),
        grid_spec=grid_spec,
        out_shape=[jax.ShapeDtypeStruct((batch * t, B_W), BF16), rd["out_shape"],
                   jax.ShapeDtypeStruct((db, N_KEY_CHUNKS, ch), F32), jax.ShapeDtypeStruct((db, 1, LANE), F32)],
        compiler_params=_params(2),
        name="moba",
    )(page_table, qb, kb, vbt, *sample_inputs, *caches)


def _page_copy(cache, buf, sem, k, slot, page, p):
    rows, width = cache.shape[1], cache.shape[2]
    return pltpu.make_async_copy(cache.at[page], buf.at[slot, pl.ds(0, rows), pl.ds(p * width, width)],
                                 sem.at[k, slot])


def _gather(pt_ref, b, caches, bufs, sem, slot, n_pages, start):
    for k, (cache, buf) in enumerate(zip(caches, bufs)):
        for p in range(n_pages):
            cp = _page_copy(cache, buf, sem, k, slot, pt_ref[b, p] if start else 0, p)
            if start:
                cp.start()
            else:
                cp.wait()


def _row_schedule(rows_per_seq, nq):
    tri = nq * (nq + 1) // 2
    prefix = [(rows_per_seq * i * (i + 1) + tri) // (2 * tri) for i in range(nq + 1)]
    return prefix, max(prefix[i + 1] - prefix[i] for i in range(nq))


def _sample_row_runner(pt_ref, rider, caches, bufs, sem, compute, before_first=None):
    n_rows, n_pages, rows_per_seq, nq = rider["n_rows"], rider["n_pages"], rider["rows_per_seq"], rider["nq"]
    tri = nq * (nq + 1) // 2
    i = pl.program_id(1)
    first = lax.div(rows_per_seq * i * (i + 1) + tri, 2 * tri)
    count = lax.div(rows_per_seq * (i + 1) * (i + 2) + tri, 2 * tri) - first
    base = pl.program_id(0) * rows_per_seq + first

    def run(k):
        @pl.when(k < count)
        def _():
            r = base + k
            slot = lax.rem(r, 2)

            @pl.when(r == 0)
            def _():
                if before_first is not None:
                    before_first()
                _gather(pt_ref, r, caches, bufs, sem, slot, n_pages, True)

            @pl.when(r + 1 < n_rows)
            def _():
                _gather(pt_ref, r + 1, caches, bufs, sem, 1 - slot, n_pages, True)

            _gather(pt_ref, r, caches, bufs, sem, slot, n_pages, False)
            compute(r, slot)

    return run


def _sample_softmax(tiles, s_new, vt_tiles, v_new):
    m = s_new
    for s in tiles:
        m = jnp.maximum(m, jnp.max(s, axis=1, keepdims=True))
    p_new = jnp.exp(s_new - m)
    l = p_new
    o = p_new * v_new
    for s, vt in zip(tiles, vt_tiles):
        p = jnp.exp(s - m)
        l = l + jnp.sum(p, axis=1, keepdims=True)
        o = o + _dot_nt(p.astype(BF16), vt)
    return o / l


N_KEY_CHUNKS = 8


def _dsa_sample_scores(qi, wi, kin, bik, slot, sc_out, sn_out, past):
    nq = N_KEY_CHUNKS
    ch = past // nq
    qib = qi.astype(BF16)
    wrow = wi * (IDX_DIM ** -0.5 * IDX_HEADS ** -0.5)
    eye = _iota((IDX_HEADS, IDX_HEADS), 0) == _iota((IDX_HEADS, IDX_HEADS), 1)
    wcol = jnp.sum(jnp.where(eye, jnp.broadcast_to(wrow, (IDX_HEADS, IDX_HEADS)), 0.0), axis=1, keepdims=True)
    lg_new = jnp.sum(qi * kin, axis=1, keepdims=True)
    s_new = jnp.sum(jnp.maximum(lg_new, 0.0) * wcol, axis=0, keepdims=True)
    sn_out[...] = jnp.broadcast_to(s_new, sn_out.shape)
    for q in range(nq):
        kq = bik[slot, :, q * ch:(q + 1) * ch].astype(BF16)
        lg = jnp.maximum(_dot(qib, kq), 0.0)
        sc_out[q:q + 1, :] = jnp.sum(lg * wcol, axis=0, keepdims=True)


def _dsa_select_kernel(sc_ref, sn_ref, vstar_ref, jcut_ref, sct_s, extra_s, *, n_sel, past):
    db = sc_ref.shape[0]
    tc = Q_BLOCK
    nchunk = past // tc
    for c in range(nchunk):
        sct_s[c * tc:(c + 1) * tc, :] = sc_ref[:, c * tc:(c + 1) * tc].T
    new_row = sn_ref[...].T[0:1, :]
    extra_s[...] = jnp.where(_iota(extra_s.shape, 0) == 0, jnp.broadcast_to(new_row, extra_s.shape), NEG)
    krow = _iota((tc, db), 0)
    erow = _iota(extra_s.shape, 0)

    def sweep(fn, init):
        def body(c, carry):
            return fn(sct_s[pl.ds(pl.multiple_of(c * tc, tc), tc), :], c * tc + krow, carry)
        return fn(extra_s[...], past + erow, lax.fori_loop(0, nchunk, body, init))

    def count_gt(x):
        return sweep(lambda s, c, a: a + jnp.sum(jnp.where(s > x, 1.0, 0.0), axis=0, keepdims=True),
                     jnp.zeros((1, db), F32))

    small = jnp.zeros((1, db), jnp.int32) > 0
    vstar, jcut = _select_topk(sweep, count_gt, n_sel, small, db, past + 1)
    vstar_ref[...] = vstar
    jcut_ref[...] = jcut


def _dsa_select(scores, snew, past):
    db = scores.shape[0]
    n_sel = float(min(IDX_TOPK, (past + 1) // 4))
    full = lambda a: pl.BlockSpec(a.shape, lambda i: (0, 0))
    return pl.pallas_call(
        functools.partial(_dsa_select_kernel, n_sel=n_sel, past=past),
        grid=(1,),
        in_specs=[full(scores), full(snew)],
        out_specs=[pl.BlockSpec((1, db), lambda i: (0, 0)), pl.BlockSpec((1, db), lambda i: (0, 0))],
        out_shape=[jax.ShapeDtypeStruct((1, db), F32), jax.ShapeDtypeStruct((1, db), jnp.int32)],
        scratch_shapes=[pltpu.VMEM((past, db), F32), pltpu.VMEM((8, db), F32)],
        compiler_params=_params(1),
        name="dsa_select",
    )(scores, snew)


def _dsa_sample_attend(sc, s_new, vstar, jcut, qbd, kan, van, bak, bav, slot, past):
    nq = N_KEY_CHUNKS
    ch = past // nq
    kidx = _iota((nq, ch), 0) * ch + _iota((nq, ch), 1)
    bias = jnp.where(sc > vstar, 0.0, jnp.where(sc == vstar, jnp.where(kidx < jcut, 0.0, NEG), NEG))
    bias_new = jnp.where(s_new > vstar, 0.0, jnp.where(s_new == vstar, jnp.where(past < jcut, 0.0, NEG), NEG))

    qs = qbd * (HEAD_DIM ** -0.5)
    sn = jnp.sum(qs * kan, axis=1, keepdims=True) + bias_new
    qsb = qs.astype(BF16)
    tiles, vts = [], []
    for q in range(nq):
        kq = bak[slot, :, q * ch:(q + 1) * ch].astype(BF16)
        tiles.append(_dot(qsb, kq) + bias[q:q + 1, :])
        vts.append(bav[slot, :, q * ch:(q + 1) * ch].astype(BF16))
    return _sample_softmax(tiles, sn, vts, van)


def _moba_sample_row(qbd, kbn, vbn, bbk, bbv, slot, n_blk, past):
    nq = N_KEY_CHUNKS
    ch = past // nq
    nblk = past // MOBA_BLOCK
    bpc = ch // MOBA_BLOCK
    qs = qbd * (HEAD_DIM ** -0.5)
    qsb = qs.astype(BF16)
    sn = jnp.sum(qs * kbn, axis=1, keepdims=True)
    bcol = _iota((B_HEADS, nblk), 1).astype(F32)

    raw, vts = [], []
    gate = jnp.zeros((B_HEADS, nblk), F32)
    for q in range(nq):
        s = _dot(qsb, bbk[slot, :, q * ch:(q + 1) * ch].astype(BF16))
        raw.append(s)
        vts.append(bbv[slot, :, q * ch:(q + 1) * ch].astype(BF16))
        for k in range(bpc):
            g = jnp.sum(s[:, k * MOBA_BLOCK:(k + 1) * MOBA_BLOCK], axis=1, keepdims=True) * (1.0 / MOBA_BLOCK)
            gate = jnp.where(bcol == float(q * bpc + k), g, gate)
    rank = jnp.zeros((B_HEADS, nblk), F32)
    for m in range(nblk):
        gm = gate[:, m:m + 1]
        rank = rank + jnp.where(gm > gate, 1.0, jnp.where(gm == gate, jnp.where(bcol > float(m), 1.0, 0.0), 0.0))
    selm = jnp.where(rank < n_blk, 1.0, 0.0)
    tiles = []
    for q in range(nq):
        bias = jnp.concatenate(
            [jnp.broadcast_to(jnp.where(selm[:, q * bpc + k:q * bpc + k + 1] > 0.0, 0.0, NEG),
                              (B_HEADS, MOBA_BLOCK)) for k in range(bpc)], axis=1)
        tiles.append(raw[q] + bias)
    return _sample_softmax(tiles, sn, vts, vbn)


def _rider(page_table, small_inputs, caches, batch, nq):
    db, n_pages = page_table.shape
    page = caches[0].shape[2]
    past = n_pages * page
    assert past % (N_KEY_CHUNKS * MOBA_BLOCK) == 0 and page % LANE == 0 and db % batch == 0
    whole = lambda shape: pl.BlockSpec(shape, lambda b, i, pt: (0,) * len(shape))
    return dict(
        in_specs=[whole(a.shape) for a in small_inputs] + [pl.BlockSpec(memory_space=pl.ANY)] * len(caches),
        whole=whole,
        out_spec=whole((db, 8, LANE)),
        out_shape=jax.ShapeDtypeStruct((db, 8, LANE), F32),
        scratch=[pltpu.VMEM((2, LANE, past), F32) for _ in caches] + [pltpu.SemaphoreType.DMA((len(caches), 2))],
        static=dict(n_rows=db, rows_per_seq=db // batch, nq=nq, n_pages=n_pages, past=past,
                    max_rows=_row_schedule(db // batch, nq)[1]),
    )


def _merge_kernel(x_ref, oa_ref, ob_ref, sga_ref, sgb_ref, wpa_ref, wpb_ref, wo_ref, y_ref):
    pa = _dot(oa_ref[...].astype(BF16), wpa_ref[...])
    pb = _dot(ob_ref[...].astype(BF16), wpb_ref[...])
    merged = sga_ref[...] * pa + sgb_ref[...] * pb
    y_ref[...] = x_ref[...] + _dot(merged.astype(BF16), wo_ref[...])


def _merge(x2d, oa, ob, sga, sgb, wpa, wpb, wo):
    n, d = x2d.shape
    tm = min(ROW_BLOCK, n)
    row = lambda c: pl.BlockSpec((tm, c), lambda i: (i, 0))
    const = lambda a: pl.BlockSpec(a.shape, lambda i: (0, 0))
    return pl.pallas_call(
        _merge_kernel,
        grid=(n // tm,),
        in_specs=[row(d), row(A_W), row(B_W), row(d), row(d), const(wpa), const(wpb), const(wo)],
        out_specs=row(d),
        out_shape=jax.ShapeDtypeStruct((n, d), F32),
        compiler_params=_params(1),
        name="merge",
    )(x2d, oa, ob, sga, sgb, wpa, wpb, wo)


FFN_COL_CHUNK = 1408


def _ffn_prompt_kernel(x_ref, oa_ref, ob_ref, sga_ref, sgb_ref, wpa_ref, wpb_ref, wo_ref,
                       n2_ref, wup_ref, cw_ref, cb_ref, wdn_ref, prev_ref, y_ref, tail_ref, ext_s, *, d_ff):
    tm = x_ref.shape[0]
    i = pl.program_id(1)

    @pl.when(i == 0)
    def _():
        ext_s[6:8, :] = prev_ref[...]

    pa = _dot(oa_ref[...].astype(BF16), wpa_ref[...])
    pb = _dot(ob_ref[...].astype(BF16), wpb_ref[...])
    merged = sga_ref[...] * pa + sgb_ref[...] * pb
    x = x_ref[...] + _dot(merged.astype(BF16), wo_ref[...])
    xn = _rms(x, n2_ref[...]).astype(BF16)
    fc = FFN_COL_CHUNK
    for c in range(2 * d_ff // fc):
        ext_s[8:8 + tm, c * fc:(c + 1) * fc] = _dot(xn, wup_ref[:, c * fc:(c + 1) * fc])

    def conv(lo):
        cols = slice(lo, lo + fc)
        out = cb_ref[:, cols] + ext_s[6:6 + tm, cols] * cw_ref[0:1, cols]
        out = out + ext_s[7:7 + tm, cols] * cw_ref[1:2, cols]
        return out + ext_s[8:8 + tm, cols] * cw_ref[2:3, cols]

    y = x
    for j in range(d_ff // fc):
        a = conv(j * fc)
        g = conv(d_ff + j * fc)
        act = (a / (1.0 + jnp.exp(-a)) * g).astype(BF16)
        y = y + _dot(act, wdn_ref[j * fc:(j + 1) * fc, :])
    y_ref[...] = y
    tail = ext_s[tm + 6:tm + 8, :]
    tail_ref[...] = tail
    ext_s[6:8, :] = tail


def _merge_ffn_prompt(x2d, oa, ob, sga, sgb, wpa, wpb, wo, n2, wup, cw, cb, wdn, prev, batch, t):
    n, d = x2d.shape
    d_ff = wdn.shape[0]
    assert d_ff % FFN_COL_CHUNK == 0
    tm = ROW_BLOCK
    nt = t // tm
    row = lambda c: pl.BlockSpec((tm, c), lambda b, i: (b * nt + i, 0))
    const = lambda a: pl.BlockSpec(a.shape, lambda b, i: (0, 0), pipeline_mode=pl.Buffered(1))
    per_b = pl.BlockSpec((None, 2, 2 * d_ff), lambda b, i: (b, 0, 0))
    return pl.pallas_call(
        functools.partial(_ffn_prompt_kernel, d_ff=d_ff),
        grid=(batch, nt),
        in_specs=[row(d), row(A_W), row(B_W), row(d), row(d), const(wpa), const(wpb), const(wo),
                  const(n2), const(wup), const(cw), const(cb), const(wdn), per_b],
        out_specs=[row(d), per_b],
        out_shape=[jax.ShapeDtypeStruct((n, d), F32), jax.ShapeDtypeStruct((batch, 2, 2 * d_ff), F32)],
        scratch_shapes=[pltpu.VMEM((tm + 8, 2 * d_ff), F32)],
        compiler_params=_params(2),
        name="merge_ffn_prompt",
    )(x2d, oa, ob, sga, sgb, wpa, wpb, wo, n2, wup, cw, cb, wdn, prev)


def _ffn_sample_kernel(x_ref, n2_ref, wup_ref, cw_ref, cb_ref, wdn_ref, s0_ref, s1_ref, y_ref, up_ref, *, d_ff):
    x = x_ref[...]
    xn = _rms(x, n2_ref[...]).astype(BF16)
    fc = FFN_COL_CHUNK
    for c in range(2 * d_ff // fc):
        up_ref[:, c * fc:(c + 1) * fc] = _dot(xn, wup_ref[:, c * fc:(c + 1) * fc])

    def conv(lo):
        cols = slice(lo, lo + fc)
        out = cb_ref[:, cols] + s0_ref[:, cols] * cw_ref[0:1, cols]
        out = out + s1_ref[:, cols] * cw_ref[1:2, cols]
        return out + up_ref[:, cols] * cw_ref[2:3, cols]

    y = x
    for j in range(d_ff // fc):
        a = conv(j * fc)
        g = conv(d_ff + j * fc)
        act = (a / (1.0 + jnp.exp(-a)) * g).astype(BF16)
        y = y + _dot(act, wdn_ref[j * fc:(j + 1) * fc, :])
    y_ref[...] = y


def _ffn_sample(x2d, n2, wup, cw, cb, wdn, s0, s1):
    n, d = x2d.shape
    d_ff = wdn.shape[0]
    full = lambda a: pl.BlockSpec(a.shape, lambda i: (0, 0), pipeline_mode=pl.Buffered(1))
    args = (x2d, n2, wup, cw, cb, wdn, s0, s1)
    return pl.pallas_call(
        functools.partial(_ffn_sample_kernel, d_ff=d_ff),
        grid=(1,),
        in_specs=[full(a) for a in args],
        out_specs=[pl.BlockSpec((n, d), lambda i: (0, 0)), pl.BlockSpec((n, 2 * d_ff), lambda i: (0, 0))],
        out_shape=[jax.ShapeDtypeStruct((n, d), F32), jax.ShapeDtypeStruct((n, 2 * d_ff), F32)],
        compiler_params=_params(1),
        name="ffn_sample",
    )(*args)


def _rope_tables(pos):
    half = HEAD_DIM // 2
    inv = ROPE_THETA ** (-jnp.arange(half, dtype=F32) / half)
    ang = pos.astype(F32)[:, None] * inv[None, :]
    cos = jnp.cos(ang)
    sin = jnp.sin(ang)
    return jnp.tile(cos, (1, 4)), jnp.tile(jnp.concatenate([-sin, sin], axis=1), (1, 2))


def _layout_w_in(w_in, d_model):
    splits = (A_W, A_KV_W, A_KV_W, I_W, IDX_DIM, IDX_HEADS, B_W, B_KV_W, B_KV_W, d_model, d_model)
    offs = [0]
    for s in splits:
        offs.append(offs[-1] + s)
    p = [w_in[:, offs[k]:offs[k + 1]] for k in range(len(splits))]
    qa, ka, va, qi, ki, wi, qb, kb, vb, ga, gb = p
    pad = jnp.zeros((w_in.shape[0], LANE - IDX_DIM - IDX_HEADS), w_in.dtype)
    return jnp.concatenate([qa, ka, va, qi, qb, kb, vb, ga, gb, ki, wi, pad], axis=1).astype(BF16)


def _layout_gains(q_norm_a, k_norm_a, k_norm_idx, q_norm_b, k_norm_b, d_model):
    one = lambda n: jnp.ones((n,), F32)
    return jnp.concatenate([
        jnp.tile(q_norm_a, A_HEADS), jnp.tile(k_norm_a, A_KV_HEADS), one(A_KV_W), one(I_W),
        jnp.tile(q_norm_b, B_HEADS), jnp.tile(k_norm_b, B_KV_HEADS), one(B_KV_W),
        one(2 * d_model), k_norm_idx, one(LANE - IDX_DIM)])[None, :]


def _block_diag_q(q, heads_per_kv):
    z = jnp.zeros_like(q)
    low = jnp.concatenate([q, z], axis=-1)
    high = jnp.concatenate([z, q], axis=-1)
    is_low = (jnp.arange(q.shape[1]) // heads_per_kv == 0)[None, :, None]
    return jnp.where(is_low, low, high)


def _pick_kv(o, heads_per_kv):
    n = o.shape[0]
    return jnp.concatenate([o[:, :heads_per_kv, :HEAD_DIM].reshape(n, -1),
                            o[:, heads_per_kv:, HEAD_DIM:].reshape(n, -1)], axis=1)


def _sample_operands(qa_s, ka_s, va_s, qi_s, qb_s, kb_s, vb_s, kw_s, c_ik, c_ak, c_av, c_bk, c_bv):
    db = qa_s.shape[0]
    n_pool, page = c_ak.shape[0], c_ak.shape[1]
    hpk_a = A_HEADS // A_KV_HEADS
    hpk_b = B_HEADS // B_KV_HEADS
    qi_pad = jnp.pad(qi_s.reshape(db, IDX_HEADS, IDX_DIM), ((0, 0), (0, 0), (0, LANE - IDX_DIM)))
    wi_s = kw_s[:, IDX_DIM:IDX_DIM + IDX_HEADS].reshape(db, 1, IDX_HEADS)
    qa_bd = _block_diag_q(qa_s.reshape(db, A_HEADS, HEAD_DIM), hpk_a)
    qb_bd = _block_diag_q(qb_s.reshape(db, B_HEADS, HEAD_DIM), hpk_b)
    kin = jnp.where(jnp.arange(LANE) < IDX_DIM, kw_s, 0.0).reshape(db, 1, LANE)
    pages_t = lambda c: jnp.moveaxis(c, 1, -1).reshape(n_pool, -1, page)
    dsa = ([qa_bd, ka_s.reshape(db, 1, LANE), va_s.reshape(db, 1, LANE)], [pages_t(c_ak), pages_t(c_av)])
    moba = ([qb_bd, kb_s.reshape(db, 1, LANE), vb_s.reshape(db, 1, LANE), qi_pad, wi_s, kin],
            [pages_t(c_bk), pages_t(c_bv), pages_t(c_ik)])
    return dsa, moba


def kernel(x_prompt, x_sample, cache_a_k, cache_a_v, cache_idx_k, cache_b_k, cache_b_v, state_conv, page_table, norm1, w_in, q_norm_a, k_norm_a, k_norm_idx, q_norm_b, k_norm_b, w_proj_a, w_proj_b, w_out, norm2, w_up, conv_w, conv_b, w_down):
    batch, t, d = x_prompt.shape
    db, ds, _ = x_sample.shape
    depth = norm1.shape[0]
    assert depth == 1 and ds == 1
    n_pool, page = cache_a_k.shape[1], cache_a_k.shape[2]
    past = page_table.shape[1] * page
    l = 0
    w = _layout_w_in(w_in[l], d)
    gains = _layout_gains(q_norm_a[l], k_norm_a[l], k_norm_idx[l], q_norm_b[l], k_norm_b[l], d)
    wpa, wpb, wo = w_proj_a[l].astype(BF16), w_proj_b[l].astype(BF16), w_out[l].astype(BF16)
    wup, wdn = w_up[l].astype(BF16), w_down[l].astype(BF16)
    n1, n2 = norm1[l][None, :], norm2[l][None, :]
    cw, cb = conv_w[l], conv_b[l][None, :]
    d_ff = wdn.shape[0]

    xp = x_prompt.reshape(batch * t, d)
    cos_p, sin_p = _rope_tables(jnp.arange(t, dtype=jnp.int32))
    qa, ka, _, qi, qb, kb, _, sga, sgb, kw, kat, vat, kbt, vbt, kit = _proj(xp, n1, w, gains, cos_p, sin_p, t)
    xs = x_sample.reshape(db, d)
    cos_s, sin_s = _rope_tables(jnp.full((db,), past, jnp.int32))
    (qa_s, ka_s, va_s, qi_s, qb_s, kb_s, vb_s, sga_s, sgb_s, kw_s,
     kat_s, vat_s, kbt_s, vbt_s, kit_s) = _proj(xs, n1, w, gains, cos_s, sin_s, db)

    dsa_s, moba_s = _sample_operands(qa_s, ka_s, va_s, qi_s, qb_s, kb_s, vb_s, kw_s,
                                     cache_idx_k[l], cache_a_k[l], cache_a_v[l], cache_b_k[l], cache_b_v[l])
    ob, ob_s, sc_s, sn_s = _moba_branch(qb, kb, vbt, batch, t, page_table, *moba_s)
    vstar_s, jcut_s = _dsa_select(sc_s.reshape(db, past), sn_s.reshape(db, LANE), past)
    per_row = lambda a: jnp.broadcast_to(a.reshape(db, 1, 1), (db, 1, LANE))
    oa, oa_s = _dsa_branch(qi, qa, kw, ka, vat, batch, t, page_table,
                           [sc_s, sn_s, per_row(vstar_s), per_row(jcut_s)] + dsa_s[0], dsa_s[1])
    oa_s = _pick_kv(oa_s, A_HEADS // A_KV_HEADS)
    ob_s = _pick_kv(ob_s, B_HEADS // B_KV_HEADS)

    yp, p_conv = _merge_ffn_prompt(xp, oa, ob, sga, sgb, wpa, wpb, wo, n2, wup, cw, cb, wdn,
                                   jnp.zeros((batch, 2, 2 * d_ff), F32), batch, t)
    x1_s = _merge(xs, oa_s, ob_s, sga_s, sgb_s, wpa, wpb, wo)
    ys, up_s = _ffn_sample(x1_s, n2, wup, cw, cb, wdn, state_conv[l, :, 0], state_conv[l, :, 1])

    def rows5(a, n, s, h):
        a = a.reshape(a.shape[0], h, HEAD_DIM, a.shape[2])
        return jnp.transpose(a, (0, 3, 1, 2)).reshape(1, n, s, h, HEAD_DIM)

    def rows4(a, n, s):
        return jnp.transpose(a, (0, 2, 1)).reshape(1, n, s, IDX_DIM)

    return (
        yp.reshape(batch, t, d), ys.reshape(db, 1, d),
        rows5(kat, batch, t, A_KV_HEADS), rows5(vat, batch, t, A_KV_HEADS), rows4(kit, batch, t),
        rows5(kbt, batch, t, B_KV_HEADS), rows5(vbt, batch, t, B_KV_HEADS),
        p_conv[None],
        rows5(kat_s, db, 1, A_KV_HEADS), rows5(vat_s, db, 1, A_KV_HEADS), rows4(kit_s, db, 1),
        rows5(kbt_s, db, 1, B_KV_HEADS), rows5(vbt_s, db, 1, B_KV_HEADS),
        jnp.stack([state_conv[l, :, 1], up_s], axis=1)[None],
    )
```

```python
import functools

import jax
import jax.numpy as jnp
from jax import lax
from jax.experimental import pallas as pl
from jax.experimental.pallas import tpu as pltpu

HEAD_DIM = 64
A_HEADS = 8
A_KV_HEADS = 2
IDX_HEADS = 8
IDX_DIM = 64
IDX_TOPK = 256
B_HEADS = 8
B_KV_HEADS = 2
MOBA_BLOCK = 256
MOBA_TOPK = 3
ROPE_THETA = 10000.0
EPS = 1e-6
NEG = -1e30
BIG = 3e38

LANE = 128
Q_BLOCK = 256
ROW_BLOCK = 256
VMEM_LIMIT = 56 * 1024 * 1024
BISECT_STEPS = 13

F32 = jnp.float32
BF16 = jnp.bfloat16

A_W = A_HEADS * HEAD_DIM
A_KV_W = A_KV_HEADS * HEAD_DIM
I_W = IDX_HEADS * IDX_DIM
B_W = B_HEADS * HEAD_DIM
B_KV_W = B_KV_HEADS * HEAD_DIM
assert A_W == 512 and I_W == 512 and B_W == 512 and A_KV_W == LANE and B_KV_W == LANE


def _params(n_grid):
    return pltpu.CompilerParams(dimension_semantics=("arbitrary",) * n_grid,
                                vmem_limit_bytes=VMEM_LIMIT)


def _dot(a, b):
    return jnp.dot(a, b, preferred_element_type=F32)


def _dot_nt(a, b):
    return lax.dot_general(a, b, (((1,), (1,)), ((), ())), preferred_element_type=F32)


def _iota(shape, dim):
    return lax.broadcasted_iota(jnp.int32, shape, dim)


def _rms(x, g):
    return x * lax.rsqrt(jnp.mean(x * x, axis=-1, keepdims=True) + EPS) * g


def _proj_kernel(x_ref, n1_ref, w_ref, g_ref, cos_ref, sin_ref,
                 qa_ref, ka_ref, va_ref, qi_ref, qb_ref, kb_ref, vb_ref, sga_ref, sgb_ref, kw_ref,
                 kat_ref, vat_ref, kbt_ref, vbt_ref, kit_ref, *, d_model):
    h = _rms(x_ref[...], n1_ref[...]).astype(BF16)
    gr = lax.shift_right_logical(_iota((2 * LANE, 2 * LANE), 0), 6)
    gc = lax.shift_right_logical(_iota((2 * LANE, 2 * LANE), 1), 6)
    gsum = jnp.where(gr == gc, 1.0, 0.0).astype(BF16)
    cos1 = cos_ref[...]
    sin1 = sin_ref[...]

    def seg(off, width):
        return _dot(h, w_ref[:, off:off + width])

    def head_norm(x, off):
        width = x.shape[1]
        ss = _dot((x * x).astype(BF16), gsum[:width, :width])
        return x * lax.rsqrt(ss * (1.0 / HEAD_DIM) + EPS) * g_ref[:, off:off + width]

    def rope(x):
        width = x.shape[1]
        rep = width // LANE
        cs = jnp.concatenate([cos1] * rep, axis=1) if rep > 1 else cos1
        sn = jnp.concatenate([sin1] * rep, axis=1) if rep > 1 else sin1
        hi = (_iota(x.shape, 1) & (HEAD_DIM // 2)) != 0
        swapped = jnp.where(hi, pltpu.roll(x, HEAD_DIM // 2, axis=1),
                            pltpu.roll(x, width - HEAD_DIM // 2, axis=1))
        return x * cs + swapped * sn

    o = 0
    for half in range(2):
        x = seg(o + half * 256, 256)
        qa_ref[:, half * 256:(half + 1) * 256] = rope(head_norm(x, o + half * 256))
    o = A_W
    x = seg(o, 256)
    xn = rope(head_norm(x, o))
    ka_ref[...] = xn[:, :LANE]
    va_ref[...] = x[:, LANE:]
    kat_ref[...] = xn[:, :LANE].T
    vat_ref[...] = x[:, LANE:].T
    o = A_W + 2 * LANE
    for half in range(2):
        qi_ref[:, half * 256:(half + 1) * 256] = rope(seg(o + half * 256, 256))
    o = A_W + 2 * LANE + I_W
    for half in range(2):
        x = seg(o + half * 256, 256)
        qb_ref[:, half * 256:(half + 1) * 256] = rope(head_norm(x, o + half * 256))
    o = A_W + 2 * LANE + I_W + B_W
    x = seg(o, 256)
    xn = rope(head_norm(x, o))
    kb_ref[...] = xn[:, :LANE]
    vb_ref[...] = x[:, LANE:]
    kbt_ref[...] = xn[:, :LANE].T
    vbt_ref[...] = x[:, LANE:].T
    o = A_W + 2 * LANE + I_W + B_W + 2 * LANE
    for part in range(d_model // 256):
        x = seg(o + part * 256, 256)
        sga_ref[:, part * 256:(part + 1) * 256] = (1.0 / (1.0 + jnp.exp(-x))).astype(sga_ref.dtype)
    o += d_model
    for part in range(d_model // 256):
        x = seg(o + part * 256, 256)
        sgb_ref[:, part * 256:(part + 1) * 256] = (1.0 / (1.0 + jnp.exp(-x))).astype(sgb_ref.dtype)
    o += d_model
    x = seg(o, LANE)
    xn = rope(head_norm(x, o))
    kw = jnp.where(_iota(x.shape, 1) < IDX_DIM, xn, x)
    kw_ref[...] = kw
    kit_ref[...] = kw.T[:IDX_DIM]


def _proj(x2d, n1, w, gains, cos, sin, rows_per_seq):
    n, d = x2d.shape
    tm = min(ROW_BLOCK, rows_per_seq)
    nt = rows_per_seq // tm
    nseq = n // rows_per_seq
    nw = w.shape[1]
    row = lambda c: pl.BlockSpec((tm, c), lambda i: (i, 0))
    const = lambda shape: pl.BlockSpec(shape, lambda i: (0, 0))
    tab = pl.BlockSpec((tm, LANE), lambda i: (i % nt, 0))
    chan = lambda c: pl.BlockSpec((None, c, tm), lambda i: (i // nt, 0, i % nt))
    rows = [A_W, LANE, LANE, I_W, B_W, LANE, LANE, d, d, LANE]
    chans = [LANE, LANE, LANE, LANE, IDX_DIM]
    return pl.pallas_call(
        functools.partial(_proj_kernel, d_model=d),
        grid=(n // tm,),
        in_specs=[row(d), const((1, d)), const((d, nw)), const((1, nw)), tab, tab],
        out_specs=[row(c) for c in rows] + [chan(c) for c in chans],
        out_shape=[jax.ShapeDtypeStruct((n, c), BF16 if k in (7, 8) else F32) for k, c in enumerate(rows)]
        + [jax.ShapeDtypeStruct((nseq, c, rows_per_seq), F32) for c in chans],
        compiler_params=_params(1),
        name="proj",
    )(x2d, n1, w, gains, cos, sin)


def _head_operand(q_ref, h, want_low, scale):
    ch = q_ref[:, LANE * (h // 2):LANE * (h // 2) + LANE]
    if (h % 2 == 0) != want_low:
        ch = pltpu.roll(ch, HEAD_DIM, axis=1)
    low = _iota(ch.shape, 1) < HEAD_DIM
    keep = low if want_low else jnp.logical_not(low)
    return (jnp.where(keep, ch, 0.0) * scale).astype(BF16)


def _fill_kv(k_ref, vt_ref, kb_s, vt_s, nchunk, tc):
    kb_s[...] = k_ref[...].astype(BF16)
    extra = jnp.where(_iota((V_ROWS - HEAD_DIM, tc), 0) == 0, 1.0, 0.0)
    for c in range(nchunk):
        for kv in range(2):
            vt = vt_ref[kv * HEAD_DIM:(kv + 1) * HEAD_DIM, c * tc:(c + 1) * tc]
            vt_s[kv, c] = jnp.concatenate([vt, extra], axis=0).astype(BF16)


def _attend_chunk(kc, c, q_s, vt_s, st, n_heads, heads_per_kv, tile_bias_of=None, row_bias_of=None):
    s_s, p_s, mc_s, a_s, m_s, acc_s = st
    for h in range(n_heads):
        s = _dot_nt(kc, q_s[h])
        if tile_bias_of is not None:
            s = s + tile_bias_of(h)
        s_s[h] = s
        mc_s[h] = jnp.max(s, axis=0, keepdims=True)
    for h in range(n_heads):
        m_old = m_s[h]
        if row_bias_of is None:
            m_new = jnp.maximum(m_old, mc_s[h])
            shift = m_new
        else:
            rb = row_bias_of(h)
            m_new = jnp.maximum(m_old, mc_s[h] + rb)
            shift = m_new - rb
        p_s[h] = jnp.exp2(s_s[h] - shift).astype(BF16)
        a_s[h] = jnp.exp2(m_old - m_new)
        m_s[h] = m_new
    for h in range(n_heads):
        acc_s[h] = a_s[h] * acc_s[h] + _dot(vt_s[h // heads_per_kv, c], p_s[h])


V_ROWS = HEAD_DIM + 16
LOG2E = 1.4426950408889634


def _attention_scratch(n_heads, tq):
    row = pltpu.VMEM((n_heads, 1, tq), F32)
    return [pltpu.VMEM((n_heads, tq, tq), F32), pltpu.VMEM((n_heads, tq, tq), BF16), row, row, row,
            pltpu.VMEM((n_heads, V_ROWS, tq), F32)]


def _init_softmax(st):
    m_s, acc_s = st[-2], st[-1]
    m_s[...] = jnp.full(m_s.shape, NEG, F32)
    acc_s[...] = jnp.zeros(acc_s.shape, F32)


def _write_heads(o_ref, st, n_heads):
    acc_s = st[-1]
    for j in range(n_heads // 2):
        pair = []
        for h in (2 * j, 2 * j + 1):
            acc = acc_s[h]
            pair.append(acc[:HEAD_DIM] / acc[HEAD_DIM:HEAD_DIM + 1])
        o_ref[:, LANE * j:LANE * (j + 1)] = jnp.concatenate(pair, axis=0).T.astype(o_ref.dtype)


def _select_topk(sweep, count_gt, n_sel, small, tq, t_total, score_range=None):
    row = lambda v: jnp.full((1, tq), v, F32)
    if score_range is not None:
        rowmax, rowmin = score_range
    else:
        rowmax = sweep(lambda s, c, a: jnp.maximum(a, jnp.max(s, axis=0, keepdims=True)), row(NEG))
        rowmin = sweep(lambda s, c, a: jnp.minimum(
            a, jnp.min(jnp.where(s > 0.5 * NEG, s, BIG), axis=0, keepdims=True)), row(BIG))

    def bisect(_, lh):
        lo, hi = lh
        mid = 0.5 * (lo + hi)
        ge = count_gt(mid) >= n_sel
        return jnp.where(ge, mid, lo), jnp.where(ge, hi, mid)

    lo, _ = lax.fori_loop(0, BISECT_STEPS, bisect, (rowmin, rowmax))
    tall = lambda v: jnp.full((8, tq), v, F32)
    fold8 = lambda x: x.reshape(x.shape[0] // 8, 8, tq)
    u0 = jnp.min(sweep(lambda s, c, a: jnp.minimum(a, jnp.min(fold8(jnp.where(s >= lo, s, BIG)), axis=0)),
                       tall(BIG)), axis=0, keepdims=True)

    def walk_cond(st):
        return st[2] == 0

    def walk(st):
        u = st[0]

        def f(s, c, carry):
            cnt, nxt = carry
            gt = s > u
            cnt = cnt + jnp.sum(fold8(jnp.where(gt, 1.0, 0.0)), axis=0)
            nxt = jnp.minimum(nxt, jnp.min(fold8(jnp.where(gt, s, BIG)), axis=0))
            return cnt, nxt

        cnt, nxt = sweep(f, (tall(0.0), tall(BIG)))
        cnt = jnp.sum(cnt, axis=0, keepdims=True)
        nxt = jnp.min(nxt, axis=0, keepdims=True)
        done = jnp.logical_or(cnt < n_sel, small)
        all_done = jnp.min(jnp.where(done, 1.0, 0.0)).astype(jnp.int32)
        return jnp.where(done, u, nxt), cnt, all_done

    u, cgt, _ = lax.while_loop(walk_cond, walk, (u0, row(0.0), jnp.int32(0)))
    vstar = jnp.where(small, NEG, u)
    need = jnp.where(small, 0.0, n_sel - cgt)
    n_eq = sweep(lambda s, c, a: a + jnp.sum(jnp.where(s == vstar, 1.0, 0.0), axis=0, keepdims=True), row(0.0))
    excess = jnp.max(jnp.where(jnp.logical_and(n_eq > need, jnp.logical_not(small)), 1.0, 0.0))

    n_jsteps = jnp.where(excess > 0.0, t_total.bit_length() + 1, 0).astype(jnp.int32)

    def jstep(_, jj):
        jlo, jhi = jj
        mid = lax.shift_right_logical(jlo + jhi, 1)
        e = sweep(lambda s, c, a: a + jnp.sum(
            jnp.where(s == vstar, jnp.where(c < mid, 1.0, 0.0), 0.0), axis=0, keepdims=True), row(0.0))
        ge = e >= need
        return jnp.where(ge, jlo, mid), jnp.where(ge, mid, jhi)

    zero_i = jnp.zeros((1, tq), jnp.int32)
    _, jcut = lax.fori_loop(0, n_jsteps, jstep, (zero_i, zero_i + t_total))
    return vstar, jnp.where(small, 0, jcut)


def _dsa_prompt_kernel(pt_ref, qi_ref, qa_ref, kwq_ref, kwk_ref, ka_ref, vat_ref,
                       ssc_ref, ssn_ref, svs_ref, sjc_ref, sqbd_ref, skan_ref, svan_ref, cak, cav,
                       o_ref, os_ref,
                       kib_s, kab_s, vt_s, qim_s, qam_s, sc_s, s_s, p_s, mc_s, a_s, m_s, acc_s,
                       bak, bav, sem, *, n_sel, t_total, rider):
    st = (s_s, p_s, mc_s, a_s, m_s, acc_s)
    tq = qi_ref.shape[0]
    tc = tq
    nchunk = t_total // tc
    i = pl.program_id(1)

    def sample_compute(k, slot):
        os_ref[k] = _dsa_sample_attend(ssc_ref[k], ssn_ref[k][:, 0:1], svs_ref[k][:, 0:1], sjc_ref[k][:, 0:1],
                                       sqbd_ref[k], skan_ref[k], svan_ref[k], bak, bav, slot, rider["past"])

    rps = rider["rows_per_step"]
    run_row = _sample_row_runner(pt_ref, pl.program_id(0) * pl.num_programs(1) + i, rps, rider["n_rows"],
                                 (cak, cav), (bak, bav), sem, rider["n_pages"], sample_compute)
    for k in range((rps + 1) // 2):
        run_row(k)

    @pl.when(i == 0)
    def _():
        kib_s[...] = kwk_ref[...].astype(BF16)
        _fill_kv(ka_ref, vat_ref, kab_s, vt_s, nchunk, tc)

    heads_per_kv = A_HEADS // A_KV_HEADS
    for h in range(IDX_HEADS):
        qim_s[h] = _head_operand(qi_ref, h, True, 1.0)
    for h in range(A_HEADS):
        qam_s[h] = _head_operand(qa_ref, h, (h // heads_per_kv) == 0, HEAD_DIM ** -0.5 * LOG2E)
    w8 = kwq_ref[...].T[IDX_DIM:IDX_DIM + IDX_HEADS, :] * (IDX_DIM ** -0.5 * IDX_HEADS ** -0.5)

    krow = _iota((tc, tq), 0)
    qcol = _iota((tc, tq), 1)

    def chunk(c):
        return pl.ds(pl.multiple_of(c * tc, tc), tc)

    def score_chunk(c, diag, rng):
        kc = kib_s[chunk(c), :]
        a = jnp.zeros((tc, tq), F32)
        for h in range(IDX_HEADS):
            a = a + jnp.maximum(_dot_nt(kc, qim_s[h]), 0.0) * w8[h:h + 1, :]
        lo_src = a
        if diag:
            lo_src = jnp.where(krow <= qcol, a, BIG)
            a = jnp.where(krow <= qcol, a, NEG)
        sc_s[chunk(c), :] = a
        return (jnp.maximum(rng[0], jnp.max(a, axis=0, keepdims=True)),
                jnp.minimum(rng[1], jnp.min(lo_src, axis=0, keepdims=True)))

    rng0 = (jnp.full((1, tq), NEG, F32), jnp.full((1, tq), BIG, F32))
    rowmax, rowmin = score_chunk(i, True, lax.fori_loop(0, i, lambda c, rng: score_chunk(c, False, rng), rng0))

    def sweep(fn, init):
        return lax.fori_loop(0, i + 1, lambda c, carry: fn(sc_s[chunk(c), :], c * tc + krow, carry), init)

    def count_gt(x):
        part = sweep(lambda s, c, a: a + jnp.sum(jnp.where(s > x, 1.0, 0.0).reshape(tc // 8, 8, tq), axis=0),
                     jnp.zeros((8, tq), F32))
        return jnp.sum(part, axis=0, keepdims=True)

    n_adm = i * tq + _iota((1, tq), 1) + 1
    small = n_adm <= n_sel
    vstar, jcut = _select_topk(sweep, count_gt, float(n_sel), small, tq, t_total, (rowmax, rowmin))

    def bias_body(c, carry):
        s = sc_s[chunk(c), :]
        tie = jnp.where((c * tc + krow) < jcut, 0.0, NEG)
        sc_s[chunk(c), :] = jnp.where(s > vstar, 0.0, jnp.where(s == vstar, tie, NEG))
        return carry

    lax.fori_loop(0, i + 1, bias_body, 0)

    for k in range((rps + 1) // 2, rps):
        run_row(k)

    _init_softmax(st)

    def att_body(c, carry):
        _attend_chunk(kab_s[chunk(c), :], c, qam_s, vt_s, st, A_HEADS, heads_per_kv,
                      tile_bias_of=lambda h: sc_s[chunk(c), :])
        return carry

    lax.fori_loop(0, i + 1, att_body, 0)
    _write_heads(o_ref, st, A_HEADS)


def _dsa_branch(qi, qa, kw, ka, vat, batch, t, page_table, sample_inputs, caches):
    tq = Q_BLOCK
    nq = t // tq
    n_sel = min(IDX_TOPK, t // 4)
    qblk = lambda c: pl.BlockSpec((tq, c), lambda b, i, pt: (b * nq + i, 0))
    full = pl.BlockSpec((t, LANE), lambda b, i, pt: (b, 0))
    full_t = pl.BlockSpec((None, LANE, t), lambda b, i, pt: (b, 0, 0))
    rd = _rider(page_table, sample_inputs, caches, batch, nq)
    grid_spec = pltpu.PrefetchScalarGridSpec(
        num_scalar_prefetch=1,
        grid=(batch, nq),
        in_specs=[qblk(I_W), qblk(A_W), qblk(LANE), full, full, full_t] + rd["in_specs"],
        out_specs=[qblk(A_W), rd["out_spec"]],
        scratch_shapes=[
            pltpu.VMEM((t, LANE), BF16), pltpu.VMEM((t, LANE), BF16),
            pltpu.VMEM((2, nq, V_ROWS, tq), BF16),
            pltpu.VMEM((IDX_HEADS, tq, LANE), BF16), pltpu.VMEM((A_HEADS, tq, LANE), BF16),
            pltpu.VMEM((t, tq), F32),
        ] + _attention_scratch(A_HEADS, tq) + rd["scratch"],
    )
    return pl.pallas_call(
        functools.partial(_dsa_prompt_kernel, n_sel=n_sel, t_total=t, rider=rd["static"]),
        grid_spec=grid_spec,
        out_shape=[jax.ShapeDtypeStruct((batch * t, A_W), BF16), rd["out_shape"]],
        compiler_params=_params(2),
        name="dsa",
    )(page_table, qi, qa, kw, kw, ka, vat, *sample_inputs, *caches)


def _moba_prompt_kernel(pt_ref, qb_ref, kb_ref, vbt_ref, sqbd_ref, skbn_ref, svbn_ref, sqi_ref, swi_ref, skin_ref,
                        cbk, cbv, cik, o_ref, os_ref, osc_ref, osn_ref,
                        kbb_s, vt_s, kbar_s, qbm_s, selb_s, s_s, p_s, mc_s, a_s, m_s, acc_s,
                        bbk, bbv, bik, sem, *, n_blk, t_total, n_blk_s, rider):
    st = (s_s, p_s, mc_s, a_s, m_s, acc_s)
    tq = qb_ref.shape[0]
    tc = tq
    nb = t_total // tc
    nbp = kbar_s.shape[0]
    i = pl.program_id(1)
    heads_per_kv = B_HEADS // B_KV_HEADS

    def sample_compute(k, slot):
        os_ref[k] = _moba_sample_row(sqbd_ref[k], skbn_ref[k], svbn_ref[k], bbk, bbv, slot, n_blk_s, rider["past"])
        _dsa_sample_scores(sqi_ref[k], swi_ref[k], skin_ref[k], bik, slot, osc_ref.at[k], osn_ref.at[k],
                           rider["past"])

    def zero_pad_rows():
        bik[...] = jnp.zeros(bik.shape, F32)

    rps = rider["rows_per_step"]
    run_row = _sample_row_runner(pt_ref, pl.program_id(0) * pl.num_programs(1) + i, rps, rider["n_rows"],
                                 (cbk, cbv, cik), (bbk, bbv, bik), sem, rider["n_pages"], sample_compute,
                                 zero_pad_rows)
    for k in range((rps + 1) // 2):
        run_row(k)

    @pl.when(i == 0)
    def _():
        _fill_kv(kb_ref, vbt_ref, kbb_s, vt_s, nb, tc)
        kbar_s[...] = jnp.zeros(kbar_s.shape, F32)
        for n in range(nb):
            kbar_s[n:n + 1, :] = jnp.mean(kb_ref[n * tc:(n + 1) * tc, :], axis=0, keepdims=True)

    kbar = kbar_s[...].astype(BF16)
    blk = _iota((nbp, tq), 0)
    past = blk < i
    for h in range(B_HEADS):
        qm = _head_operand(qb_ref, h, (h // heads_per_kv) == 0, HEAD_DIM ** -0.5 * LOG2E)
        qbm_s[h] = qm
        gate = jnp.where(past, _dot_nt(kbar, qm), NEG)
        rank = jnp.zeros((nbp, tq), F32)
        for m in range(nb):
            gm = gate[m:m + 1, :]
            first = jnp.where(blk > m, 1.0, 0.0)
            rank = rank + jnp.where(gm > gate, 1.0, jnp.where(gm == gate, first, 0.0))
        sel = jnp.logical_and(past, rank < n_blk)
        selb_s[h] = jnp.where(sel, 0.0, NEG)

    _init_softmax(st)
    krow = _iota((tc, tq), 0)
    qcol = _iota((tc, tq), 1)

    def chunk(c):
        return pl.ds(pl.multiple_of(c * tc, tc), tc)

    _attend_chunk(kbb_s[chunk(i), :], i, qbm_s, vt_s, st, B_HEADS, heads_per_kv,
                  tile_bias_of=lambda h: jnp.where(krow <= qcol, 0.0, NEG))

    for k in range((rps + 1) // 2, rps):
        run_row(k)

    def att_body(c, carry):
        _attend_chunk(kbb_s[chunk(c), :], c, qbm_s, vt_s, st, B_HEADS, heads_per_kv,
                      row_bias_of=lambda h: selb_s[h, pl.ds(c, 1), :])
        return carry

    lax.fori_loop(0, i, att_body, 0)
    _write_heads(o_ref, st, B_HEADS)


def _moba_branch(qb, kb, vbt, batch, t, page_table, sample_inputs, caches):
    tq = Q_BLOCK
    assert tq == MOBA_BLOCK and t % tq == 0
    nq = t // tq
    n_blk = min(MOBA_TOPK, (t - 1) // MOBA_BLOCK)
    qblk = pl.BlockSpec((tq, B_W), lambda b, i, pt: (b * nq + i, 0))
    full = pl.BlockSpec((t, LANE), lambda b, i, pt: (b, 0))
    full_t = pl.BlockSpec((None, LANE, t), lambda b, i, pt: (b, 0, 0))
    rd = _rider(page_table, sample_inputs, caches, batch, nq)
    past, rps, db = rd["static"]["past"], rd["static"]["rows_per_step"], rd["static"]["n_rows"]
    ch = past // N_KEY_CHUNKS
    per_step = lambda shape: pl.BlockSpec((rps,) + shape, lambda b, i, pt: (b * nq + i, 0, 0))
    grid_spec = pltpu.PrefetchScalarGridSpec(
        num_scalar_prefetch=1,
        grid=(batch, nq),
        in_specs=[qblk, full, full_t] + rd["in_specs"],
        out_specs=[qblk, rd["out_spec"], per_step((N_KEY_CHUNKS, ch)), per_step((1, LANE))],
        scratch_shapes=[
            pltpu.VMEM((t, LANE), BF16),
            pltpu.VMEM((2, nq, V_ROWS, tq), BF16),
            pltpu.VMEM((max(nq, 8), LANE), F32),
            pltpu.VMEM((B_HEADS, tq, LANE), BF16),
            pltpu.VMEM((B_HEADS, max(nq, 8), tq), F32),
        ] + _attention_scratch(B_HEADS, tq) + rd["scratch"],
    )
    return pl.pallas_call(
        functools.partial(_moba_prompt_kernel, n_blk=float(n_blk), t_total=t,
                          n_blk_s=float(min(MOBA_TOPK, past // MOBA_BLOCK)), rider=rd["static"]),
        grid_spec=grid_spec,
        out_shape=[jax.ShapeDtypeStruct((batch * t, B_W), BF16), rd["out_shape"],
                   jax.ShapeDtypeStruct((db, N_KEY_CHUNKS, ch), F32), jax.ShapeDtypeStruct((db, 1, LANE), F32)],
        compiler_params=_params(2),
        name="moba",
    )(page_table, qb, kb, vbt, *sample_inputs, *caches)


def _page_copy(cache, buf, sem, k, slot, page, p):
    rows, width = cache.shape[1], cache.shape[2]
    return pltpu.make_async_copy(cache.at[page], buf.at[slot, pl.ds(0, rows), pl.ds(p * width, width)],
                                 sem.at[k, slot])


def _gather(pt_ref, b, caches, bufs, sem, slot, n_pages, start):
    for k, (cache, buf) in enumerate(zip(caches, bufs)):
        for p in range(n_pages):
            cp = _page_copy(cache, buf, sem, k, slot, pt_ref[b, p] if start else 0, p)
            if start:
                cp.start()
            else:
                cp.wait()


def _sample_row_runner(pt_ref, step, rows_per_step, n_rows, caches, bufs, sem, n_pages, compute, before_first=None):
    def run(k):
        r = step * rows_per_step + k
        slot = lax.rem(r, 2)

        @pl.when(r == 0)
        def _():
            if before_first is not None:
                before_first()
            _gather(pt_ref, r, caches, bufs, sem, slot, n_pages, True)

        @pl.when(r + 1 < n_rows)
        def _():
            _gather(pt_ref, r + 1, caches, bufs, sem, 1 - slot, n_pages, True)

        _gather(pt_ref, r, caches, bufs, sem, slot, n_pages, False)
        compute(k, slot)

    return run


def _sample_softmax(tiles, s_new, vt_tiles, v_new):
    m = s_new
    for s in tiles:
        m = jnp.maximum(m, jnp.max(s, axis=1, keepdims=True))
    p_new = jnp.exp(s_new - m)
    l = p_new
    o = p_new * v_new
    for s, vt in zip(tiles, vt_tiles):
        p = jnp.exp(s - m)
        l = l + jnp.sum(p, axis=1, keepdims=True)
        o = o + _dot_nt(p.astype(BF16), vt)
    return o / l


N_KEY_CHUNKS = 8


def _dsa_sample_scores(qi, wi, kin, bik, slot, sc_out, sn_out, past):
    nq = N_KEY_CHUNKS
    ch = past // nq
    qib = qi.astype(BF16)
    wrow = wi * (IDX_DIM ** -0.5 * IDX_HEADS ** -0.5)
    eye = _iota((IDX_HEADS, IDX_HEADS), 0) == _iota((IDX_HEADS, IDX_HEADS), 1)
    wcol = jnp.sum(jnp.where(eye, jnp.broadcast_to(wrow, (IDX_HEADS, IDX_HEADS)), 0.0), axis=1, keepdims=True)
    lg_new = jnp.sum(qi * kin, axis=1, keepdims=True)
    s_new = jnp.sum(jnp.maximum(lg_new, 0.0) * wcol, axis=0, keepdims=True)
    sn_out[...] = jnp.broadcast_to(s_new, sn_out.shape)
    for q in range(nq):
        kq = bik[slot, :, q * ch:(q + 1) * ch].astype(BF16)
        lg = jnp.maximum(_dot(qib, kq), 0.0)
        sc_out[q:q + 1, :] = jnp.sum(lg * wcol, axis=0, keepdims=True)


def _dsa_select_kernel(sc_ref, sn_ref, vstar_ref, jcut_ref, sct_s, extra_s, *, n_sel, past):
    db = sc_ref.shape[0]
    tc = Q_BLOCK
    nchunk = past // tc
    for c in range(nchunk):
        sct_s[c * tc:(c + 1) * tc, :] = sc_ref[:, c * tc:(c + 1) * tc].T
    new_row = sn_ref[...].T[0:1, :]
    extra_s[...] = jnp.where(_iota(extra_s.shape, 0) == 0, jnp.broadcast_to(new_row, extra_s.shape), NEG)
    krow = _iota((tc, db), 0)
    erow = _iota(extra_s.shape, 0)

    def sweep(fn, init):
        def body(c, carry):
            return fn(sct_s[pl.ds(pl.multiple_of(c * tc, tc), tc), :], c * tc + krow, carry)
        return fn(extra_s[...], past + erow, lax.fori_loop(0, nchunk, body, init))

    def count_gt(x):
        part = sweep(lambda s, c, a: a + jnp.sum(jnp.where(s > x, 1.0, 0.0).reshape(s.shape[0] // 8, 8, db), axis=0),
                     jnp.zeros((8, db), F32))
        return jnp.sum(part, axis=0, keepdims=True)

    small = jnp.zeros((1, db), jnp.int32) > 0
    vstar, jcut = _select_topk(sweep, count_gt, n_sel, small, db, past + 1)
    vstar_ref[...] = vstar
    jcut_ref[...] = jcut


def _dsa_select(scores, snew, past):
    db = scores.shape[0]
    n_sel = float(min(IDX_TOPK, (past + 1) // 4))
    full = lambda a: pl.BlockSpec(a.shape, lambda i: (0, 0))
    return pl.pallas_call(
        functools.partial(_dsa_select_kernel, n_sel=n_sel, past=past),
        grid=(1,),
        in_specs=[full(scores), full(snew)],
        out_specs=[pl.BlockSpec((1, db), lambda i: (0, 0)), pl.BlockSpec((1, db), lambda i: (0, 0))],
        out_shape=[jax.ShapeDtypeStruct((1, db), F32), jax.ShapeDtypeStruct((1, db), jnp.int32)],
        scratch_shapes=[pltpu.VMEM((past, db), F32), pltpu.VMEM((8, db), F32)],
        compiler_params=_params(1),
        name="dsa_select",
    )(scores, snew)


def _dsa_sample_attend(sc, s_new, vstar, jcut, qbd, kan, van, bak, bav, slot, past):
    nq = N_KEY_CHUNKS
    ch = past // nq
    kidx = _iota((nq, ch), 0) * ch + _iota((nq, ch), 1)
    bias = jnp.where(sc > vstar, 0.0, jnp.where(sc == vstar, jnp.where(kidx < jcut, 0.0, NEG), NEG))
    bias_new = jnp.where(s_new > vstar, 0.0, jnp.where(s_new == vstar, jnp.where(past < jcut, 0.0, NEG), NEG))

    qs = qbd * (HEAD_DIM ** -0.5)
    sn = jnp.sum(qs * kan, axis=1, keepdims=True) + bias_new
    qsb = qs.astype(BF16)
    tiles, vts = [], []
    for q in range(nq):
        kq = bak[slot, :, q * ch:(q + 1) * ch].astype(BF16)
        tiles.append(_dot(qsb, kq) + bias[q:q + 1, :])
        vts.append(bav[slot, :, q * ch:(q + 1) * ch].astype(BF16))
    return _sample_softmax(tiles, sn, vts, van)


def _moba_sample_row(qbd, kbn, vbn, bbk, bbv, slot, n_blk, past):
    nq = N_KEY_CHUNKS
    ch = past // nq
    nblk = past // MOBA_BLOCK
    bpc = ch // MOBA_BLOCK
    qs = qbd * (HEAD_DIM ** -0.5)
    qsb = qs.astype(BF16)
    sn = jnp.sum(qs * kbn, axis=1, keepdims=True)
    bcol = _iota((B_HEADS, nblk), 1).astype(F32)

    raw, vts = [], []
    gate = jnp.zeros((B_HEADS, nblk), F32)
    for q in range(nq):
        s = _dot(qsb, bbk[slot, :, q * ch:(q + 1) * ch].astype(BF16))
        raw.append(s)
        vts.append(bbv[slot, :, q * ch:(q + 1) * ch].astype(BF16))
        for k in range(bpc):
            g = jnp.sum(s[:, k * MOBA_BLOCK:(k + 1) * MOBA_BLOCK], axis=1, keepdims=True) * (1.0 / MOBA_BLOCK)
            gate = jnp.where(bcol == float(q * bpc + k), g, gate)
    rank = jnp.zeros((B_HEADS, nblk), F32)
    for m in range(nblk):
        gm = gate[:, m:m + 1]
        rank = rank + jnp.where(gm > gate, 1.0, jnp.where(gm == gate, jnp.where(bcol > float(m), 1.0, 0.0), 0.0))
    selm = jnp.where(rank < n_blk, 1.0, 0.0)
    tiles = []
    for q in range(nq):
        bias = jnp.concatenate(
            [jnp.broadcast_to(jnp.where(selm[:, q * bpc + k:q * bpc + k + 1] > 0.0, 0.0, NEG),
                              (B_HEADS, MOBA_BLOCK)) for k in range(bpc)], axis=1)
        tiles.append(raw[q] + bias)
    return _sample_softmax(tiles, sn, vts, vbn)


def _rider(page_table, small_inputs, caches, batch, nq):
    db, n_pages = page_table.shape
    page = caches[0].shape[2]
    past = n_pages * page
    n_steps = batch * nq
    assert past % (N_KEY_CHUNKS * MOBA_BLOCK) == 0 and page % LANE == 0 and db % n_steps == 0
    rps = db // n_steps
    per_step = lambda a: pl.BlockSpec((rps,) + a.shape[1:], lambda b, i, pt: (b * nq + i, 0, 0))
    return dict(
        in_specs=[per_step(a) for a in small_inputs] + [pl.BlockSpec(memory_space=pl.ANY)] * len(caches),
        out_spec=pl.BlockSpec((rps, 8, LANE), lambda b, i, pt: (b * nq + i, 0, 0)),
        out_shape=jax.ShapeDtypeStruct((db, 8, LANE), F32),
        scratch=[pltpu.VMEM((2, LANE, past), F32) for _ in caches] + [pltpu.SemaphoreType.DMA((len(caches), 2))],
        static=dict(n_rows=db, rows_per_step=rps, n_pages=n_pages, past=past),
    )


def _merge_kernel(x_ref, oa_ref, ob_ref, sga_ref, sgb_ref, wpa_ref, wpb_ref, wo_ref, y_ref):
    pa = _dot(oa_ref[...].astype(BF16), wpa_ref[...])
    pb = _dot(ob_ref[...].astype(BF16), wpb_ref[...])
    merged = sga_ref[...] * pa + sgb_ref[...] * pb
    y_ref[...] = x_ref[...] + _dot(merged.astype(BF16), wo_ref[...])


def _merge(x2d, oa, ob, sga, sgb, wpa, wpb, wo):
    n, d = x2d.shape
    tm = min(ROW_BLOCK, n)
    row = lambda c: pl.BlockSpec((tm, c), lambda i: (i, 0))
    const = lambda a: pl.BlockSpec(a.shape, lambda i: (0, 0))
    return pl.pallas_call(
        _merge_kernel,
        grid=(n // tm,),
        in_specs=[row(d), row(A_W), row(B_W), row(d), row(d), const(wpa), const(wpb), const(wo)],
        out_specs=row(d),
        out_shape=jax.ShapeDtypeStruct((n, d), F32),
        compiler_params=_params(1),
        name="merge",
    )(x2d, oa, ob, sga, sgb, wpa, wpb, wo)


FFN_COL_CHUNK = 1408


def _ffn_prompt_kernel(x_ref, oa_ref, ob_ref, sga_ref, sgb_ref, wpa_ref, wpb_ref, wo_ref,
                       n2_ref, wup_ref, cw_ref, cb_ref, wdn_ref, prev_ref, y_ref, tail_ref, ext_s, *, d_ff):
    tm = x_ref.shape[0]
    i = pl.program_id(1)

    @pl.when(i == 0)
    def _():
        ext_s[6:8, :] = prev_ref[...]

    pa = _dot(oa_ref[...].astype(BF16), wpa_ref[...])
    pb = _dot(ob_ref[...].astype(BF16), wpb_ref[...])
    merged = sga_ref[...] * pa + sgb_ref[...] * pb
    x = x_ref[...] + _dot(merged.astype(BF16), wo_ref[...])
    xn = _rms(x, n2_ref[...]).astype(BF16)
    fc = FFN_COL_CHUNK
    for c in range(2 * d_ff // fc):
        ext_s[8:8 + tm, c * fc:(c + 1) * fc] = _dot(xn, wup_ref[:, c * fc:(c + 1) * fc])

    def conv(lo):
        cols = slice(lo, lo + fc)
        out = cb_ref[:, cols] + ext_s[6:6 + tm, cols] * cw_ref[0:1, cols]
        out = out + ext_s[7:7 + tm, cols] * cw_ref[1:2, cols]
        return out + ext_s[8:8 + tm, cols] * cw_ref[2:3, cols]

    y = x
    for j in range(d_ff // fc):
        a = conv(j * fc)
        g = conv(d_ff + j * fc)
        act = (a / (1.0 + jnp.exp(-a)) * g).astype(BF16)
        y = y + _dot(act, wdn_ref[j * fc:(j + 1) * fc, :])
    y_ref[...] = y
    tail = ext_s[tm + 6:tm + 8, :]
    tail_ref[...] = tail
    ext_s[6:8, :] = tail


def _merge_ffn_prompt(x2d, oa, ob, sga, sgb, wpa, wpb, wo, n2, wup, cw, cb, wdn, prev, batch, t):
    n, d = x2d.shape
    d_ff = wdn.shape[0]
    assert d_ff % FFN_COL_CHUNK == 0
    tm = ROW_BLOCK
    nt = t // tm
    row = lambda c: pl.BlockSpec((tm, c), lambda b, i: (b * nt + i, 0))
    const = lambda a: pl.BlockSpec(a.shape, lambda b, i: (0, 0), pipeline_mode=pl.Buffered(1))
    per_b = pl.BlockSpec((None, 2, 2 * d_ff), lambda b, i: (b, 0, 0))
    return pl.pallas_call(
        functools.partial(_ffn_prompt_kernel, d_ff=d_ff),
        grid=(batch, nt),
        in_specs=[row(d), row(A_W), row(B_W), row(d), row(d), const(wpa), const(wpb), const(wo),
                  const(n2), const(wup), const(cw), const(cb), const(wdn), per_b],
        out_specs=[row(d), per_b],
        out_shape=[jax.ShapeDtypeStruct((n, d), F32), jax.ShapeDtypeStruct((batch, 2, 2 * d_ff), F32)],
        scratch_shapes=[pltpu.VMEM((tm + 8, 2 * d_ff), F32)],
        compiler_params=_params(2),
        name="merge_ffn_prompt",
    )(x2d, oa, ob, sga, sgb, wpa, wpb, wo, n2, wup, cw, cb, wdn, prev)


def _ffn_sample_kernel(x_ref, n2_ref, wup_ref, cw_ref, cb_ref, wdn_ref, s0_ref, s1_ref, y_ref, up_ref, *, d_ff):
    x = x_ref[...]
    xn = _rms(x, n2_ref[...]).astype(BF16)
    fc = FFN_COL_CHUNK
    for c in range(2 * d_ff // fc):
        up_ref[:, c * fc:(c + 1) * fc] = _dot(xn, wup_ref[:, c * fc:(c + 1) * fc])

    def conv(lo):
        cols = slice(lo, lo + fc)
        out = cb_ref[:, cols] + s0_ref[:, cols] * cw_ref[0:1, cols]
        out = out + s1_ref[:, cols] * cw_ref[1:2, cols]
        return out + up_ref[:, cols] * cw_ref[2:3, cols]

    y = x
    for j in range(d_ff // fc):
        a = conv(j * fc)
        g = conv(d_ff + j * fc)
        act = (a / (1.0 + jnp.exp(-a)) * g).astype(BF16)
        y = y + _dot(act, wdn_ref[j * fc:(j + 1) * fc, :])
    y_ref[...] = y


def _ffn_sample(x2d, n2, wup, cw, cb, wdn, s0, s1):
    n, d = x2d.shape
    d_ff = wdn.shape[0]
    full = lambda a: pl.BlockSpec(a.shape, lambda i: (0, 0), pipeline_mode=pl.Buffered(1))
    args = (x2d, n2, wup, cw, cb, wdn, s0, s1)
    return pl.pallas_call(
        functools.partial(_ffn_sample_kernel, d_ff=d_ff),
        grid=(1,),
        in_specs=[full(a) for a in args],
        out_specs=[pl.BlockSpec((n, d), lambda i: (0, 0)), pl.BlockSpec((n, 2 * d_ff), lambda i: (0, 0))],
        out_shape=[jax.ShapeDtypeStruct((n, d), F32), jax.ShapeDtypeStruct((n, 2 * d_ff), F32)],
        compiler_params=_params(1),
        name="ffn_sample",
    )(*args)


def _rope_tables(pos):
    half = HEAD_DIM // 2
    inv = ROPE_THETA ** (-jnp.arange(half, dtype=F32) / half)
    ang = pos.astype(F32)[:, None] * inv[None, :]
    cos = jnp.cos(ang)
    sin = jnp.sin(ang)
    return jnp.tile(cos, (1, 4)), jnp.tile(jnp.concatenate([-sin, sin], axis=1), (1, 2))


def _layout_w_in(w_in, d_model):
    splits = (A_W, A_KV_W, A_KV_W, I_W, IDX_DIM, IDX_HEADS, B_W, B_KV_W, B_KV_W, d_model, d_model)
    offs = [0]
    for s in splits:
        offs.append(offs[-1] + s)
    p = [w_in[:, offs[k]:offs[k + 1]] for k in range(len(splits))]
    qa, ka, va, qi, ki, wi, qb, kb, vb, ga, gb = p
    pad = jnp.zeros((w_in.shape[0], LANE - IDX_DIM - IDX_HEADS), w_in.dtype)
    return jnp.concatenate([qa, ka, va, qi, qb, kb, vb, ga, gb, ki, wi, pad], axis=1).astype(BF16)


def _layout_gains(q_norm_a, k_norm_a, k_norm_idx, q_norm_b, k_norm_b, d_model):
    one = lambda n: jnp.ones((n,), F32)
    return jnp.concatenate([
        jnp.tile(q_norm_a, A_HEADS), jnp.tile(k_norm_a, A_KV_HEADS), one(A_KV_W), one(I_W),
        jnp.tile(q_norm_b, B_HEADS), jnp.tile(k_norm_b, B_KV_HEADS), one(B_KV_W),
        one(2 * d_model), k_norm_idx, one(LANE - IDX_DIM)])[None, :]


def _block_diag_q(q, heads_per_kv):
    z = jnp.zeros_like(q)
    low = jnp.concatenate([q, z], axis=-1)
    high = jnp.concatenate([z, q], axis=-1)
    is_low = (jnp.arange(q.shape[1]) // heads_per_kv == 0)[None, :, None]
    return jnp.where(is_low, low, high)


def _pick_kv(o, heads_per_kv):
    n = o.shape[0]
    return jnp.concatenate([o[:, :heads_per_kv, :HEAD_DIM].reshape(n, -1),
                            o[:, heads_per_kv:, HEAD_DIM:].reshape(n, -1)], axis=1)


def _sample_operands(qa_s, ka_s, va_s, qi_s, qb_s, kb_s, vb_s, kw_s, c_ik, c_ak, c_av, c_bk, c_bv):
    db = qa_s.shape[0]
    n_pool, page = c_ak.shape[0], c_ak.shape[1]
    hpk_a = A_HEADS // A_KV_HEADS
    hpk_b = B_HEADS // B_KV_HEADS
    qi_pad = jnp.pad(qi_s.reshape(db, IDX_HEADS, IDX_DIM), ((0, 0), (0, 0), (0, LANE - IDX_DIM)))
    wi_s = kw_s[:, IDX_DIM:IDX_DIM + IDX_HEADS].reshape(db, 1, IDX_HEADS)
    qa_bd = _block_diag_q(qa_s.reshape(db, A_HEADS, HEAD_DIM), hpk_a)
    qb_bd = _block_diag_q(qb_s.reshape(db, B_HEADS, HEAD_DIM), hpk_b)
    kin = jnp.where(jnp.arange(LANE) < IDX_DIM, kw_s, 0.0).reshape(db, 1, LANE)
    pages_t = lambda c: jnp.moveaxis(c, 1, -1).reshape(n_pool, -1, page)
    dsa = ([qa_bd, ka_s.reshape(db, 1, LANE), va_s.reshape(db, 1, LANE)], [pages_t(c_ak), pages_t(c_av)])
    moba = ([qb_bd, kb_s.reshape(db, 1, LANE), vb_s.reshape(db, 1, LANE), qi_pad, wi_s, kin],
            [pages_t(c_bk), pages_t(c_bv), pages_t(c_ik)])
    return dsa, moba


def kernel(x_prompt, x_sample, cache_a_k, cache_a_v, cache_idx_k, cache_b_k, cache_b_v, state_conv, page_table, norm1, w_in, q_norm_a, k_norm_a, k_norm_idx, q_norm_b, k_norm_b, w_proj_a, w_proj_b, w_out, norm2, w_up, conv_w, conv_b, w_down):
    batch, t, d = x_prompt.shape
    db, ds, _ = x_sample.shape
    depth = norm1.shape[0]
    assert depth == 1 and ds == 1
    n_pool, page = cache_a_k.shape[1], cache_a_k.shape[2]
    past = page_table.shape[1] * page
    l = 0
    w = _layout_w_in(w_in[l], d)
    gains = _layout_gains(q_norm_a[l], k_norm_a[l], k_norm_idx[l], q_norm_b[l], k_norm_b[l], d)
    wpa, wpb, wo = w_proj_a[l].astype(BF16), w_proj_b[l].astype(BF16), w_out[l].astype(BF16)
    wup, wdn = w_up[l].astype(BF16), w_down[l].astype(BF16)
    n1, n2 = norm1[l][None, :], norm2[l][None, :]
    cw, cb = conv_w[l], conv_b[l][None, :]
    d_ff = wdn.shape[0]

    xp = x_prompt.reshape(batch * t, d)
    cos_p, sin_p = _rope_tables(jnp.arange(t, dtype=jnp.int32))
    qa, ka, _, qi, qb, kb, _, sga, sgb, kw, kat, vat, kbt, vbt, kit = _proj(xp, n1, w, gains, cos_p, sin_p, t)
    xs = x_sample.reshape(db, d)
    cos_s, sin_s = _rope_tables(jnp.full((db,), past, jnp.int32))
    (qa_s, ka_s, va_s, qi_s, qb_s, kb_s, vb_s, sga_s, sgb_s, kw_s,
     kat_s, vat_s, kbt_s, vbt_s, kit_s) = _proj(xs, n1, w, gains, cos_s, sin_s, db)

    dsa_s, moba_s = _sample_operands(qa_s, ka_s, va_s, qi_s, qb_s, kb_s, vb_s, kw_s,
                                     cache_idx_k[l], cache_a_k[l], cache_a_v[l], cache_b_k[l], cache_b_v[l])
    ob, ob_s, sc_s, sn_s = _moba_branch(qb, kb, vbt, batch, t, page_table, *moba_s)
    vstar_s, jcut_s = _dsa_select(sc_s.reshape(db, past), sn_s.reshape(db, LANE), past)
    per_row = lambda a: jnp.broadcast_to(a.reshape(db, 1, 1), (db, 1, LANE))
    oa, oa_s = _dsa_branch(qi, qa, kw, ka, vat, batch, t, page_table,
                           [sc_s, sn_s, per_row(vstar_s), per_row(jcut_s)] + dsa_s[0], dsa_s[1])
    oa_s = _pick_kv(oa_s, A_HEADS // A_KV_HEADS)
    ob_s = _pick_kv(ob_s, B_HEADS // B_KV_HEADS)

    yp, p_conv = _merge_ffn_prompt(xp, oa, ob, sga, sgb, wpa, wpb, wo, n2, wup, cw, cb, wdn,
                                   jnp.zeros((batch, 2, 2 * d_ff), F32), batch, t)
    x1_s = _merge(xs, oa_s, ob_s, sga_s, sgb_s, wpa, wpb, wo)
    ys, up_s = _ffn_sample(x1_s, n2, wup, cw, cb, wdn, state_conv[l, :, 0], state_conv[l, :, 1])

    def rows5(a, n, s, h):
        a = a.reshape(a.shape[0], h, HEAD_DIM, a.shape[2])
        return jnp.transpose(a, (0, 3, 1, 2)).reshape(1, n, s, h, HEAD_DIM)

    def rows4(a, n, s):
        return jnp.transpose(a, (0, 2, 1)).reshape(1, n, s, IDX_DIM)

    return (
        yp.reshape(batch, t, d), ys.reshape(db, 1, d),
        rows5(kat, batch, t, A_KV_HEADS), rows5(vat, batch, t, A_KV_HEADS), rows4(kit, batch, t),
        rows5(kbt, batch, t, B_KV_HEADS), rows5(vbt, batch, t, B_KV_HEADS),
        p_conv[None],
        rows5(kat_s, db, 1, A_KV_HEADS), rows5(vat_s, db, 1, A_KV_HEADS), rows4(kit_s, db, 1),
        rows5(kbt_s, db, 1, B_KV_HEADS), rows5(vbt_s, db, 1, B_KV_HEADS),
        jnp.stack([state_conv[l, :, 1], up_s], axis=1)[None],
    )
```
